```python
import jax, jax.numpy as jnp
from jax import lax
import numpy as np


D_MODEL = 1024
BATCH = 8
SEQ = 2048
DEPTH = 4

GRID_W = 64
CTX_LEN = 256
N_MIXERS = 4
EPS = 1e-6
NEG = -1e30
ROPE_THETA = 10000.0
BLOCK = 128
D_FF = 2816
FFN_RESIDUAL_WEIGHT = 0.5
N_MOD = 9
N_A = (DEPTH + 3) // 4
N_B = (DEPTH + 2) // 4
N_C = (DEPTH + 1) // 4
N_D = DEPTH // 4
NA_HEADS = 16
NA_HEAD_DIM = D_MODEL // NA_HEADS
NA_WIN_R = 8
NA_WIN_C = 16
NA_Q_BLK_C = 16
NA_KEY_BLK_C = 32
SW_HEADS = 16
SW_KV_HEADS = 2
SW_HEAD_DIM = 64
SW_WINDOW = 128
SW_QKV_DIM = (SW_HEADS + 2 * SW_KV_HEADS) * SW_HEAD_DIM
CONV_WIDTH = 31
GA_HEADS = 8
GA_KV_HEADS = 2
GA_HEAD_DIM = 128
GA_QKV_DIM = (GA_HEADS + 2 * GA_KV_HEADS) * GA_HEAD_DIM

kernel_name = 'hybrid_interleaved_diffusion_backbone'


def rmsnorm(x, g):
    xf = x.astype(jnp.float32)
    y = xf * lax.rsqrt(jnp.mean(xf * xf, axis=-1, keepdims=True) + EPS)
    return (y * g.astype(jnp.float32)).astype(x.dtype)


def layernorm(x, g, b):
    xf = x.astype(jnp.float32)
    mu = jnp.mean(xf, axis=-1, keepdims=True)
    var = jnp.mean(jnp.square(xf - mu), axis=-1, keepdims=True)
    y = (xf - mu) * lax.rsqrt(var + EPS)
    return (y * g.astype(jnp.float32) + b.astype(jnp.float32)).astype(x.dtype)


def modulated_norm(x, g, shift, scale):
    return rmsnorm(x, g) * (1.0 + scale) + shift


def gated_residual(x, y, g_post, gate, weight):
    return x + weight * gate * rmsnorm(y, g_post)


def swiglu(h, w_gate, w_up, w_down):
    return (jax.nn.silu(h @ w_gate) * (h @ w_up)) @ w_down


def macaron_half_ffn(x, m, slot, g_pre, g_post, w_gate, w_up, w_down):
    h = modulated_norm(x, g_pre, m[:, :, 3 * slot], m[:, :, 3 * slot + 1])
    y = swiglu(h, w_gate, w_up, w_down)
    return gated_residual(x, y, g_post, m[:, :, 3 * slot + 2], FFN_RESIDUAL_WEIGHT)


def axial_rope_angles(n_tokens, head_dim):
    t = jnp.arange(n_tokens)
    row = (t // GRID_W).astype(jnp.float32)
    col = (t % GRID_W).astype(jnp.float32)
    n_pairs_axis = head_dim // 4
    freq = ROPE_THETA ** (-jnp.arange(n_pairs_axis, dtype=jnp.float32) / n_pairs_axis)
    ang = jnp.concatenate([row[:, None] * freq, col[:, None] * freq], axis=-1)
    return jnp.cos(ang), jnp.sin(ang)


def apply_rope(x, cos, sin):
    xf = x.astype(jnp.float32).reshape(x.shape[:-1] + (-1, 2))
    x1, x2 = xf[..., 0], xf[..., 1]
    c = cos[None, :, None, :]
    s = sin[None, :, None, :]
    out = jnp.stack([x1 * c - x2 * s, x1 * s + x2 * c], axis=-1).reshape(x.shape)
    return out.astype(x.dtype)


def softmax_with_sink(s, sink):
    sk = jnp.broadcast_to(sink.astype(jnp.float32).reshape(1, SW_KV_HEADS, -1, 1, 1), s.shape[:-1] + (1,))
    return jax.nn.softmax(jnp.concatenate([sk, s], axis=-1), axis=-1)[..., 1:]


def neighborhood_attention(h_lat, h_ctx, w_qkv, w_o, rpb, with_ctx_out):
    B, S, _ = h_lat.shape
    C = h_ctx.shape[1]
    H, dh = NA_HEADS, NA_HEAD_DIM
    qd = H * dh
    rows = S // GRID_W
    kr = min(NA_WIN_R, rows)
    scale = dh ** -0.5
    n_cb = GRID_W // NA_Q_BLK_C
    qkv = (h_lat @ w_qkv).reshape(B, rows, GRID_W, 3, H, dh)
    q, k, v = qkv[:, :, :, 0], qkv[:, :, :, 1], qkv[:, :, :, 2]
    kv_c = (h_ctx @ w_qkv[:, qd:]).reshape(B, C, 2, H, dh)
    k_c, v_c = kv_c[:, :, 0], kv_c[:, :, 1]
    jcol = np.arange(GRID_W)
    qcol = jcol.reshape(n_cb, NA_Q_BLK_C)
    win_start = np.clip(jcol - NA_WIN_C // 2, 0, GRID_W - NA_WIN_C).reshape(n_cb, NA_Q_BLK_C)
    blk_start = np.clip(qcol[:, 0] - NA_WIN_C // 2, 0, GRID_W - NA_KEY_BLK_C)
    key_cols = blk_start[:, None] + np.arange(NA_KEY_BLK_C)
    kc3 = key_cols[:, None, :]
    col_valid = (kc3 >= win_start[:, :, None]) & (kc3 < win_start[:, :, None] + NA_WIN_C)
    col_idx = np.clip(kc3 - qcol[:, :, None] + NA_WIN_C - 1, 0, 2 * NA_WIN_C - 2)
    n_nb = kr * NA_KEY_BLK_C

    def row_fn(r):
        rs = jnp.clip(r - NA_WIN_R // 2, 0, rows - kr)
        k_blk = lax.dynamic_slice_in_dim(k, rs, kr, axis=1)[:, :, key_cols]
        v_blk = lax.dynamic_slice_in_dim(v, rs, kr, axis=1)[:, :, key_cols]
        q_r = lax.dynamic_index_in_dim(q, r, axis=1, keepdims=False).reshape(B, n_cb, NA_Q_BLK_C, H, dh)
        row_idx = rs + jnp.arange(kr) - r + NA_WIN_R - 1
        bias = rpb[:, row_idx][:, :, col_idx].transpose(2, 0, 3, 1, 4)
        s_nb = jnp.einsum('bnqhd,brnkhd->bnhqrk', q_r, k_blk, preferred_element_type=jnp.float32) * scale
        s_nb = jnp.where(col_valid[None, :, None, :, None, :], s_nb + bias.astype(jnp.float32), NEG)
        s_c = jnp.einsum('bnqhd,bchd->bnhqc', q_r, k_c, preferred_element_type=jnp.float32) * scale
        s = jnp.concatenate([s_nb.reshape(B, n_cb, H, NA_Q_BLK_C, n_nb), s_c], axis=-1)
        p = jax.nn.softmax(s, axis=-1).astype(v.dtype)
        p_nb = p[..., :n_nb].reshape(B, n_cb, H, NA_Q_BLK_C, kr, NA_KEY_BLK_C)
        o = (jnp.einsum('bnhqrk,brnkhd->bnqhd', p_nb, v_blk)
             + jnp.einsum('bnhqc,bchd->bnqhd', p[..., n_nb:], v_c))
        return o.reshape(B, GRID_W, qd)

    o = lax.map(row_fn, jnp.arange(rows))
    y_lat = o.transpose(1, 0, 2, 3).reshape(B, S, qd) @ w_o
    y_ctx = None
    if with_ctx_out:
        q_c = (h_ctx @ w_qkv[:, :qd]).reshape(B, C, H, dh)
        s = jnp.einsum('bqhd,bkhd->bhqk', q_c, k_c, preferred_element_type=jnp.float32) * scale
        p = jax.nn.softmax(s, axis=-1).astype(v_c.dtype)
        y_ctx = jnp.einsum('bhqk,bkhd->bqhd', p, v_c).reshape(B, C, qd) @ w_o
    return y_lat, y_ctx


def window_gqa_sink(h_lat, h_ctx, w_qkv, w_o, sink, cos, sin, with_ctx_out):
    B, S, _ = h_lat.shape
    C = h_ctx.shape[1]
    H, Hkv, dh = SW_HEADS, SW_KV_HEADS, SW_HEAD_DIM
    G = H // Hkv
    qd, kvd = H * dh, Hkv * dh
    scale = dh ** -0.5
    proj = h_lat @ w_qkv
    q = apply_rope(proj[..., :qd].reshape(B, S, H, dh), cos, sin).reshape(B, S, Hkv, G, dh)
    k = apply_rope(proj[..., qd:qd + kvd].reshape(B, S, Hkv, dh), cos, sin)
    v = proj[..., qd + kvd:].reshape(B, S, Hkv, dh)
    proj_c = h_ctx @ w_qkv[:, qd:]
    k_c = proj_c[..., :kvd].reshape(B, C, Hkv, dh)
    v_c = proj_c[..., kvd:].reshape(B, C, Hkv, dh)
    pad = ((0, 0), (BLOCK, BLOCK), (0, 0), (0, 0))
    k_pad, v_pad = jnp.pad(k, pad), jnp.pad(v, pad)
    nb = S // BLOCK
    q_blocks = q.reshape(B, nb, BLOCK, Hkv, G, dh).swapaxes(0, 1)

    def blk_fn(args):
        n, q_b = args
        start = n * BLOCK
        k_b = lax.dynamic_slice_in_dim(k_pad, start, 3 * BLOCK, axis=1)
        v_b = lax.dynamic_slice_in_dim(v_pad, start, 3 * BLOCK, axis=1)
        qpos = start + jnp.arange(BLOCK)
        kpos = start - BLOCK + jnp.arange(3 * BLOCK)
        valid = ((jnp.abs(kpos[None, :] - qpos[:, None]) <= SW_WINDOW)
                 & (kpos >= 0)[None, :] & (kpos < S)[None, :])
        s_w = jnp.einsum('bqkgd,bjkd->bkgqj', q_b, k_b, preferred_element_type=jnp.float32) * scale
        s_w = jnp.where(valid, s_w, NEG)
        s_c = jnp.einsum('bqkgd,bjkd->bkgqj', q_b, k_c, preferred_element_type=jnp.float32) * scale
        p = softmax_with_sink(jnp.concatenate([s_w, s_c], axis=-1), sink).astype(v.dtype)
        o = (jnp.einsum('bkgqj,bjkd->bqkgd', p[..., :3 * BLOCK], v_b)
             + jnp.einsum('bkgqj,bjkd->bqkgd', p[..., 3 * BLOCK:], v_c))
        return o.reshape(B, BLOCK, qd)

    o = lax.map(blk_fn, (jnp.arange(nb), q_blocks))
    y_lat = o.swapaxes(0, 1).reshape(B, S, qd) @ w_o
    y_ctx = None
    if with_ctx_out:
        q_c = (h_ctx @ w_qkv[:, :qd]).reshape(B, C, Hkv, G, dh)
        s = jnp.einsum('bqkgd,bjkd->bkgqj', q_c, k_c, preferred_element_type=jnp.float32) * scale
        p = softmax_with_sink(s, sink).astype(v_c.dtype)
        y_ctx = jnp.einsum('bkgqj,bjkd->bqkgd', p, v_c).reshape(B, C, qd) @ w_o
    return y_lat, y_ctx


def conformer_conv(h, w_pw1, b_pw1, w_dw, b_dw, ln_g, ln_b, w_pw2, b_pw2):
    D = h.shape[-1]
    u = h @ w_pw1 + b_pw1
    u = u[..., :D] * jax.nn.sigmoid(u[..., D:])
    u = lax.conv_general_dilated(u, w_dw[:, None, :], window_strides=(1,),
                                 padding=[(CONV_WIDTH // 2, CONV_WIDTH // 2)],
                                 dimension_numbers=('NWC', 'WIO', 'NWC'),
                                 feature_group_count=D) + b_dw
    u = jax.nn.silu(layernorm(u, ln_g, ln_b))
    return u @ w_pw2 + b_pw2


def global_gqa(h_lat, h_ctx, w_qkv, w_o, q_norm, k_norm, cos, sin, with_ctx_out):
    B, S, _ = h_lat.shape
    C = h_ctx.shape[1]
    H, Hkv, dh = GA_HEADS, GA_KV_HEADS, GA_HEAD_DIM
    G = H // Hkv
    qd, kvd = H * dh, Hkv * dh
    scale = dh ** -0.5
    proj = h_lat @ w_qkv
    q = apply_rope(rmsnorm(proj[..., :qd].reshape(B, S, H, dh), q_norm), cos, sin)
    k = apply_rope(rmsnorm(proj[..., qd:qd + kvd].reshape(B, S, Hkv, dh), k_norm), cos, sin)
    v = proj[..., qd + kvd:].reshape(B, S, Hkv, dh)
    proj_c = h_ctx @ w_qkv[:, qd:]
    k_c = rmsnorm(proj_c[..., :kvd].reshape(B, C, Hkv, dh), k_norm)
    v_c = proj_c[..., kvd:].reshape(B, C, Hkv, dh)
    k_all = jnp.concatenate([k, k_c], axis=1)
    v_all = jnp.concatenate([v, v_c], axis=1)
    nb = S // BLOCK
    q_blocks = q.reshape(B, nb, BLOCK, Hkv, G, dh).swapaxes(0, 1)

    def blk_fn(q_b):
        s = jnp.einsum('bqkgd,bjkd->bkgqj', q_b, k_all, preferred_element_type=jnp.float32) * scale
        p = jax.nn.softmax(s, axis=-1).astype(v_all.dtype)
        return jnp.einsum('bkgqj,bjkd->bqkgd', p, v_all).reshape(B, BLOCK, qd)

    o = lax.map(blk_fn, q_blocks)
    y_lat = o.swapaxes(0, 1).reshape(B, S, qd) @ w_o
    y_ctx = None
    if with_ctx_out:
        q_c = rmsnorm((h_ctx @ w_qkv[:, :qd]).reshape(B, C, H, dh), q_norm).reshape(B, C, Hkv, G, dh)
        s = jnp.einsum('bqkgd,bjkd->bkgqj', q_c, k_c, preferred_element_type=jnp.float32) * scale
        p = jax.nn.softmax(s, axis=-1).astype(v_c.dtype)
        y_ctx = jnp.einsum('bkgqj,bjkd->bqkgd', p, v_c).reshape(B, C, qd) @ w_o
    return y_lat, y_ctx


def setup_inputs(seed: int = 0) -> dict:
    key = jax.random.key(seed)
    ks = iter(jax.random.split(key, 32))
    D = D_MODEL

    def nrm(shape, s):
        return jax.random.normal(next(ks), shape, jnp.float32) * s

    return {
        'x': nrm((BATCH, SEQ, D), 1.0),
        'c': nrm((BATCH, D), 1.0),
        'ctx': nrm((BATCH, CTX_LEN, D), 1.0),
        'c_ctx': nrm((D,), 1.0),
        'mod_w': nrm((DEPTH, D, N_MOD * D), 0.5 * D ** -0.5),
        'mod_b': nrm((DEPTH, N_MOD * D), 0.02),
        'norm_g': 1.0 + nrm((DEPTH, 6, D), 0.02),
        'ffn_w_gate': nrm((DEPTH, 2, D, D_FF), D ** -0.5),
        'ffn_w_up': nrm((DEPTH, 2, D, D_FF), D ** -0.5),
        'ffn_w_down': nrm((DEPTH, 2, D_FF, D), D_FF ** -0.5),
        'na_w_qkv': nrm((N_A, D, 3 * NA_HEADS * NA_HEAD_DIM), D ** -0.5),
        'na_w_o': nrm((N_A, NA_HEADS * NA_HEAD_DIM, D), (NA_HEADS * NA_HEAD_DIM) ** -0.5),
        'na_rpb': nrm((N_A, NA_HEADS, 2 * NA_WIN_R - 1, 2 * NA_WIN_C - 1), 0.1),
        'sw_w_qkv': nrm((N_B, D, SW_QKV_DIM), D ** -0.5),
        'sw_w_o': nrm((N_B, SW_HEADS * SW_HEAD_DIM, D), (SW_HEADS * SW_HEAD_DIM) ** -0.5),
        'sw_sink': nrm((N_B, SW_HEADS), 0.5),
        'cv_w_pw1': nrm((N_C, D, 2 * D), D ** -0.5),
        'cv_b_pw1': nrm((N_C, 2 * D), 0.02),
        'cv_w_dw': nrm((N_C, CONV_WIDTH, D), CONV_WIDTH ** -0.5),
        'cv_b_dw': nrm((N_C, D), 0.02),
        'cv_ln_g': 1.0 + nrm((N_C, D), 0.02),
        'cv_ln_b': nrm((N_C, D), 0.02),
        'cv_w_pw2': nrm((N_C, D, D), D ** -0.5),
        'cv_b_pw2': nrm((N_C, D), 0.02),
        'ga_w_qkv': nrm((N_D, D, GA_QKV_DIM), D ** -0.5),
        'ga_w_o': nrm((N_D, GA_HEADS * GA_HEAD_DIM, D), (GA_HEADS * GA_HEAD_DIM) ** -0.5),
        'ga_q_norm': 1.0 + nrm((N_D, GA_HEAD_DIM), 0.02),
        'ga_k_norm': 1.0 + nrm((N_D, GA_HEAD_DIM), 0.02),
    }


def reference(x, c, ctx, c_ctx, mod_w, mod_b, norm_g, ffn_w_gate, ffn_w_up, ffn_w_down,
              na_w_qkv, na_w_o, na_rpb, sw_w_qkv, sw_w_o, sw_sink,
              cv_w_pw1, cv_b_pw1, cv_w_dw, cv_b_dw, cv_ln_g, cv_ln_b, cv_w_pw2, cv_b_pw2,
              ga_w_qkv, ga_w_o, ga_q_norm, ga_k_norm):
    B, S, D = x.shape
    cos_sw, sin_sw = axial_rope_angles(S, SW_HEAD_DIM)
    cos_ga, sin_ga = axial_rope_angles(S, GA_HEAD_DIM)
    xl, xc = x, ctx
    for i in range(DEPTH):
        kind, j = i % N_MIXERS, i // N_MIXERS
        last = i == DEPTH - 1
        ctx_in = not (last and kind == 2)
        ctx_out = not last
        ml = (jax.nn.silu(c) @ mod_w[i] + mod_b[i]).reshape(B, 1, N_MOD, D)
        mc = (jax.nn.silu(c_ctx) @ mod_w[i] + mod_b[i]).reshape(1, 1, N_MOD, D)
        g = norm_g[i]
        xl = macaron_half_ffn(xl, ml, 0, g[0], g[1], ffn_w_gate[i, 0], ffn_w_up[i, 0], ffn_w_down[i, 0])
        if ctx_in:
            xc = macaron_half_ffn(xc, mc, 0, g[0], g[1], ffn_w_gate[i, 0], ffn_w_up[i, 0], ffn_w_down[i, 0])
        hl = modulated_norm(xl, g[2], ml[:, :, 3], ml[:, :, 4])
        hc = modulated_norm(xc, g[2], mc[:, :, 3], mc[:, :, 4]) if ctx_in else None
        if kind == 0:
            yl, yc = neighborhood_attention(hl, hc, na_w_qkv[j], na_w_o[j], na_rpb[j], ctx_out)
        elif kind == 1:
            yl, yc = window_gqa_sink(hl, hc, sw_w_qkv[j], sw_w_o[j], sw_sink[j], cos_sw, sin_sw, ctx_out)
        elif kind == 2:
            yl = conformer_conv(hl, cv_w_pw1[j], cv_b_pw1[j], cv_w_dw[j], cv_b_dw[j],
                                cv_ln_g[j], cv_ln_b[j], cv_w_pw2[j], cv_b_pw2[j])
            yc = conformer_conv(hc, cv_w_pw1[j], cv_b_pw1[j], cv_w_dw[j], cv_b_dw[j],
                                cv_ln_g[j], cv_ln_b[j], cv_w_pw2[j], cv_b_pw2[j]) if ctx_out else None
        else:
            yl, yc = global_gqa(hl, hc, ga_w_qkv[j], ga_w_o[j], ga_q_norm[j], ga_k_norm[j],
                                cos_ga, sin_ga, ctx_out)
        xl = gated_residual(xl, yl, g[3], ml[:, :, 5], 1.0)
        xl = macaron_half_ffn(xl, ml, 2, g[4], g[5], ffn_w_gate[i, 1], ffn_w_up[i, 1], ffn_w_down[i, 1])
        if ctx_out:
            xc = gated_residual(xc, yc, g[3], mc[:, :, 5], 1.0)
            xc = macaron_half_ffn(xc, mc, 2, g[4], g[5], ffn_w_gate[i, 1], ffn_w_up[i, 1], ffn_w_down[i, 1])
    return xl
```

```python
import functools

import numpy as np
import jax
import jax.numpy as jnp
from jax import lax
from jax.experimental import pallas as pl
from jax.experimental.pallas import tpu as pltpu

F32 = jnp.float32
BF16 = jnp.bfloat16

D_MODEL = 1024
DEPTH = 4
GRID_W = 64
N_MIXERS = 4
EPS = 1e-6
NEG = -1e30
ROPE_THETA = 10000.0
D_FF = 2816
FFN_RESIDUAL_WEIGHT = 0.5
N_MOD = 9
NA_HEADS = 16
NA_HEAD_DIM = 64
NA_WIN_R = 8
NA_WIN_C = 16
SW_HEADS = 16
SW_KV_HEADS = 2
SW_HEAD_DIM = 64
SW_WINDOW = 128
SW_BLOCK = 128
CONV_WIDTH = 31
GA_HEADS = 8
GA_KV_HEADS = 2
GA_HEAD_DIM = 128
GA_BLOCK = 128

LANES = 128
CONV_HALO = 16
CONV_TILE = 256
MOD_ROWS = 16
VMEM_LIMIT = 56 * 1024 * 1024

_DN_T = (((1,), (1,)), ((), ()))


def _params(n_axes, vmem=VMEM_LIMIT):
    return pltpu.CompilerParams(dimension_semantics=("arbitrary",) * n_axes, vmem_limit_bytes=vmem)


def _resident(block_shape, index_map):
    return pl.BlockSpec(block_shape, index_map, pipeline_mode=pl.Buffered(1))


def _rms(x):
    return x * lax.rsqrt(jnp.mean(x * x, axis=-1, keepdims=True) + EPS)


def _silu(x):
    return x * jax.nn.sigmoid(x)


def _row_tile(n_lat_rows, n_ctx_rows, seq):
    for t in (512, 256, 128):
        if seq % t == 0 and n_ctx_rows % t == 0 and n_lat_rows % t == 0:
            return t
    raise ValueError("unsupported token counts")


def _mod_kernel(c_ref, w_ref, b_ref, o_ref):
    a = _silu(c_ref[...]).astype(BF16)
    o_ref[0] = jnp.dot(a, w_ref[0].astype(BF16), preferred_element_type=F32) + b_ref[0]


def _modulation(c_all, mod_w, mod_b):
    depth, d, n = mod_w.shape
    tn = 1152
    return pl.pallas_call(
        _mod_kernel,
        out_shape=jax.ShapeDtypeStruct((depth, MOD_ROWS, n), F32),
        grid=(depth, n // tn),
        in_specs=[
            pl.BlockSpec((MOD_ROWS, d), lambda i, j: (0, 0)),
            pl.BlockSpec((1, d, tn), lambda i, j: (i, 0, j)),
            pl.BlockSpec((1, 1, tn), lambda i, j: (i, 0, j)),
        ],
        out_specs=pl.BlockSpec((1, MOD_ROWS, tn), lambda i, j: (i, 0, j)),
        compiler_params=_params(2),
        name="modulation",
    )(c_all, mod_w, mod_b.reshape(depth, 1, n))


def _ffn_kernel(x_ref, m_ref, g_ref, wg_ref, wu_ref, wd_ref, o_ref, *, slot):
    x = x_ref[...]
    shift = m_ref[3 * slot:3 * slot + 1, :]
    scale = m_ref[3 * slot + 1:3 * slot + 2, :]
    gate = m_ref[3 * slot + 2:3 * slot + 3, :]
    g_pre = g_ref[2 * slot:2 * slot + 1, :]
    g_post = g_ref[2 * slot + 1:2 * slot + 2, :]
    h = (_rms(x) * g_pre * (1.0 + scale) + shift).astype(BF16)
    gt = jnp.dot(h, wg_ref[...], preferred_element_type=F32)
    up = jnp.dot(h, wu_ref[...], preferred_element_type=F32)
    a = (_silu(gt) * up).astype(BF16)
    y = jnp.dot(a, wd_ref[...], preferred_element_type=F32)
    o_ref[...] = x + (FFN_RESIDUAL_WEIGHT * gate) * (_rms(y) * g_post)


def _group_map(tm, seq, n_batch):
    return lambda i: jnp.minimum((i * tm) // seq, n_batch)


def _ffn(x, mod, norm_g, wg, wu, wd, *, layer, half, n_rows, tm, seq, n_batch):
    d = x.shape[1]
    f = wg.shape[-1]
    slot = 2 * half
    grp = _group_map(tm, seq, n_batch)
    return pl.pallas_call(
        functools.partial(_ffn_kernel, slot=slot),
        out_shape=jax.ShapeDtypeStruct((n_rows, d), F32),
        grid=(n_rows // tm,),
        in_specs=[
            pl.BlockSpec((tm, d), lambda i: (i, 0)),
            pl.BlockSpec((None, None, N_MOD, d), lambda i: (layer, grp(i), 0, 0)),
            pl.BlockSpec((None, 6, d), lambda i: (layer, 0, 0)),
            _resident((None, None, d, f), lambda i: (layer, half, 0, 0)),
            _resident((None, None, d, f), lambda i: (layer, half, 0, 0)),
            _resident((None, None, f, d), lambda i: (layer, half, 0, 0)),
        ],
        out_specs=pl.BlockSpec((tm, d), lambda i: (i, 0)),
        compiler_params=_params(1),
        name=f"ffn_l{layer}h{half}",
    )(x, mod, norm_g, wg, wu, wd)


def _mixer_norm(x_ref, m_ref, g_ref):
    return (_rms(x_ref[...]) * g_ref[2:3, :] * (1.0 + m_ref[4:5, :]) + m_ref[3:4, :]).astype(BF16)


def _rotate_half(x, half):
    if 2 * half == LANES:
        return pltpu.roll(x, half, axis=1)
    lane = lax.broadcasted_iota(jnp.int32, x.shape, 1)
    return jnp.where(lane % (2 * half) < half, pltpu.roll(x, LANES - half, axis=1), pltpu.roll(x, half, axis=1))


def _na_proj_kernel(x_ref, m_ref, g_ref, w_ref, o_ref):
    h = _mixer_norm(x_ref, m_ref, g_ref)
    p = jnp.dot(h, w_ref[...], preferred_element_type=F32)
    qd = NA_HEADS * NA_HEAD_DIM
    o_ref[:, :qd] = (p[:, :qd] * (NA_HEAD_DIM ** -0.5)).astype(BF16)
    o_ref[:, qd:] = p[:, qd:].astype(BF16)


def _sw_proj_kernel(x_ref, m_ref, g_ref, w_ref, cos_ref, sin_ref, o_ref):
    h = _mixer_norm(x_ref, m_ref, g_ref)
    p = jnp.dot(h, w_ref[...], preferred_element_type=F32)
    cos = cos_ref[...]
    sin = sin_ref[...]
    n_q = SW_HEADS * SW_HEAD_DIM // LANES
    n_k = SW_KV_HEADS * SW_HEAD_DIM // LANES
    for j in range(n_q + n_k):
        xs = p[:, j * LANES:(j + 1) * LANES]
        r = xs * cos + _rotate_half(xs, SW_HEAD_DIM // 2) * sin
        if j < n_q:
            r = r * (SW_HEAD_DIM ** -0.5)
        o_ref[:, j * LANES:(j + 1) * LANES] = r.astype(BF16)
    o_ref[:, (n_q + n_k) * LANES:] = p[:, (n_q + n_k) * LANES:].astype(BF16)


def _ga_proj_kernel(x_ref, m_ref, g_ref, w_ref, cos_ref, sin_ref, qn_ref, kn_ref, o_ref):
    h = _mixer_norm(x_ref, m_ref, g_ref)
    p = jnp.dot(h, w_ref[...], preferred_element_type=F32)
    cos = cos_ref[...]
    sin = sin_ref[...]
    for j in range(GA_HEADS + GA_KV_HEADS):
        xs = _rms(p[:, j * LANES:(j + 1) * LANES])
        if j < GA_HEADS:
            xs = xs * (qn_ref[...] * (GA_HEAD_DIM ** -0.5))
        else:
            xs = xs * kn_ref[...]
        r = xs * cos + _rotate_half(xs, GA_HEAD_DIM // 2) * sin
        o_ref[:, j * LANES:(j + 1) * LANES] = r.astype(BF16)
    n0 = (GA_HEADS + GA_KV_HEADS) * LANES
    o_ref[:, n0:] = p[:, n0:].astype(BF16)


def _cv_proj_kernel(x_ref, m_ref, g_ref, w_ref, b_ref, o_ref):
    h = _mixer_norm(x_ref, m_ref, g_ref)
    u = jnp.dot(h, w_ref[...], preferred_element_type=F32) + b_ref[...]
    d = o_ref.shape[1]
    o_ref[...] = u[:, :d] * jax.nn.sigmoid(u[:, d:])


def _proj(body, x, mod, norm_g, w, extras, extra_specs, *, layer, n_out, out_dtype, tm, seq, n_batch, name):
    n_rows, d = x.shape
    grp = _group_map(tm, seq, n_batch)
    return pl.pallas_call(
        body,
        out_shape=jax.ShapeDtypeStruct((n_rows, n_out), out_dtype),
        grid=(n_rows // tm,),
        in_specs=[
            pl.BlockSpec((tm, d), lambda i: (i, 0)),
            pl.BlockSpec((None, None, N_MOD, d), lambda i: (layer, grp(i), 0, 0)),
            pl.BlockSpec((None, 6, d), lambda i: (layer, 0, 0)),
            _resident(w.shape, lambda i: (0, 0)),
        ] + extra_specs,
        out_specs=pl.BlockSpec((tm, n_out), lambda i: (i, 0)),
        compiler_params=_params(1),
        name=name,
    )(x, mod, norm_g, w, *extras)


def _rope_specs(tm, seq, n_lat_rows):
    n_seq_blocks = seq // tm
    idx = lambda i: (jnp.where(i * tm < n_lat_rows, i % n_seq_blocks, n_seq_blocks), 0)
    return [pl.BlockSpec((tm, LANES), idx), pl.BlockSpec((tm, LANES), idx)]


def _rope_tables(seq, head_dim, tm):
    t = np.arange(seq)
    row = (t // GRID_W).astype(np.float32)
    col = (t % GRID_W).astype(np.float32)
    n_pairs_axis = head_dim // 4
    freq = jnp.asarray(ROPE_THETA, F32) ** (-jnp.arange(n_pairs_axis, dtype=F32) / n_pairs_axis)
    ang = jnp.concatenate([row[:, None] * freq, col[:, None] * freq], axis=-1)
    cos, sin = jnp.cos(ang), jnp.sin(ang)
    reps = LANES // head_dim
    cos_t = jnp.tile(jnp.concatenate([cos, cos], axis=-1), (1, reps))
    sin_t = jnp.tile(jnp.concatenate([-sin, sin], axis=-1), (1, reps))
    cos_t = jnp.concatenate([cos_t, jnp.ones((tm, LANES), F32)], axis=0)
    sin_t = jnp.concatenate([sin_t, jnp.zeros((tm, LANES), F32)], axis=0)
    return cos_t, sin_t


def _half_split_perm(n_heads, head_dim):
    per_head = np.concatenate([np.arange(0, head_dim, 2), np.arange(1, head_dim, 2)])
    return (np.arange(n_heads)[:, None] * head_dim + per_head[None, :]).reshape(-1)


def _softmax_pv(scores, values, extra_logit=None):
    m = scores[0].max(axis=-1, keepdims=True)
    for s in scores[1:]:
        m = jnp.maximum(m, s.max(axis=-1, keepdims=True))
    if extra_logit is not None:
        m = jnp.maximum(m, extra_logit)
    l = None
    o = None
    for s, v in zip(scores, values):
        p = jnp.exp(s - m)
        ls = p.sum(axis=-1, keepdims=True)
        os_ = jnp.dot(p.astype(BF16), v, preferred_element_type=F32)
        l = ls if l is None else l + ls
        o = os_ if o is None else o + os_
    if extra_logit is not None:
        l = l + jnp.exp(extra_logit - m)
    return o / l


def _na_kernel(q_ref, kl_ref, vl_ref, kc_ref, vc_ref, bias_ref, o_ref, *, rows):
    r = pl.program_id(1)
    dh = NA_HEAD_DIM
    n_keys = NA_WIN_R * GRID_W

    @pl.when(r < rows)
    def _():
        rs = jnp.clip(r - NA_WIN_R // 2, 0, rows - NA_WIN_R)
        off = pl.multiple_of(rs * GRID_W, GRID_W)
        for h in range(NA_HEADS):
            sl = slice(h * dh, (h + 1) * dh)
            qh = q_ref[:, sl]
            s_nb = lax.dot_general(qh, kl_ref[pl.ds(off, n_keys), sl], _DN_T,
                                   preferred_element_type=F32) + bias_ref[h]
            s_c = lax.dot_general(qh, kc_ref[:, sl], _DN_T, preferred_element_type=F32)
            o = _softmax_pv([s_nb, s_c], [vl_ref[pl.ds(off, n_keys), sl], vc_ref[:, sl]])
            o_ref[:, sl] = o.astype(BF16)

    @pl.when(r >= rows)
    def _():
        for h in range(NA_HEADS):
            sl = slice(h * dh, (h + 1) * dh)
            s_c = lax.dot_general(q_ref[:, sl], kc_ref[:, sl], _DN_T, preferred_element_type=F32)
            o_ref[:, sl] = _softmax_pv([s_c], [vc_ref[:, sl]]).astype(BF16)


def _na_bias_table(rpb, rows):
    kr = NA_WIN_R
    reps = np.array([0, 1, 2, 3, 4, rows - 3, rows - 2, rows - 1])
    rs = np.clip(reps - kr // 2, 0, rows - kr)
    ridx = rs[:, None] + np.arange(kr)[None, :] - reps[:, None] + kr - 1
    j = np.arange(GRID_W)[:, None]
    kc = np.arange(GRID_W)[None, :]
    ws = np.clip(j - NA_WIN_C // 2, 0, GRID_W - NA_WIN_C)
    valid = (kc >= ws) & (kc < ws + NA_WIN_C)
    cidx = np.clip(kc - j + NA_WIN_C - 1, 0, 2 * NA_WIN_C - 2)
    tab = rpb.astype(F32)[:, ridx][:, :, :, cidx]
    tab = jnp.where(valid[None, None, None], tab, NEG)
    return tab.transpose(1, 0, 3, 2, 4).reshape(8, rpb.shape[0], GRID_W, kr * GRID_W)


def _na_attention(qkv, bias_tab, *, n_batch, seq, ctx_len, with_ctx_out):
    rows = seq // GRID_W
    qd = NA_HEADS * NA_HEAD_DIM
    n_ctx_tiles = ctx_len // GRID_W if with_ctx_out else 0
    lat_tiles = n_batch * rows
    n_rows = (n_batch * seq + n_batch * ctx_len) if with_ctx_out else n_batch * seq

    def q_idx(b, r):
        return (jnp.where(r < rows, b * rows + r, lat_tiles + b * (ctx_len // GRID_W) + (r - rows)), 0)

    def variant(b, r):
        v = jnp.where(r < 4, r, jnp.where(r <= rows - 4, 4, r - (rows - 4) + 4))
        return (jnp.minimum(v, 7), 0, 0, 0)

    ctx_blk0 = (n_batch * seq) // ctx_len
    return pl.pallas_call(
        functools.partial(_na_kernel, rows=rows),
        out_shape=jax.ShapeDtypeStruct((n_rows, qd), BF16),
        grid=(n_batch, rows + n_ctx_tiles),
        in_specs=[
            pl.BlockSpec((GRID_W, qd), q_idx),
            pl.BlockSpec((seq, qd), lambda b, r: (b, 1)),
            pl.BlockSpec((seq, qd), lambda b, r: (b, 2)),
            pl.BlockSpec((ctx_len, qd), lambda b, r: (ctx_blk0 + b, 1)),
            pl.BlockSpec((ctx_len, qd), lambda b, r: (ctx_blk0 + b, 2)),
            pl.BlockSpec((None, NA_HEADS, GRID_W, NA_WIN_R * GRID_W), variant),
        ],
        out_specs=pl.BlockSpec((GRID_W, qd), q_idx),
        compiler_params=_params(2),
        name="na_attention",
    )(qkv, qkv, qkv, qkv, qkv, bias_tab)


def _stack_heads(ref, first_head, n_heads, dh):
    return jnp.concatenate([ref[:, (first_head + g) * dh:(first_head + g + 1) * dh] for g in range(n_heads)], axis=0)


def _sw_kernel(sink_ref, q_ref, kl_ref, vl_ref, kc_ref, vc_ref, o_ref, *, n_blocks, seq):
    n = pl.program_id(1)
    dh = SW_HEAD_DIM
    grp = SW_HEADS // SW_KV_HEADS
    blk = SW_BLOCK
    win = 3 * blk

    def sink_col(kvh):
        return jnp.concatenate([jnp.full((blk, 1), sink_ref[kvh * grp + g], F32) for g in range(grp)], axis=0)

    def store(kvh, o):
        for g in range(grp):
            hd = kvh * grp + g
            o_ref[:, hd * dh:(hd + 1) * dh] = o[g * blk:(g + 1) * blk].astype(BF16)

    @pl.when(n < n_blocks)
    def _():
        start = n * blk
        ws = pl.multiple_of(jnp.clip(start - blk, 0, seq - win), blk)
        qpos = start + lax.broadcasted_iota(jnp.int32, (blk, win), 0)
        kpos = ws + lax.broadcasted_iota(jnp.int32, (blk, win), 1)
        valid = jnp.abs(kpos - qpos) <= SW_WINDOW
        valid = jnp.concatenate([valid] * grp, axis=0)
        for kvh in range(SW_KV_HEADS):
            sl = slice(kvh * dh, (kvh + 1) * dh)
            q = _stack_heads(q_ref, kvh * grp, grp, dh)
            s_w = lax.dot_general(q, kl_ref[pl.ds(ws, win), sl], _DN_T, preferred_element_type=F32)
            s_w = jnp.where(valid, s_w, NEG)
            s_c = lax.dot_general(q, kc_ref[:, sl], _DN_T, preferred_element_type=F32)
            o = _softmax_pv([s_w, s_c], [vl_ref[pl.ds(ws, win), sl], vc_ref[:, sl]], extra_logit=sink_col(kvh))
            store(kvh, o)

    @pl.when(n >= n_blocks)
    def _():
        for kvh in range(SW_KV_HEADS):
            sl = slice(kvh * dh, (kvh + 1) * dh)
            q = _stack_heads(q_ref, kvh * grp, grp, dh)
            s_c = lax.dot_general(q, kc_ref[:, sl], _DN_T, preferred_element_type=F32)
            store(kvh, _softmax_pv([s_c], [vc_ref[:, sl]], extra_logit=sink_col(kvh)))


def _sw_attention(proj, sink, *, n_batch, seq, ctx_len, with_ctx_out):
    qd = SW_HEADS * SW_HEAD_DIM
    blk = SW_BLOCK
    n_blocks = seq // blk
    ctx_tiles = ctx_len // blk
    n_ctx_tiles = ctx_tiles if with_ctx_out else 0
    lat_tiles = n_batch * n_blocks
    n_rows = (n_batch * seq + n_batch * ctx_len) if with_ctx_out else n_batch * seq
    k_col = qd // LANES
    ctx_blk0 = (n_batch * seq) // ctx_len

    def q_idx(b, n, s):
        return (jnp.where(n < n_blocks, b * n_blocks + n, lat_tiles + b * ctx_tiles + (n - n_blocks)), 0)

    grid_spec = pltpu.PrefetchScalarGridSpec(
        num_scalar_prefetch=1,
        grid=(n_batch, n_blocks + n_ctx_tiles),
        in_specs=[
            pl.BlockSpec((blk, qd), q_idx),
            pl.BlockSpec((seq, LANES), lambda b, n, s: (b, k_col)),
            pl.BlockSpec((seq, LANES), lambda b, n, s: (b, k_col + 1)),
            pl.BlockSpec((ctx_len, LANES), lambda b, n, s: (ctx_blk0 + b, k_col)),
            pl.BlockSpec((ctx_len, LANES), lambda b, n, s: (ctx_blk0 + b, k_col + 1)),
        ],
        out_specs=pl.BlockSpec((blk, qd), q_idx),
    )
    return pl.pallas_call(
        functools.partial(_sw_kernel, n_blocks=n_blocks, seq=seq),
        out_shape=jax.ShapeDtypeStruct((n_rows, qd), BF16),
        grid_spec=grid_spec,
        compiler_params=_params(2),
        name="sw_attention",
    )(sink.astype(F32), proj, proj, proj, proj, proj)


def _ga_kernel(q_ref, kl_ref, vl_ref, kc_ref, vc_ref, o_ref):
    dh = GA_HEAD_DIM
    grp = GA_HEADS // GA_KV_HEADS
    blk = GA_BLOCK
    for kvh in range(GA_KV_HEADS):
        sl = slice(kvh * dh, (kvh + 1) * dh)
        q = _stack_heads(q_ref, kvh * grp, grp, dh)
        s_l = lax.dot_general(q, kl_ref[:, sl], _DN_T, preferred_element_type=F32)
        s_c = lax.dot_general(q, kc_ref[:, sl], _DN_T, preferred_element_type=F32)
        o = _softmax_pv([s_l, s_c], [vl_ref[:, sl], vc_ref[:, sl]])
        for g in range(grp):
            hd = kvh * grp + g
            o_ref[:, hd * dh:(hd + 1) * dh] = o[g * blk:(g + 1) * blk].astype(BF16)


def _ga_attention(proj, *, n_batch, seq, ctx_len):
    qd = GA_HEADS * GA_HEAD_DIM
    kvd = GA_KV_HEADS * GA_HEAD_DIM
    blk = GA_BLOCK
    n_blocks = seq // blk
    k_col = qd // kvd
    ctx_blk0 = (n_batch * seq) // ctx_len
    return pl.pallas_call(
        _ga_kernel,
        out_shape=jax.ShapeDtypeStruct((n_batch * seq, qd), BF16),
        grid=(n_batch, n_blocks),
        in_specs=[
            pl.BlockSpec((blk, qd), lambda b, n: (b * n_blocks + n, 0)),
            pl.BlockSpec((seq, kvd), lambda b, n: (b, k_col)),
            pl.BlockSpec((seq, kvd), lambda b, n: (b, k_col + 1)),
            pl.BlockSpec((ctx_len, kvd), lambda b, n: (ctx_blk0 + b, k_col)),
            pl.BlockSpec((ctx_len, kvd), lambda b, n: (ctx_blk0 + b, k_col + 1)),
        ],
        out_specs=pl.BlockSpec((blk, qd), lambda b, n: (b * n_blocks + n, 0)),
        compiler_params=_params(2),
        name="ga_attention",
    )(proj, proj, proj, proj, proj)


def _conv_kernel(um_ref, up_ref, un_ref, w_ref, b_ref, lg_ref, lb_ref, o_ref, win_ref, acc_ref, *,
                 lat_tiles, ctx_tiles):
    j = pl.program_id(1)
    jc = j - lat_tiles
    first = jnp.where(j < lat_tiles, j == 0, jc == 0)
    last = jnp.where(j < lat_tiles, j == lat_tiles - 1, jc == ctx_tiles - 1)
    tl = um_ref.shape[0]
    d = um_ref.shape[1]
    win_ref[0:CONV_HALO, :] = jnp.where(first, 0.0, up_ref[...])
    win_ref[CONV_HALO:CONV_HALO + tl, :] = um_ref[...]
    win_ref[CONV_HALO + tl:, :] = jnp.where(last, 0.0, un_ref[...])
    base = CONV_HALO - CONV_WIDTH // 2
    for s in range(d // LANES):
        ls = slice(s * LANES, (s + 1) * LANES)
        acc = jnp.broadcast_to(b_ref[:, ls], (tl, LANES))
        for k in range(CONV_WIDTH):
            acc = acc + w_ref[k:k + 1, ls] * win_ref[base + k:base + k + tl, ls]
        acc_ref[:, ls] = acc
    u = acc_ref[...]
    mu = jnp.mean(u, axis=-1, keepdims=True)
    var = jnp.mean(jnp.square(u - mu), axis=-1, keepdims=True)
    y = (u - mu) * lax.rsqrt(var + EPS) * lg_ref[...] + lb_ref[...]
    o_ref[...] = _silu(y).astype(BF16)


def _conv_core(u, w_dw, b_dw, ln_g, ln_b, *, n_batch, seq, ctx_len, with_ctx_out):
    d = u.shape[1]
    tl = CONV_TILE
    lat_tiles = seq // tl
    ctx_tiles = ctx_len // tl
    n_ctx = ctx_tiles if with_ctx_out else 0
    n_rows = (n_batch * seq + n_batch * ctx_len) if with_ctx_out else n_batch * seq
    halo_per_tile = tl // CONV_HALO
    n_halo_blocks = u.shape[0] // CONV_HALO

    def main(b, j):
        return jnp.where(j < lat_tiles, b * lat_tiles + j, n_batch * lat_tiles + b * ctx_tiles + (j - lat_tiles))

    return pl.pallas_call(
        functools.partial(_conv_kernel, lat_tiles=lat_tiles, ctx_tiles=ctx_tiles),
        out_shape=jax.ShapeDtypeStruct((n_rows, d), BF16),
        grid=(n_batch, lat_tiles + n_ctx),
        in_specs=[
            pl.BlockSpec((tl, d), lambda b, j: (main(b, j), 0)),
            pl.BlockSpec((CONV_HALO, d), lambda b, j: (jnp.maximum(main(b, j) * halo_per_tile - 1, 0), 0)),
            pl.BlockSpec((CONV_HALO, d),
                         lambda b, j: (jnp.minimum((main(b, j) + 1) * halo_per_tile, n_halo_blocks - 1), 0)),
            pl.BlockSpec((CONV_WIDTH, d), lambda b, j: (0, 0)),
            pl.BlockSpec((1, d), lambda b, j: (0, 0)),
            pl.BlockSpec((1, d), lambda b, j: (0, 0)),
            pl.BlockSpec((1, d), lambda b, j: (0, 0)),
        ],
        out_specs=pl.BlockSpec((tl, d), lambda b, j: (main(b, j), 0)),
        scratch_shapes=[pltpu.VMEM((tl + 2 * CONV_HALO, d), F32), pltpu.VMEM((tl, d), F32)],
        compiler_params=_params(2),
        name="conv_core",
    )(u, u, u, w_dw, b_dw.reshape(1, d), ln_g.reshape(1, d), ln_b.reshape(1, d))


def _out_kernel(x_ref, a_ref, m_ref, g_ref, w_ref, b_ref, o_ref):
    y = jnp.dot(a_ref[...], w_ref[...], preferred_element_type=F32) + b_ref[...]
    o_ref[...] = x_ref[...] + m_ref[5:6, :] * (_rms(y) * g_ref[3:4, :])


def _out_proj(x, a, mod, norm_g, w, bias, *, layer, n_rows, tm, seq, n_batch):
    d = x.shape[1]
    grp = _group_map(tm, seq, n_batch)
    return pl.pallas_call(
        _out_kernel,
        out_shape=jax.ShapeDtypeStruct((n_rows, d), F32),
        grid=(n_rows // tm,),
        in_specs=[
            pl.BlockSpec((tm, d), lambda i: (i, 0)),
            pl.BlockSpec((tm, a.shape[1]), lambda i: (i, 0)),
            pl.BlockSpec((None, None, N_MOD, d), lambda i: (layer, grp(i), 0, 0)),
            pl.BlockSpec((None, 6, d), lambda i: (layer, 0, 0)),
            _resident(w.shape, lambda i: (0, 0)),
            pl.BlockSpec((1, d), lambda i: (0, 0)),
        ],
        out_specs=pl.BlockSpec((tm, d), lambda i: (i, 0)),
        compiler_params=_params(1),
        name=f"out_proj_l{layer}",
    )(x, a, mod, norm_g, w, bias)


def kernel(x, c, ctx, c_ctx, mod_w, mod_b, norm_g, ffn_w_gate, ffn_w_up, ffn_w_down, na_w_qkv, na_w_o, na_rpb, sw_w_qkv, sw_w_o, sw_sink, cv_w_pw1, cv_b_pw1, cv_w_dw, cv_b_dw, cv_ln_g, cv_ln_b, cv_w_pw2, cv_b_pw2, ga_w_qkv, ga_w_o, ga_q_norm, ga_k_norm):
    n_batch, seq, d = x.shape
    ctx_len = ctx.shape[1]
    depth = mod_w.shape[0]
    n_lat = n_batch * seq
    n_ctx = n_batch * ctx_len
    n_all = n_lat + n_ctx
    assert d == D_MODEL and n_batch < MOD_ROWS and seq % CONV_TILE == 0 and ctx_len % CONV_TILE == 0
    assert seq % ctx_len == 0 and (seq // GRID_W) >= NA_WIN_R and seq >= 3 * SW_BLOCK
    tm = _row_tile(n_lat, n_ctx, seq)
    common = dict(tm=tm, seq=seq, n_batch=n_batch)

    c_all = jnp.concatenate([c, c_ctx[None, :], jnp.zeros((MOD_ROWS - n_batch - 1, d), F32)], axis=0)
    mod = _modulation(c_all, mod_w, mod_b).reshape(depth, MOD_ROWS, N_MOD, d)

    wg = ffn_w_gate.astype(BF16)
    wu = ffn_w_up.astype(BF16)
    wd = ffn_w_down.astype(BF16)
    zero_bias = jnp.zeros((1, d), F32)

    xs = jnp.concatenate([x.reshape(n_lat, d), ctx.reshape(n_ctx, d)], axis=0)
    for i in range(depth):
        kind, j = i % N_MIXERS, i // N_MIXERS
        last = i == depth - 1
        ctx_in = not (last and kind == 2)
        ctx_out = not last
        rows_in = n_all if ctx_in else n_lat
        rows_out = n_all if ctx_out else n_lat
        if not ctx_in:
            xs = xs[:n_lat]
        xs = _ffn(xs, mod, norm_g, wg, wu, wd, layer=i, half=0, n_rows=rows_in, **common)
        if kind == 0:
            qkv = _proj(_na_proj_kernel, xs, mod, norm_g, na_w_qkv[j].astype(BF16), [], [], layer=i,
                        n_out=na_w_qkv.shape[-1], out_dtype=BF16, name="na_proj", **common)
            bias_tab = _na_bias_table(na_rpb[j], seq // GRID_W)
            a = _na_attention(qkv, bias_tab, n_batch=n_batch, seq=seq, ctx_len=ctx_len, with_ctx_out=ctx_out)
            w_o, b_o = na_w_o[j], zero_bias
        elif kind == 1:
            qd, kvd = SW_HEADS * SW_HEAD_DIM, SW_KV_HEADS * SW_HEAD_DIM
            perm = np.concatenate([_half_split_perm(SW_HEADS, SW_HEAD_DIM),
                                   qd + _half_split_perm(SW_KV_HEADS, SW_HEAD_DIM),
                                   np.arange(qd + kvd, qd + 2 * kvd)])
            cos_t, sin_t = _rope_tables(seq, SW_HEAD_DIM, tm)
            proj = _proj(_sw_proj_kernel, xs, mod, norm_g, sw_w_qkv[j][:, perm].astype(BF16), [cos_t, sin_t],
                         _rope_specs(tm, seq, n_lat), layer=i, n_out=qd + 2 * kvd, out_dtype=BF16,
                         name="sw_proj", **common)
            a = _sw_attention(proj, sw_sink[j], n_batch=n_batch, seq=seq, ctx_len=ctx_len, with_ctx_out=ctx_out)
            w_o, b_o = sw_w_o[j], zero_bias
        elif kind == 2:
            u = _proj(_cv_proj_kernel, xs, mod, norm_g, cv_w_pw1[j].astype(BF16), [cv_b_pw1[j].reshape(1, -1)],
                      [pl.BlockSpec((1, 2 * d), lambda t: (0, 0))], layer=i, n_out=d, out_dtype=F32,
                      name="cv_proj", **common)
            a = _conv_core(u, cv_w_dw[j], cv_b_dw[j], cv_ln_g[j], cv_ln_b[j], n_batch=n_batch, seq=seq,
                           ctx_len=ctx_len, with_ctx_out=ctx_out)
            w_o, b_o = cv_w_pw2[j], cv_b_pw2[j].reshape(1, d)
        else:
            assert not ctx_out, "context outputs of the global mixer are not implemented"
            qd, kvd = GA_HEADS * GA_HEAD_DIM, GA_KV_HEADS * GA_HEAD_DIM
            perm = np.concatenate([_half_split_perm(GA_HEADS, GA_HEAD_DIM),
                                   qd + _half_split_perm(GA_KV_HEADS, GA_HEAD_DIM),
                                   np.arange(qd + kvd, qd + 2 * kvd)])
            hperm = _half_split_perm(1, GA_HEAD_DIM)
            cos_t, sin_t = _rope_tables(seq, GA_HEAD_DIM, tm)
            vec = pl.BlockSpec((1, GA_HEAD_DIM), lambda t: (0, 0))
            proj = _proj(_ga_proj_kernel, xs, mod, norm_g, ga_w_qkv[j][:, perm].astype(BF16),
                         [cos_t, sin_t, ga_q_norm[j][hperm].reshape(1, -1), ga_k_norm[j][hperm].reshape(1, -1)],
                         _rope_specs(tm, seq, n_lat) + [vec, vec], layer=i, n_out=qd + 2 * kvd, out_dtype=BF16,
                         name="ga_proj", **common)
            a = _ga_attention(proj, n_batch=n_batch, seq=seq, ctx_len=ctx_len)
            w_o, b_o = ga_w_o[j], zero_bias
        xs = _out_proj(xs, a, mod, norm_g, w_o.astype(BF16), b_o, layer=i, n_rows=rows_out, **common)
        xs = _ffn(xs, mod, norm_g, wg, wu, wd, layer=i, half=1, n_rows=rows_out, **common)
    return xs[:n_lat].reshape(n_batch, seq, d)
```

```python
import functools

import numpy as np
import jax
import jax.numpy as jnp
from jax import lax
from jax.experimental import pallas as pl
from jax.experimental.pallas import tpu as pltpu

F32 = jnp.float32
BF16 = jnp.bfloat16

D_MODEL = 1024
DEPTH = 4
GRID_W = 64
N_MIXERS = 4
EPS = 1e-6
NEG = -1e30
LOG2E = 1.4426950408889634
ROPE_THETA = 10000.0
D_FF = 2816
FFN_RESIDUAL_WEIGHT = 0.5
N_MOD = 9
NA_HEADS = 16
NA_HEAD_DIM = 64
NA_WIN_R = 8
NA_WIN_C = 16
NA_QROWS = 4
NA_KROWS = NA_QROWS + NA_WIN_R
SW_HEADS = 16
SW_KV_HEADS = 2
SW_HEAD_DIM = 64
SW_WINDOW = 128
SW_QTILE = 256
SW_KWIN = SW_QTILE + 2 * SW_WINDOW
ONES_ROWS = 16
CONV_WIDTH = 31
GA_HEADS = 8
GA_KV_HEADS = 2
GA_HEAD_DIM = 128
GA_QTILE = 256

LANES = 128
SUBLANES = 8
CONV_HALO = 16
CONV_TILE = 256
MOD_ROWS = 16
VMEM_LIMIT = 56 * 1024 * 1024

_DN_T = (((1,), (1,)), ((), ()))


def _params(n_axes, vmem=VMEM_LIMIT):
    return pltpu.CompilerParams(dimension_semantics=("arbitrary",) * n_axes, vmem_limit_bytes=vmem)


def _resident(block_shape, index_map):
    return pl.BlockSpec(block_shape, index_map, pipeline_mode=pl.Buffered(1))


def _rms(x):
    return x * lax.rsqrt(jnp.mean(x * x, axis=-1, keepdims=True) + EPS)


def _silu(x):
    return x * jax.nn.sigmoid(x)


def _row_tile(n_lat_rows, n_ctx_rows, seq):
    for t in (512, 256, 128):
        if seq % t == 0 and n_ctx_rows % t == 0 and n_lat_rows % t == 0:
            return t
    raise ValueError("unsupported token counts")


def _mod_kernel(c_ref, w_ref, b_ref, o_ref):
    a = _silu(c_ref[...]).astype(BF16)
    o_ref[0] = jnp.dot(a, w_ref[0].astype(BF16), preferred_element_type=F32) + b_ref[0]


def _modulation(c_all, mod_w, mod_b):
    depth, d, n = mod_w.shape
    tn = 1152
    return pl.pallas_call(
        _mod_kernel,
        out_shape=jax.ShapeDtypeStruct((depth, MOD_ROWS, n), F32),
        grid=(depth, n // tn),
        in_specs=[
            pl.BlockSpec((MOD_ROWS, d), lambda i, j: (0, 0)),
            pl.BlockSpec((1, d, tn), lambda i, j: (i, 0, j)),
            pl.BlockSpec((1, 1, tn), lambda i, j: (i, 0, j)),
        ],
        out_specs=pl.BlockSpec((1, MOD_ROWS, tn), lambda i, j: (i, 0, j)),
        compiler_params=_params(2),
        name="modulation",
    )(c_all, mod_w, mod_b.reshape(depth, 1, n))


def _ffn_kernel(x_ref, m_ref, g_ref, wg_ref, wu_ref, wd_ref, o_ref, *, slot):
    x = x_ref[...]
    shift = m_ref[3 * slot:3 * slot + 1, :]
    scale = m_ref[3 * slot + 1:3 * slot + 2, :]
    gate = m_ref[3 * slot + 2:3 * slot + 3, :]
    g_pre = g_ref[2 * slot:2 * slot + 1, :]
    g_post = g_ref[2 * slot + 1:2 * slot + 2, :]
    h = (_rms(x) * g_pre * (1.0 + scale) + shift).astype(BF16)
    gt = jnp.dot(h, wg_ref[...], preferred_element_type=F32)
    up = jnp.dot(h, wu_ref[...], preferred_element_type=F32)
    a = (_silu(gt) * up).astype(BF16)
    y = jnp.dot(a, wd_ref[...], preferred_element_type=F32)
    o_ref[...] = x + (FFN_RESIDUAL_WEIGHT * gate) * (_rms(y) * g_post)


def _group_map(tm, seq, n_batch):
    return lambda i: jnp.minimum((i * tm) // seq, n_batch)


def _ffn(x, mod, norm_g, wg, wu, wd, *, layer, half, n_rows, tm, seq, n_batch):
    d = x.shape[1]
    f = wg.shape[-1]
    slot = 2 * half
    grp = _group_map(tm, seq, n_batch)
    return pl.pallas_call(
        functools.partial(_ffn_kernel, slot=slot),
        out_shape=jax.ShapeDtypeStruct((n_rows, d), F32),
        grid=(n_rows // tm,),
        in_specs=[
            pl.BlockSpec((tm, d), lambda i: (i, 0)),
            pl.BlockSpec((None, None, N_MOD, d), lambda i: (layer, grp(i), 0, 0)),
            pl.BlockSpec((None, 6, d), lambda i: (layer, 0, 0)),
            _resident((None, None, d, f), lambda i: (layer, half, 0, 0)),
            _resident((None, None, d, f), lambda i: (layer, half, 0, 0)),
            _resident((None, None, f, d), lambda i: (layer, half, 0, 0)),
        ],
        out_specs=pl.BlockSpec((tm, d), lambda i: (i, 0)),
        compiler_params=_params(1),
        name=f"ffn_l{layer}h{half}",
    )(x, mod, norm_g, wg, wu, wd)


def _mixer_norm(x_ref, m_ref, g_ref):
    return (_rms(x_ref[...]) * g_ref[2:3, :] * (1.0 + m_ref[4:5, :]) + m_ref[3:4, :]).astype(BF16)


def _rotate_half(x, half):
    if 2 * half == LANES:
        return pltpu.roll(x, half, axis=1)
    lane = lax.broadcasted_iota(jnp.int32, x.shape, 1)
    return jnp.where(lane % (2 * half) < half, pltpu.roll(x, LANES - half, axis=1), pltpu.roll(x, half, axis=1))


def _na_proj_kernel(x_ref, m_ref, g_ref, w_ref, o_ref):
    h = _mixer_norm(x_ref, m_ref, g_ref)
    p = jnp.dot(h, w_ref[...], preferred_element_type=F32)
    qd = NA_HEADS * NA_HEAD_DIM
    o_ref[:, :qd] = (p[:, :qd] * (NA_HEAD_DIM ** -0.5 * LOG2E)).astype(BF16)
    o_ref[:, qd:] = p[:, qd:].astype(BF16)


def _sw_proj_kernel(x_ref, m_ref, g_ref, w_ref, cos_ref, sin_ref, o_ref):
    h = _mixer_norm(x_ref, m_ref, g_ref)
    p = jnp.dot(h, w_ref[...], preferred_element_type=F32)
    cos = cos_ref[...]
    sin = sin_ref[...]
    n_q = SW_HEADS * SW_HEAD_DIM // LANES
    n_k = SW_KV_HEADS * SW_HEAD_DIM // LANES
    for j in range(n_q + n_k):
        xs = p[:, j * LANES:(j + 1) * LANES]
        r = xs * cos + _rotate_half(xs, SW_HEAD_DIM // 2) * sin
        if j < n_q:
            r = r * (SW_HEAD_DIM ** -0.5 * LOG2E)
        o_ref[:, j * LANES:(j + 1) * LANES] = r.astype(BF16)
    o_ref[:, (n_q + n_k) * LANES:] = p[:, (n_q + n_k) * LANES:].astype(BF16)


def _ga_proj_kernel(x_ref, m_ref, g_ref, w_ref, cos_ref, sin_ref, qn_ref, kn_ref, o_ref):
    h = _mixer_norm(x_ref, m_ref, g_ref)
    p = jnp.dot(h, w_ref[...], preferred_element_type=F32)
    cos = cos_ref[...]
    sin = sin_ref[...]
    for j in range(GA_HEADS + GA_KV_HEADS):
        xs = _rms(p[:, j * LANES:(j + 1) * LANES])
        if j < GA_HEADS:
            xs = xs * (qn_ref[...] * (GA_HEAD_DIM ** -0.5 * LOG2E))
        else:
            xs = xs * kn_ref[...]
        r = xs * cos + _rotate_half(xs, GA_HEAD_DIM // 2) * sin
        o_ref[:, j * LANES:(j + 1) * LANES] = r.astype(BF16)
    n0 = (GA_HEADS + GA_KV_HEADS) * LANES
    o_ref[:, n0:] = p[:, n0:].astype(BF16)


def _cv_proj_kernel(x_ref, m_ref, g_ref, w_ref, b_ref, o_ref):
    h = _mixer_norm(x_ref, m_ref, g_ref)
    u = jnp.dot(h, w_ref[...], preferred_element_type=F32) + b_ref[...]
    d = o_ref.shape[1]
    o_ref[...] = u[:, :d] * jax.nn.sigmoid(u[:, d:])


def _proj(body, x, mod, norm_g, w, extras, extra_specs, *, layer, n_out, out_dtype, tm, seq, n_batch, name):
    n_rows, d = x.shape
    grp = _group_map(tm, seq, n_batch)
    return pl.pallas_call(
        body,
        out_shape=jax.ShapeDtypeStruct((n_rows, n_out), out_dtype),
        grid=(n_rows // tm,),
        in_specs=[
            pl.BlockSpec((tm, d), lambda i: (i, 0)),
            pl.BlockSpec((None, None, N_MOD, d), lambda i: (layer, grp(i), 0, 0)),
            pl.BlockSpec((None, 6, d), lambda i: (layer, 0, 0)),
            _resident(w.shape, lambda i: (0, 0)),
        ] + extra_specs,
        out_specs=pl.BlockSpec((tm, n_out), lambda i: (i, 0)),
        compiler_params=_params(1),
        name=name,
    )(x, mod, norm_g, w, *extras)


def _rope_specs(tm, seq, n_lat_rows):
    n_seq_blocks = seq // tm
    idx = lambda i: (jnp.where(i * tm < n_lat_rows, i % n_seq_blocks, n_seq_blocks), 0)
    return [pl.BlockSpec((tm, LANES), idx), pl.BlockSpec((tm, LANES), idx)]


def _rope_tables(seq, head_dim, tm):
    t = np.arange(seq)
    row = (t // GRID_W).astype(np.float32)
    col = (t % GRID_W).astype(np.float32)
    n_pairs_axis = head_dim // 4
    freq = jnp.asarray(ROPE_THETA, F32) ** (-jnp.arange(n_pairs_axis, dtype=F32) / n_pairs_axis)
    ang = jnp.concatenate([row[:, None] * freq, col[:, None] * freq], axis=-1)
    cos, sin = jnp.cos(ang), jnp.sin(ang)
    reps = LANES // head_dim
    cos_t = jnp.tile(jnp.concatenate([cos, cos], axis=-1), (1, reps))
    sin_t = jnp.tile(jnp.concatenate([-sin, sin], axis=-1), (1, reps))
    cos_t = jnp.concatenate([cos_t, jnp.ones((tm, LANES), F32)], axis=0)
    sin_t = jnp.concatenate([sin_t, jnp.zeros((tm, LANES), F32)], axis=0)
    return cos_t, sin_t


def _half_split_perm(n_heads, head_dim):
    per_head = np.concatenate([np.arange(0, head_dim, 2), np.arange(1, head_dim, 2)])
    return (np.arange(n_heads)[:, None] * head_dim + per_head[None, :]).reshape(-1)


def _attend(q_ref, o_ref, *, n_heads, dh, keys_of, values_of, addends_of, floor_of=None):
    per_store = LANES // dh

    def scores(hd):
        qh = q_ref[:, hd * dh:(hd + 1) * dh]
        ss = []
        for kb, adds in zip(keys_of(hd), addends_of(hd)):
            s = lax.dot_general(kb, qh, _DN_T, preferred_element_type=F32)
            for a in adds:
                s = s + a
            ss.append(s)
        return ss

    def finish(hd, ss):
        m = ss[0].max(axis=0, keepdims=True)
        for s in ss[1:]:
            m = jnp.maximum(m, s.max(axis=0, keepdims=True))
        floor = None if floor_of is None else floor_of(hd)
        if floor is not None:
            m = jnp.maximum(m, floor)
        ot = None
        for s, vb in zip(ss, values_of(hd)):
            part = jnp.dot(vb, jnp.exp2(s - m).astype(BF16), preferred_element_type=F32)
            ot = part if ot is None else ot + part
        l = ot[dh:dh + 1, :]
        if floor is not None:
            l = l + jnp.exp2(floor - m)
        return ot[:dh, :] / l

    pending = scores(0)
    outs = []
    for hd in range(n_heads):
        ss = pending
        if hd + 1 < n_heads:
            pending = scores(hd + 1)
        outs.append(finish(hd, ss))
        if len(outs) == per_store:
            o2 = outs[0] if per_store == 1 else jnp.concatenate(outs, axis=0)
            outs = []
            o_ref[:, (hd + 1 - per_store) * dh:(hd + 1) * dh] = o2.T.astype(BF16)


def _values_transposed(v, n_batch, n_tok, n_heads, dh, pad=0):
    vt = v.reshape(n_batch, n_tok, n_heads, dh).transpose(0, 2, 3, 1)
    if pad:
        vt = jnp.pad(vt, ((0, 0), (0, 0), (0, 0), (pad, pad)))
    return jnp.concatenate([vt, jnp.ones((n_batch, n_heads, ONES_ROWS, n_tok + 2 * pad), vt.dtype)], axis=2)


def _na_kernel(q_ref, kl_ref, kc_ref, vlt_ref, vct_ref, val_ref, mask_ref, o_ref, *, n_groups):
    g = pl.program_id(1)
    dh = NA_HEAD_DIM
    n_keys = NA_KROWS * GRID_W
    hs = lambda hd: slice(hd * dh, (hd + 1) * dh)

    @pl.when(g < n_groups)
    def _():
        off = pl.multiple_of(g * (NA_QROWS * GRID_W), NA_QROWS * GRID_W)
        _attend(q_ref, o_ref, n_heads=NA_HEADS, dh=dh,
                keys_of=lambda hd: [kl_ref[pl.ds(off, n_keys), hs(hd)], kc_ref[:, hs(hd)]],
                values_of=lambda hd: [vlt_ref[hd, :, pl.ds(off, n_keys)], vct_ref[hd]],
                addends_of=lambda hd: [[val_ref[hd], mask_ref[...]], []])

    @pl.when(g >= n_groups)
    def _():
        _attend(q_ref, o_ref, n_heads=NA_HEADS, dh=dh,
                keys_of=lambda hd: [kc_ref[:, hs(hd)]],
                values_of=lambda hd: [vct_ref[hd]],
                addends_of=lambda hd: [[]])


def _na_bias_tables(rpb, rows):
    qr = np.arange(NA_QROWS)
    kri = np.arange(NA_KROWS)
    j = np.arange(GRID_W)
    kc = np.arange(GRID_W)
    half = NA_WIN_R // 2
    ridx = np.clip(kri[:, None] - half - qr[None, :] + NA_WIN_R - 1, 0, 2 * NA_WIN_R - 2)
    cidx = np.clip(kc[:, None] - j[None, :] + NA_WIN_C - 1, 0, 2 * NA_WIN_C - 2)
    val = rpb.astype(F32)[:, ridx][:, :, :, cidx] * LOG2E
    n_k, n_q = NA_KROWS * GRID_W, NA_QROWS * GRID_W
    val = val.transpose(0, 1, 3, 2, 4).reshape(rpb.shape[0], n_k, n_q)
    ws = np.clip(j - NA_WIN_C // 2, 0, GRID_W - NA_WIN_C)
    col_ok = (kc[:, None] >= ws[None, :]) & (kc[:, None] < ws[None, :] + NA_WIN_C)
    masks = []
    for r0 in (0, NA_QROWS, rows - NA_QROWS):
        r = r0 + qr
        rs = np.clip(r - half, 0, rows - NA_WIN_R)
        k_abs = r0 - half + kri
        row_ok = (k_abs[:, None] >= rs[None, :]) & (k_abs[:, None] < rs[None, :] + NA_WIN_R)
        ok = row_ok[:, None, :, None] & col_ok[None, :, None, :]
        masks.append(np.where(ok, 0.0, NEG).reshape(n_k, n_q))
    return val, jnp.asarray(np.stack(masks), F32)


def _na_attention(qkv, rpb, *, n_batch, seq, ctx_len, with_ctx_out):
    rows = seq // GRID_W
    qd = NA_HEADS * NA_HEAD_DIM
    tq = NA_QROWS * GRID_W
    n_keys = NA_KROWS * GRID_W
    pad = (NA_WIN_R // 2) * GRID_W
    n_groups = seq // tq
    ctx_tiles = ctx_len // tq
    n_ctx_tiles = ctx_tiles if with_ctx_out else 0
    n_lat = n_batch * seq
    n_rows = (n_lat + n_batch * ctx_len) if with_ctx_out else n_lat
    ctx_blk0 = n_lat // ctx_len
    vrows = NA_HEAD_DIM + ONES_ROWS
    val, mask = _na_bias_tables(rpb, rows)
    k_pad = jnp.pad(qkv[:n_lat, qd:2 * qd].reshape(n_batch, seq, qd), ((0, 0), (pad, pad), (0, 0)))
    k_pad = k_pad.reshape(n_batch * (seq + 2 * pad), qd)
    vlt = _values_transposed(qkv[:n_lat, 2 * qd:], n_batch, seq, NA_HEADS, NA_HEAD_DIM, pad=pad)
    vct = _values_transposed(qkv[n_lat:, 2 * qd:], n_batch, ctx_len, NA_HEADS, NA_HEAD_DIM)

    def q_idx(b, g):
        return (jnp.where(g < n_groups, b * n_groups + g, n_batch * n_groups + b * ctx_tiles + (g - n_groups)), 0)

    def mask_idx(b, g):
        return (jnp.where(g == 0, 0, jnp.where(g >= n_groups - 1, 2, 1)), 0, 0)

    return pl.pallas_call(
        functools.partial(_na_kernel, n_groups=n_groups),
        out_shape=jax.ShapeDtypeStruct((n_rows, qd), BF16),
        grid=(n_batch, n_groups + n_ctx_tiles),
        in_specs=[
            pl.BlockSpec((tq, qd), q_idx),
            pl.BlockSpec((seq + 2 * pad, qd), lambda b, g: (b, 0)),
            pl.BlockSpec((ctx_len, qd), lambda b, g: (ctx_blk0 + b, 1)),
            pl.BlockSpec((None, NA_HEADS, vrows, seq + 2 * pad), lambda b, g: (b, 0, 0, 0)),
            pl.BlockSpec((None, NA_HEADS, vrows, ctx_len), lambda b, g: (b, 0, 0, 0)),
            _resident((NA_HEADS, n_keys, tq), lambda b, g: (0, 0, 0)),
            pl.BlockSpec((None, n_keys, tq), mask_idx),
        ],
        out_specs=pl.BlockSpec((tq, qd), q_idx),
        compiler_params=_params(2),
        name="na_attention",
    )(qkv, k_pad, qkv, vlt, vct, val, mask)


def _sw_kernel(sink_ref, q_ref, kl_ref, kc_ref, vlt_ref, vct_ref, mask_ref, o_ref, *, n_tiles, seq):
    n = pl.program_id(1)
    dh = SW_HEAD_DIM
    grp = SW_HEADS // SW_KV_HEADS
    win = SW_KWIN
    ks = lambda hd: slice((hd // grp) * dh, (hd // grp + 1) * dh)
    floor_of = lambda hd: sink_ref[hd] * LOG2E

    @pl.when(n < n_tiles)
    def _():
        ws = pl.multiple_of(jnp.clip(n * SW_QTILE - SW_WINDOW, 0, seq - win), LANES)
        _attend(q_ref, o_ref, n_heads=SW_HEADS, dh=dh,
                keys_of=lambda hd: [kl_ref[pl.ds(ws, win), ks(hd)], kc_ref[:, ks(hd)]],
                values_of=lambda hd: [vlt_ref[hd // grp, :, pl.ds(ws, win)], vct_ref[hd // grp]],
                addends_of=lambda hd: [[mask_ref[...]], []], floor_of=floor_of)

    @pl.when(n >= n_tiles)
    def _():
        _attend(q_ref, o_ref, n_heads=SW_HEADS, dh=dh,
                keys_of=lambda hd: [kc_ref[:, ks(hd)]],
                values_of=lambda hd: [vct_ref[hd // grp]],
                addends_of=lambda hd: [[]], floor_of=floor_of)


def _sw_attention(proj, sink, *, n_batch, seq, ctx_len, with_ctx_out):
    qd = SW_HEADS * SW_HEAD_DIM
    kvd = SW_KV_HEADS * SW_HEAD_DIM
    tq = SW_QTILE
    win = SW_KWIN
    n_tiles = seq // tq
    ctx_tiles = ctx_len // tq
    n_ctx_tiles = ctx_tiles if with_ctx_out else 0
    lat_tiles = n_batch * n_tiles
    n_lat = n_batch * seq
    n_rows = (n_lat + n_batch * ctx_len) if with_ctx_out else n_lat
    k_col = qd // LANES
    ctx_blk0 = n_lat // ctx_len
    vrows = SW_HEAD_DIM + ONES_ROWS
    vlt = _values_transposed(proj[:n_lat, qd + kvd:], n_batch, seq, SW_KV_HEADS, SW_HEAD_DIM)
    vct = _values_transposed(proj[n_lat:, qd + kvd:], n_batch, ctx_len, SW_KV_HEADS, SW_HEAD_DIM)
    ki = np.arange(win)[:, None]
    qj = np.arange(tq)[None, :]
    mask = np.stack([np.where(np.abs(ki - qj - off) <= SW_WINDOW, 0.0, NEG) for off in (0, SW_WINDOW, 2 * SW_WINDOW)])
    mask = jnp.asarray(mask, F32)

    def q_idx(b, n, s):
        return (jnp.where(n < n_tiles, b * n_tiles + n, lat_tiles + b * ctx_tiles + (n - n_tiles)), 0)

    def mask_idx(b, n, s):
        nn = jnp.minimum(n, n_tiles - 1)
        ws_blk = jnp.clip(nn * (tq // SW_WINDOW) - 1, 0, (seq - win) // SW_WINDOW)
        return (nn * (tq // SW_WINDOW) - ws_blk, 0, 0)

    grid_spec = pltpu.PrefetchScalarGridSpec(
        num_scalar_prefetch=1,
        grid=(n_batch, n_tiles + n_ctx_tiles),
        in_specs=[
            pl.BlockSpec((tq, qd), q_idx),
            pl.BlockSpec((seq, LANES), lambda b, n, s: (b, k_col)),
            pl.BlockSpec((ctx_len, LANES), lambda b, n, s: (ctx_blk0 + b, k_col)),
            pl.BlockSpec((None, SW_KV_HEADS, vrows, seq), lambda b, n, s: (b, 0, 0, 0)),
            pl.BlockSpec((None, SW_KV_HEADS, vrows, ctx_len), lambda b, n, s: (b, 0, 0, 0)),
            pl.BlockSpec((None, win, tq), mask_idx),
        ],
        out_specs=pl.BlockSpec((tq, qd), q_idx),
    )
    return pl.pallas_call(
        functools.partial(_sw_kernel, n_tiles=n_tiles, seq=seq),
        out_shape=jax.ShapeDtypeStruct((n_rows, qd), BF16),
        grid_spec=grid_spec,
        compiler_params=_params(2),
        name="sw_attention",
    )(sink.astype(F32), proj, proj, proj, vlt, vct, mask)


def _ga_kernel(q_ref, kl_ref, kc_ref, vlt_ref, vct_ref, o_ref):
    dh = GA_HEAD_DIM
    grp = GA_HEADS // GA_KV_HEADS
    ks = lambda hd: slice((hd // grp) * dh, (hd // grp + 1) * dh)
    _attend(q_ref, o_ref, n_heads=GA_HEADS, dh=dh,
            keys_of=lambda hd: [kl_ref[:, ks(hd)], kc_ref[:, ks(hd)]],
            values_of=lambda hd: [vlt_ref[hd // grp], vct_ref[hd // grp]],
            addends_of=lambda hd: [[], []])


def _ga_attention(proj, *, n_batch, seq, ctx_len):
    qd = GA_HEADS * GA_HEAD_DIM
    kvd = GA_KV_HEADS * GA_HEAD_DIM
    tq = GA_QTILE
    n_tiles = seq // tq
    n_lat = n_batch * seq
    k_col = qd // kvd
    ctx_blk0 = n_lat // ctx_len
    vrows = GA_HEAD_DIM + ONES_ROWS
    vlt = _values_transposed(proj[:n_lat, qd + kvd:], n_batch, seq, GA_KV_HEADS, GA_HEAD_DIM)
    vct = _values_transposed(proj[n_lat:, qd + kvd:], n_batch, ctx_len, GA_KV_HEADS, GA_HEAD_DIM)
    return pl.pallas_call(
        _ga_kernel,
        out_shape=jax.ShapeDtypeStruct((n_lat, qd), BF16),
        grid=(n_batch, n_tiles),
        in_specs=[
            pl.BlockSpec((tq, qd), lambda b, n: (b * n_tiles + n, 0)),
            pl.BlockSpec((seq, kvd), lambda b, n: (b, k_col)),
            pl.BlockSpec((ctx_len, kvd), lambda b, n: (ctx_blk0 + b, k_col)),
            pl.BlockSpec((None, GA_KV_HEADS, vrows, seq), lambda b, n: (b, 0, 0, 0)),
            pl.BlockSpec((None, GA_KV_HEADS, vrows, ctx_len), lambda b, n: (b, 0, 0, 0)),
        ],
        out_specs=pl.BlockSpec((tq, qd), lambda b, n: (b * n_tiles + n, 0)),
        compiler_params=_params(2),
        name="ga_attention",
    )(proj, proj, proj, vlt, vct)


def _conv_kernel(um_ref, up_ref, un_ref, w_ref, b_ref, lg_ref, lb_ref, o_ref, win_ref, sh_ref, acc_ref, *,
                 lat_tiles, ctx_tiles):
    j = pl.program_id(1)
    jc = j - lat_tiles
    first = jnp.where(j < lat_tiles, j == 0, jc == 0)
    last = jnp.where(j < lat_tiles, j == lat_tiles - 1, jc == ctx_tiles - 1)
    tl = um_ref.shape[0]
    d = um_ref.shape[1]
    sh_rows = sh_ref.shape[1]
    win_ref[0:CONV_HALO, :] = jnp.where(first, 0.0, up_ref[...])
    win_ref[CONV_HALO:CONV_HALO + tl, :] = um_ref[...]
    win_ref[CONV_HALO + tl:, :] = jnp.where(last, 0.0, un_ref[...])
    base = CONV_HALO - CONV_WIDTH // 2
    for s in range(d // LANES):
        ls = slice(s * LANES, (s + 1) * LANES)
        for r in range(1, SUBLANES):
            sh_ref[r] = win_ref[r:r + sh_rows, ls]
        acc = jnp.broadcast_to(b_ref[:, ls], (tl, LANES))
        for k in range(CONV_WIDTH):
            a8, r = divmod(base + k, SUBLANES)
            lo = a8 * SUBLANES
            src = win_ref[lo:lo + tl, ls] if r == 0 else sh_ref[r, lo:lo + tl, :]
            acc = acc + w_ref[k:k + 1, ls] * src
        acc_ref[:, ls] = acc
    u = acc_ref[...]
    mu = jnp.mean(u, axis=-1, keepdims=True)
    var = jnp.mean(jnp.square(u - mu), axis=-1, keepdims=True)
    y = (u - mu) * lax.rsqrt(var + EPS) * lg_ref[...] + lb_ref[...]
    o_ref[...] = _silu(y).astype(BF16)


def _conv_core(u, w_dw, b_dw, ln_g, ln_b, *, n_batch, seq, ctx_len, with_ctx_out):
    d = u.shape[1]
    tl = CONV_TILE
    lat_tiles = seq // tl
    ctx_tiles = ctx_len // tl
    n_ctx = ctx_tiles if with_ctx_out else 0
    n_rows = (n_batch * seq + n_batch * ctx_len) if with_ctx_out else n_batch * seq
    halo_per_tile = tl // CONV_HALO
    n_halo_blocks = u.shape[0] // CONV_HALO
    sh_rows = tl + 2 * CONV_HALO - SUBLANES

    def main(b, j):
        return jnp.where(j < lat_tiles, b * lat_tiles + j, n_batch * lat_tiles + b * ctx_tiles + (j - lat_tiles))

    return pl.pallas_call(
        functools.partial(_conv_kernel, lat_tiles=lat_tiles, ctx_tiles=ctx_tiles),
        out_shape=jax.ShapeDtypeStruct((n_rows, d), BF16),
        grid=(n_batch, lat_tiles + n_ctx),
        in_specs=[
            pl.BlockSpec((tl, d), lambda b, j: (main(b, j), 0)),
            pl.BlockSpec((CONV_HALO, d), lambda b, j: (jnp.maximum(main(b, j) * halo_per_tile - 1, 0), 0)),
            pl.BlockSpec((CONV_HALO, d),
                         lambda b, j: (jnp.minimum((main(b, j) + 1) * halo_per_tile, n_halo_blocks - 1), 0)),
            pl.BlockSpec((CONV_WIDTH, d), lambda b, j: (0, 0)),
            pl.BlockSpec((1, d), lambda b, j: (0, 0)),
            pl.BlockSpec((1, d), lambda b, j: (0, 0)),
            pl.BlockSpec((1, d), lambda b, j: (0, 0)),
        ],
        out_specs=pl.BlockSpec((tl, d), lambda b, j: (main(b, j), 0)),
        scratch_shapes=[pltpu.VMEM((tl + 2 * CONV_HALO, d), F32), pltpu.VMEM((SUBLANES, sh_rows, LANES), F32),
                        pltpu.VMEM((tl, d), F32)],
        compiler_params=_params(2),
        name="conv_core",
    )(u, u, u, w_dw, b_dw.reshape(1, d), ln_g.reshape(1, d), ln_b.reshape(1, d))


def _out_kernel(x_ref, a_ref, m_ref, g_ref, w_ref, b_ref, o_ref):
    y = jnp.dot(a_ref[...], w_ref[...], preferred_element_type=F32) + b_ref[...]
    o_ref[...] = x_ref[...] + m_ref[5:6, :] * (_rms(y) * g_ref[3:4, :])


def _out_proj(x, a, mod, norm_g, w, bias, *, layer, n_rows, tm, seq, n_batch):
    d = x.shape[1]
    grp = _group_map(tm, seq, n_batch)
    return pl.pallas_call(
        _out_kernel,
        out_shape=jax.ShapeDtypeStruct((n_rows, d), F32),
        grid=(n_rows // tm,),
        in_specs=[
            pl.BlockSpec((tm, d), lambda i: (i, 0)),
            pl.BlockSpec((tm, a.shape[1]), lambda i: (i, 0)),
            pl.BlockSpec((None, None, N_MOD, d), lambda i: (layer, grp(i), 0, 0)),
            pl.BlockSpec((None, 6, d), lambda i: (layer, 0, 0)),
            _resident(w.shape, lambda i: (0, 0)),
            pl.BlockSpec((1, d), lambda i: (0, 0)),
        ],
        out_specs=pl.BlockSpec((tm, d), lambda i: (i, 0)),
        compiler_params=_params(1),
        name=f"out_proj_l{layer}",
    )(x, a, mod, norm_g, w, bias)


def kernel(x, c, ctx, c_ctx, mod_w, mod_b, norm_g, ffn_w_gate, ffn_w_up, ffn_w_down, na_w_qkv, na_w_o, na_rpb, sw_w_qkv, sw_w_o, sw_sink, cv_w_pw1, cv_b_pw1, cv_w_dw, cv_b_dw, cv_ln_g, cv_ln_b, cv_w_pw2, cv_b_pw2, ga_w_qkv, ga_w_o, ga_q_norm, ga_k_norm):
    n_batch, seq, d = x.shape
    ctx_len = ctx.shape[1]
    depth = mod_w.shape[0]
    n_lat = n_batch * seq
    n_ctx = n_batch * ctx_len
    n_all = n_lat + n_ctx
    assert d == D_MODEL and n_batch < MOD_ROWS and seq % CONV_TILE == 0 and ctx_len % CONV_TILE == 0
    assert seq % ctx_len == 0 and seq >= SW_KWIN and seq % SW_QTILE == 0 and ctx_len % SW_QTILE == 0
    assert (seq // GRID_W) % NA_QROWS == 0 and (seq // GRID_W) >= NA_KROWS and seq % GA_QTILE == 0
    tm = _row_tile(n_lat, n_ctx, seq)
    common = dict(tm=tm, seq=seq, n_batch=n_batch)

    c_all = jnp.concatenate([c, c_ctx[None, :], jnp.zeros((MOD_ROWS - n_batch - 1, d), F32)], axis=0)
    mod = _modulation(c_all, mod_w, mod_b).reshape(depth, MOD_ROWS, N_MOD, d)

    wg = ffn_w_gate.astype(BF16)
    wu = ffn_w_up.astype(BF16)
    wd = ffn_w_down.astype(BF16)
    zero_bias = jnp.zeros((1, d), F32)

    xs = jnp.concatenate([x.reshape(n_lat, d), ctx.reshape(n_ctx, d)], axis=0)
    for i in range(depth):
        kind, j = i % N_MIXERS, i // N_MIXERS
        last = i == depth - 1
        ctx_in = not (last and kind == 2)
        ctx_out = not last
        rows_in = n_all if ctx_in else n_lat
        rows_out = n_all if ctx_out else n_lat
        if not ctx_in:
            xs = xs[:n_lat]
        xs = _ffn(xs, mod, norm_g, wg, wu, wd, layer=i, half=0, n_rows=rows_in, **common)
        if kind == 0:
            qkv = _proj(_na_proj_kernel, xs, mod, norm_g, na_w_qkv[j].astype(BF16), [], [], layer=i,
                        n_out=na_w_qkv.shape[-1], out_dtype=BF16, name="na_proj", **common)
            a = _na_attention(qkv, na_rpb[j], n_batch=n_batch, seq=seq, ctx_len=ctx_len, with_ctx_out=ctx_out)
            w_o, b_o = na_w_o[j], zero_bias
        elif kind == 1:
            qd, kvd = SW_HEADS * SW_HEAD_DIM, SW_KV_HEADS * SW_HEAD_DIM
            perm = np.concatenate([_half_split_perm(SW_HEADS, SW_HEAD_DIM),
                                   qd + _half_split_perm(SW_KV_HEADS, SW_HEAD_DIM),
                                   np.arange(qd + kvd, qd + 2 * kvd)])
            cos_t, sin_t = _rope_tables(seq, SW_HEAD_DIM, tm)
            proj = _proj(_sw_proj_kernel, xs, mod, norm_g, sw_w_qkv[j][:, perm].astype(BF16), [cos_t, sin_t],
                         _rope_specs(tm, seq, n_lat), layer=i, n_out=qd + 2 * kvd, out_dtype=BF16,
                         name="sw_proj", **common)
            a = _sw_attention(proj, sw_sink[j], n_batch=n_batch, seq=seq, ctx_len=ctx_len, with_ctx_out=ctx_out)
            w_o, b_o = sw_w_o[j], zero_bias
        elif kind == 2:
            u = _proj(_cv_proj_kernel, xs, mod, norm_g, cv_w_pw1[j].astype(BF16), [cv_b_pw1[j].reshape(1, -1)],
                      [pl.BlockSpec((1, 2 * d), lambda t: (0, 0))], layer=i, n_out=d, out_dtype=F32,
                      name="cv_proj", **common)
            a = _conv_core(u, cv_w_dw[j], cv_b_dw[j], cv_ln_g[j], cv_ln_b[j], n_batch=n_batch, seq=seq,
                           ctx_len=ctx_len, with_ctx_out=ctx_out)
            w_o, b_o = cv_w_pw2[j], cv_b_pw2[j].reshape(1, d)
        else:
            assert not ctx_out, "context outputs of the global mixer are not implemented"
            qd, kvd = GA_HEADS * GA_HEAD_DIM, GA_KV_HEADS * GA_HEAD_DIM
            perm = np.concatenate([_half_split_perm(GA_HEADS, GA_HEAD_DIM),
                                   qd + _half_split_perm(GA_KV_HEADS, GA_HEAD_DIM),
                                   np.arange(qd + kvd, qd + 2 * kvd)])
            hperm = _half_split_perm(1, GA_HEAD_DIM)
            cos_t, sin_t = _rope_tables(seq, GA_HEAD_DIM, tm)
            vec = pl.BlockSpec((1, GA_HEAD_DIM), lambda t: (0, 0))
            proj = _proj(_ga_proj_kernel, xs, mod, norm_g, ga_w_qkv[j][:, perm].astype(BF16),
                         [cos_t, sin_t, ga_q_norm[j][hperm].reshape(1, -1), ga_k_norm[j][hperm].reshape(1, -1)],
                         _rope_specs(tm, seq, n_lat) + [vec, vec], layer=i, n_out=qd + 2 * kvd, out_dtype=BF16,
                         name="ga_proj", **common)
            a = _ga_attention(proj, n_batch=n_batch, seq=seq, ctx_len=ctx_len)
            w_o, b_o = ga_w_o[j], zero_bias
        xs = _out_proj(xs, a, mod, norm_g, w_o.astype(BF16), b_o, layer=i, n_rows=rows_out, **common)
        xs = _ffn(xs, mod, norm_g, wg, wu, wd, layer=i, half=1, n_rows=rows_out, **common)
    return xs[:n_lat].reshape(n_batch, seq, d)
```

```python
import functools

import numpy as np
import jax
import jax.numpy as jnp
from jax import lax
from jax.experimental import pallas as pl
from jax.experimental.pallas import tpu as pltpu

F32 = jnp.float32
BF16 = jnp.bfloat16

D_MODEL = 1024
DEPTH = 4
GRID_W = 64
N_MIXERS = 4
EPS = 1e-6
NEG = -1e30
LOG2E = 1.4426950408889634
ROPE_THETA = 10000.0
D_FF = 2816
FFN_RESIDUAL_WEIGHT = 0.5
N_MOD = 9
NA_HEADS = 16
NA_HEAD_DIM = 64
NA_WIN_R = 8
NA_WIN_C = 16
NA_QROWS = 4
NA_KROWS = NA_QROWS + NA_WIN_R
SW_HEADS = 16
SW_KV_HEADS = 2
SW_HEAD_DIM = 64
SW_WINDOW = 128
SW_QTILE = 256
SW_KWIN = SW_QTILE + 2 * SW_WINDOW
ONES_ROWS = 16
CONV_WIDTH = 31
GA_HEADS = 8
GA_KV_HEADS = 2
GA_HEAD_DIM = 128
GA_QTILE = 256

LANES = 128
SUBLANES = 8
CONV_HALO = 16
CONV_TILE = 256
CONV_ROWS = 128
MOD_ROWS = 16
VMEM_LIMIT = 56 * 1024 * 1024

_DN_T = (((1,), (1,)), ((), ()))


def _params(n_axes, vmem=VMEM_LIMIT):
    return pltpu.CompilerParams(dimension_semantics=("arbitrary",) * n_axes, vmem_limit_bytes=vmem)


def _resident(block_shape, index_map):
    return pl.BlockSpec(block_shape, index_map, pipeline_mode=pl.Buffered(1))


def _rms(x):
    return x * lax.rsqrt(jnp.mean(x * x, axis=-1, keepdims=True) + EPS)


def _silu(x):
    return x * jax.nn.sigmoid(x)


def _row_tile(n_lat_rows, n_ctx_rows, seq):
    for t in (512, 256, 128):
        if seq % t == 0 and n_ctx_rows % t == 0 and n_lat_rows % t == 0:
            return t
    raise ValueError("unsupported token counts")


def _mod_kernel(c_ref, w_ref, b_ref, o_ref):
    a = _silu(c_ref[...]).astype(BF16)
    o_ref[0] = jnp.dot(a, w_ref[0].astype(BF16), preferred_element_type=F32) + b_ref[0]


def _modulation(c_all, mod_w, mod_b):
    depth, d, n = mod_w.shape
    tn = 1152
    return pl.pallas_call(
        _mod_kernel,
        out_shape=jax.ShapeDtypeStruct((depth, MOD_ROWS, n), F32),
        grid=(depth, n // tn),
        in_specs=[
            pl.BlockSpec((MOD_ROWS, d), lambda i, j: (0, 0)),
            pl.BlockSpec((1, d, tn), lambda i, j: (i, 0, j)),
            pl.BlockSpec((1, 1, tn), lambda i, j: (i, 0, j)),
        ],
        out_specs=pl.BlockSpec((1, MOD_ROWS, tn), lambda i, j: (i, 0, j)),
        compiler_params=_params(2),
        name="modulation",
    )(c_all, mod_w, mod_b.reshape(depth, 1, n))


def _ffn_kernel(*refs, slot, split):
    if split is None:
        x_ref, m_ref, g_ref, wg_ref, wu_ref, wd_ref, o_ref = refs
        x = x_ref[...]
    else:
        x_ref, x2_ref, m_ref, g_ref, wg_ref, wu_ref, wd_ref, o_ref = refs
        x = jnp.where(pl.program_id(0) < split, x_ref[...], x2_ref[...])
    shift = m_ref[3 * slot:3 * slot + 1, :]
    scale = m_ref[3 * slot + 1:3 * slot + 2, :]
    gate = m_ref[3 * slot + 2:3 * slot + 3, :]
    g_pre = g_ref[2 * slot:2 * slot + 1, :]
    g_post = g_ref[2 * slot + 1:2 * slot + 2, :]
    h = (_rms(x) * g_pre * (1.0 + scale) + shift).astype(BF16)
    gt = jnp.dot(h, wg_ref[...], preferred_element_type=F32)
    up = jnp.dot(h, wu_ref[...], preferred_element_type=F32)
    a = (_silu(gt) * up).astype(BF16)
    y = jnp.dot(a, wd_ref[...], preferred_element_type=F32)
    o_ref[...] = x + (FFN_RESIDUAL_WEIGHT * gate) * (_rms(y) * g_post)


def _group_map(tm, seq, n_batch):
    return lambda i: jnp.minimum((i * tm) // seq, n_batch)


def _ffn(x, mod, norm_g, wg, wu, wd, *, layer, half, n_rows, tm, seq, n_batch, x2=None):
    d = x.shape[1]
    f = wg.shape[-1]
    slot = 2 * half
    grp = _group_map(tm, seq, n_batch)
    if x2 is None:
        split = None
        xs, x_specs = [x], [pl.BlockSpec((tm, d), lambda i: (i, 0))]
    else:
        split = x.shape[0] // tm
        xs = [x, x2]
        x_specs = [pl.BlockSpec((tm, d), lambda i: (jnp.minimum(i, split - 1), 0)),
                   pl.BlockSpec((tm, d), lambda i: (jnp.maximum(i - split, 0), 0))]
    return pl.pallas_call(
        functools.partial(_ffn_kernel, slot=slot, split=split),
        out_shape=jax.ShapeDtypeStruct((n_rows, d), F32),
        grid=(n_rows // tm,),
        in_specs=x_specs + [
            pl.BlockSpec((None, None, N_MOD, d), lambda i: (layer, grp(i), 0, 0)),
            pl.BlockSpec((None, 6, d), lambda i: (layer, 0, 0)),
            _resident((None, None, d, f), lambda i: (layer, half, 0, 0)),
            _resident((None, None, d, f), lambda i: (layer, half, 0, 0)),
            _resident((None, None, f, d), lambda i: (layer, half, 0, 0)),
        ],
        out_specs=pl.BlockSpec((tm, d), lambda i: (i, 0)),
        compiler_params=_params(1),
        name=f"ffn_l{layer}h{half}",
    )(*xs, mod, norm_g, wg, wu, wd)


def _mixer_norm(x_ref, m_ref, g_ref):
    return (_rms(x_ref[...]) * g_ref[2:3, :] * (1.0 + m_ref[4:5, :]) + m_ref[3:4, :]).astype(BF16)


def _rotate_half(x, half):
    if 2 * half == LANES:
        return pltpu.roll(x, half, axis=1)
    lane = lax.broadcasted_iota(jnp.int32, x.shape, 1)
    return jnp.where(lane % (2 * half) < half, pltpu.roll(x, LANES - half, axis=1), pltpu.roll(x, half, axis=1))


def _store_values_transposed(vt_ref, v, dh):
    per = LANES // dh
    for j in range(v.shape[1] // LANES):
        t = v[:, j * LANES:(j + 1) * LANES].T
        for i in range(per):
            vt_ref[j * per + i, 0:dh, :] = t[i * dh:(i + 1) * dh, :].astype(BF16)
    vt_ref[:, dh:, :] = jnp.ones((vt_ref.shape[0], ONES_ROWS, vt_ref.shape[2]), BF16)


def _na_proj_kernel(x_ref, m_ref, g_ref, w_ref, o_ref, vt_ref):
    h = _mixer_norm(x_ref, m_ref, g_ref)
    p = jnp.dot(h, w_ref[...], preferred_element_type=F32)
    qd = NA_HEADS * NA_HEAD_DIM
    o_ref[:, :qd] = (p[:, :qd] * (NA_HEAD_DIM ** -0.5 * LOG2E)).astype(BF16)
    o_ref[:, qd:] = p[:, qd:2 * qd].astype(BF16)
    _store_values_transposed(vt_ref, p[:, 2 * qd:], NA_HEAD_DIM)


def _sw_proj_kernel(x_ref, m_ref, g_ref, w_ref, cos_ref, sin_ref, o_ref, vt_ref):
    h = _mixer_norm(x_ref, m_ref, g_ref)
    p = jnp.dot(h, w_ref[...], preferred_element_type=F32)
    cos = cos_ref[...]
    sin = sin_ref[...]
    n_q = SW_HEADS * SW_HEAD_DIM // LANES
    n_k = SW_KV_HEADS * SW_HEAD_DIM // LANES
    for j in range(n_q + n_k):
        xs = p[:, j * LANES:(j + 1) * LANES]
        r = xs * cos + _rotate_half(xs, SW_HEAD_DIM // 2) * sin
        if j < n_q:
            r = r * (SW_HEAD_DIM ** -0.5 * LOG2E)
        o_ref[:, j * LANES:(j + 1) * LANES] = r.astype(BF16)
    _store_values_transposed(vt_ref, p[:, (n_q + n_k) * LANES:], SW_HEAD_DIM)


def _ga_proj_kernel(x_ref, m_ref, g_ref, w_ref, cos_ref, sin_ref, qn_ref, kn_ref, o_ref, vt_ref):
    h = _mixer_norm(x_ref, m_ref, g_ref)
    p = jnp.dot(h, w_ref[...], preferred_element_type=F32)
    cos = cos_ref[...]
    sin = sin_ref[...]
    for j in range(GA_HEADS + GA_KV_HEADS):
        xs = _rms(p[:, j * LANES:(j + 1) * LANES])
        if j < GA_HEADS:
            xs = xs * (qn_ref[...] * (GA_HEAD_DIM ** -0.5 * LOG2E))
        else:
            xs = xs * kn_ref[...]
        r = xs * cos + _rotate_half(xs, GA_HEAD_DIM // 2) * sin
        o_ref[:, j * LANES:(j + 1) * LANES] = r.astype(BF16)
    _store_values_transposed(vt_ref, p[:, (GA_HEADS + GA_KV_HEADS) * LANES:], GA_HEAD_DIM)


def _cv_proj_kernel(x_ref, m_ref, g_ref, w_ref, b_ref, o_ref):
    h = _mixer_norm(x_ref, m_ref, g_ref)
    u = jnp.dot(h, w_ref[...], preferred_element_type=F32) + b_ref[...]
    d = o_ref.shape[1]
    o_ref[...] = u[:, :d] * jax.nn.sigmoid(u[:, d:])


def _proj(body, x, mod, norm_g, w, extras, extra_specs, *, layer, n_out, out_dtype, tm, seq, n_batch, name,
          vt_heads=0, vt_dim=0):
    n_rows, d = x.shape
    grp = _group_map(tm, seq, n_batch)
    out_shape = jax.ShapeDtypeStruct((n_rows, n_out), out_dtype)
    out_specs = pl.BlockSpec((tm, n_out), lambda i: (i, 0))
    if vt_heads:
        vrows = vt_dim + ONES_ROWS
        out_shape = [out_shape, jax.ShapeDtypeStruct((vt_heads, vrows, n_rows), BF16)]
        out_specs = [out_specs, pl.BlockSpec((vt_heads, vrows, tm), lambda i: (0, 0, i))]
    return pl.pallas_call(
        body,
        out_shape=out_shape,
        grid=(n_rows // tm,),
        in_specs=[
            pl.BlockSpec((tm, d), lambda i: (i, 0)),
            pl.BlockSpec((None, None, N_MOD, d), lambda i: (layer, grp(i), 0, 0)),
            pl.BlockSpec((None, 6, d), lambda i: (layer, 0, 0)),
            _resident(w.shape, lambda i: (0, 0)),
        ] + extra_specs,
        out_specs=out_specs,
        compiler_params=_params(1),
        name=name,
    )(x, mod, norm_g, w, *extras)


def _rope_specs(tm, seq, n_lat_rows):
    n_seq_blocks = seq // tm
    idx = lambda i: (jnp.where(i * tm < n_lat_rows, i % n_seq_blocks, n_seq_blocks), 0)
    return [pl.BlockSpec((tm, LANES), idx), pl.BlockSpec((tm, LANES), idx)]


def _rope_tables(seq, head_dim, tm):
    t = np.arange(seq)
    row = (t // GRID_W).astype(np.float32)
    col = (t % GRID_W).astype(np.float32)
    n_pairs_axis = head_dim // 4
    freq = jnp.asarray(ROPE_THETA, F32) ** (-jnp.arange(n_pairs_axis, dtype=F32) / n_pairs_axis)
    ang = jnp.concatenate([row[:, None] * freq, col[:, None] * freq], axis=-1)
    cos, sin = jnp.cos(ang), jnp.sin(ang)
    reps = LANES // head_dim
    cos_t = jnp.tile(jnp.concatenate([cos, cos], axis=-1), (1, reps))
    sin_t = jnp.tile(jnp.concatenate([-sin, sin], axis=-1), (1, reps))
    cos_t = jnp.concatenate([cos_t, jnp.ones((tm, LANES), F32)], axis=0)
    sin_t = jnp.concatenate([sin_t, jnp.zeros((tm, LANES), F32)], axis=0)
    return cos_t, sin_t


def _half_split_perm(n_heads, head_dim):
    per_head = np.concatenate([np.arange(0, head_dim, 2), np.arange(1, head_dim, 2)])
    return (np.arange(n_heads)[:, None] * head_dim + per_head[None, :]).reshape(-1)


def _attend(q_ref, o_ref, *, n_heads, dh, keys_of, values_of, addends_of, floor_of=None):
    per_store = LANES // dh

    def scores(hd):
        qh = q_ref[:, hd * dh:(hd + 1) * dh]
        ss = []
        for kb, adds in zip(keys_of(hd), addends_of(hd)):
            s = lax.dot_general(kb, qh, _DN_T, preferred_element_type=F32)
            for a in adds:
                s = s + a
            ss.append(s)
        return ss

    def finish(hd, ss):
        m = ss[0].max(axis=0, keepdims=True)
        for s in ss[1:]:
            m = jnp.maximum(m, s.max(axis=0, keepdims=True))
        floor = None if floor_of is None else floor_of(hd)
        if floor is not None:
            m = jnp.maximum(m, floor)
        ot = None
        for s, vb in zip(ss, values_of(hd)):
            part = jnp.dot(vb, jnp.exp2(s - m).astype(BF16), preferred_element_type=F32)
            ot = part if ot is None else ot + part
        l = ot[dh:dh + 1, :]
        if floor is not None:
            l = l + jnp.exp2(floor - m)
        return ot[:dh, :] / l

    pending = scores(0)
    outs = []
    for hd in range(n_heads):
        ss = pending
        if hd + 1 < n_heads:
            pending = scores(hd + 1)
        outs.append(finish(hd, ss))
        if len(outs) == per_store:
            o2 = outs[0] if per_store == 1 else jnp.concatenate(outs, axis=0)
            outs = []
            o_ref[:, (hd + 1 - per_store) * dh:(hd + 1) * dh] = o2.T.astype(BF16)


def _na_kernel(q_ref, kl_ref, kc_ref, vlt_ref, vct_ref, val_ref, mask_ref, o_ref, *, n_groups, seq):
    g = pl.program_id(1)
    dh = NA_HEAD_DIM
    tq = NA_QROWS * GRID_W
    n_keys = NA_KROWS * GRID_W
    n_edge = NA_WIN_R * GRID_W
    half = (NA_WIN_R // 2) * GRID_W
    hs = lambda hd: slice(hd * dh, (hd + 1) * dh)
    edge = (g == 0) | (g == n_groups - 1)

    @pl.when((g < n_groups) & jnp.logical_not(edge))
    def _():
        off = pl.multiple_of(g * tq - half, tq)
        _attend(q_ref, o_ref, n_heads=NA_HEADS, dh=dh,
                keys_of=lambda hd: [kl_ref[pl.ds(off, n_keys), hs(hd)], kc_ref[:, hs(hd)]],
                values_of=lambda hd: [vlt_ref[hd, :, pl.ds(off, n_keys)], vct_ref[hd]],
                addends_of=lambda hd: [[val_ref[hd], mask_ref[...]], []])

    @pl.when((g < n_groups) & edge)
    def _():
        koff = pl.multiple_of(jnp.where(g == 0, 0, seq - n_edge), tq)
        toff = pl.multiple_of(jnp.where(g == 0, half, 0), tq)
        _attend(q_ref, o_ref, n_heads=NA_HEADS, dh=dh,
                keys_of=lambda hd: [kl_ref[pl.ds(koff, n_edge), hs(hd)], kc_ref[:, hs(hd)]],
                values_of=lambda hd: [vlt_ref[hd, :, pl.ds(koff, n_edge)], vct_ref[hd]],
                addends_of=lambda hd: [[val_ref[hd, pl.ds(toff, n_edge), :], mask_ref[pl.ds(toff, n_edge), :]], []])

    @pl.when(g >= n_groups)
    def _():
        _attend(q_ref, o_ref, n_heads=NA_HEADS, dh=dh,
                keys_of=lambda hd: [kc_ref[:, hs(hd)]],
                values_of=lambda hd: [vct_ref[hd]],
                addends_of=lambda hd: [[]])


def _na_bias_tables(rpb, rows):
    qr = np.arange(NA_QROWS)
    kri = np.arange(NA_KROWS)
    j = np.arange(GRID_W)
    kc = np.arange(GRID_W)
    half = NA_WIN_R // 2
    ridx = np.clip(kri[:, None] - half - qr[None, :] + NA_WIN_R - 1, 0, 2 * NA_WIN_R - 2)
    cidx = np.clip(kc[:, None] - j[None, :] + NA_WIN_C - 1, 0, 2 * NA_WIN_C - 2)
    n_k, n_q = NA_KROWS * GRID_W, NA_QROWS * GRID_W
    onehot = (cidx[None] == np.arange(2 * NA_WIN_C - 1)[:, None, None]).astype(np.float32)
    val = jnp.einsum("hkqd,dcj->hkcqj", rpb.astype(F32)[:, ridx] * LOG2E, jnp.asarray(onehot),
                     precision=lax.Precision.HIGHEST).reshape(rpb.shape[0], n_k, n_q)
    ws = np.clip(j - NA_WIN_C // 2, 0, GRID_W - NA_WIN_C)
    col_ok = (kc[:, None] >= ws[None, :]) & (kc[:, None] < ws[None, :] + NA_WIN_C)
    masks = []
    for r0 in (0, NA_QROWS, rows - NA_QROWS):
        r = r0 + qr
        rs = np.clip(r - half, 0, rows - NA_WIN_R)
        k_abs = r0 - half + kri
        row_ok = (k_abs[:, None] >= rs[None, :]) & (k_abs[:, None] < rs[None, :] + NA_WIN_R)
        ok = row_ok[:, None, :, None] & col_ok[None, :, None, :]
        masks.append(np.where(ok, 0.0, NEG).reshape(n_k, n_q))
    return val, jnp.asarray(np.stack(masks), F32)


def _na_attention(qk, vt, rpb, *, n_batch, seq, ctx_len, with_ctx_out):
    rows = seq // GRID_W
    qd = NA_HEADS * NA_HEAD_DIM
    tq = NA_QROWS * GRID_W
    n_keys = NA_KROWS * GRID_W
    n_groups = seq // tq
    ctx_tiles = ctx_len // tq
    n_ctx_tiles = ctx_tiles if with_ctx_out else 0
    n_lat = n_batch * seq
    n_rows = (n_lat + n_batch * ctx_len) if with_ctx_out else n_lat
    ctx_blk0 = n_lat // ctx_len
    vrows = NA_HEAD_DIM + ONES_ROWS
    val, mask = _na_bias_tables(rpb, rows)

    def q_idx(b, g):
        return (jnp.where(g < n_groups, b * n_groups + g, n_batch * n_groups + b * ctx_tiles + (g - n_groups)), 0)

    def mask_idx(b, g):
        return (jnp.where(g == 0, 0, jnp.where(g >= n_groups - 1, 2, 1)), 0, 0)

    return pl.pallas_call(
        functools.partial(_na_kernel, n_groups=n_groups, seq=seq),
        out_shape=jax.ShapeDtypeStruct((n_rows, qd), BF16),
        grid=(n_batch, n_groups + n_ctx_tiles),
        in_specs=[
            pl.BlockSpec((tq, qd), q_idx),
            pl.BlockSpec((seq, qd), lambda b, g: (b, 1)),
            pl.BlockSpec((ctx_len, qd), lambda b, g: (ctx_blk0 + b, 1)),
            pl.BlockSpec((NA_HEADS, vrows, seq), lambda b, g: (0, 0, b)),
            pl.BlockSpec((NA_HEADS, vrows, ctx_len), lambda b, g: (0, 0, ctx_blk0 + b)),
            _resident((NA_HEADS, n_keys, tq), lambda b, g: (0, 0, 0)),
            pl.BlockSpec((None, n_keys, tq), mask_idx),
        ],
        out_specs=pl.BlockSpec((tq, qd), q_idx),
        compiler_params=_params(2),
        name="na_attention",
    )(qk, qk, qk, vt, vt, val, mask)


def _sw_kernel(sink_ref, q_ref, kl_ref, kc_ref, vlt_ref, vct_ref, mask_ref, o_ref, *, n_tiles, seq):
    n = pl.program_id(1)
    dh = SW_HEAD_DIM
    grp = SW_HEADS // SW_KV_HEADS
    win = SW_KWIN
    ks = lambda hd: slice((hd // grp) * dh, (hd // grp + 1) * dh)
    floor_of = lambda hd: sink_ref[hd] * LOG2E

    @pl.when(n < n_tiles)
    def _():
        ws = pl.multiple_of(jnp.clip(n * SW_QTILE - SW_WINDOW, 0, seq - win), LANES)
        _attend(q_ref, o_ref, n_heads=SW_HEADS, dh=dh,
                keys_of=lambda hd: [kl_ref[pl.ds(ws, win), ks(hd)], kc_ref[:, ks(hd)]],
                values_of=lambda hd: [vlt_ref[hd // grp, :, pl.ds(ws, win)], vct_ref[hd // grp]],
                addends_of=lambda hd: [[mask_ref[...]], []], floor_of=floor_of)

    @pl.when(n >= n_tiles)
    def _():
        _attend(q_ref, o_ref, n_heads=SW_HEADS, dh=dh,
                keys_of=lambda hd: [kc_ref[:, ks(hd)]],
                values_of=lambda hd: [vct_ref[hd // grp]],
                addends_of=lambda hd: [[]], floor_of=floor_of)


def _sw_attention(proj, vt, sink, *, n_batch, seq, ctx_len, with_ctx_out):
    qd = SW_HEADS * SW_HEAD_DIM
    tq = SW_QTILE
    win = SW_KWIN
    n_tiles = seq // tq
    ctx_tiles = ctx_len // tq
    n_ctx_tiles = ctx_tiles if with_ctx_out else 0
    lat_tiles = n_batch * n_tiles
    n_lat = n_batch * seq
    n_rows = (n_lat + n_batch * ctx_len) if with_ctx_out else n_lat
    k_col = qd // LANES
    ctx_blk0 = n_lat // ctx_len
    vrows = SW_HEAD_DIM + ONES_ROWS
    ki = np.arange(win)[:, None]
    qj = np.arange(tq)[None, :]
    mask = np.stack([np.where(np.abs(ki - qj - off) <= SW_WINDOW, 0.0, NEG) for off in (0, SW_WINDOW, 2 * SW_WINDOW)])
    mask = jnp.asarray(mask, F32)

    def q_idx(b, n, s):
        return (jnp.where(n < n_tiles, b * n_tiles + n, lat_tiles + b * ctx_tiles + (n - n_tiles)), 0)

    def mask_idx(b, n, s):
        nn = jnp.minimum(n, n_tiles - 1)
        ws_blk = jnp.clip(nn * (tq // SW_WINDOW) - 1, 0, (seq - win) // SW_WINDOW)
        return (nn * (tq // SW_WINDOW) - ws_blk, 0, 0)

    grid_spec = pltpu.PrefetchScalarGridSpec(
        num_scalar_prefetch=1,
        grid=(n_batch, n_tiles + n_ctx_tiles),
        in_specs=[
            pl.BlockSpec((tq, qd), q_idx),
            pl.BlockSpec((seq, LANES), lambda b, n, s: (b, k_col)),
            pl.BlockSpec((ctx_len, LANES), lambda b, n, s: (ctx_blk0 + b, k_col)),
            pl.BlockSpec((SW_KV_HEADS, vrows, seq), lambda b, n, s: (0, 0, b)),
            pl.BlockSpec((SW_KV_HEADS, vrows, ctx_len), lambda b, n, s: (0, 0, ctx_blk0 + b)),
            pl.BlockSpec((None, win, tq), mask_idx),
        ],
        out_specs=pl.BlockSpec((tq, qd), q_idx),
    )
    return pl.pallas_call(
        functools.partial(_sw_kernel, n_tiles=n_tiles, seq=seq),
        out_shape=jax.ShapeDtypeStruct((n_rows, qd), BF16),
        grid_spec=grid_spec,
        compiler_params=_params(2),
        name="sw_attention",
    )(sink.astype(F32), proj, proj, proj, vt, vt, mask)


def _ga_kernel(q_ref, kl_ref, kc_ref, vlt_ref, vct_ref, o_ref):
    dh = GA_HEAD_DIM
    grp = GA_HEADS // GA_KV_HEADS
    ks = lambda hd: slice((hd // grp) * dh, (hd // grp + 1) * dh)
    _attend(q_ref, o_ref, n_heads=GA_HEADS, dh=dh,
            keys_of=lambda hd: [kl_ref[:, ks(hd)], kc_ref[:, ks(hd)]],
            values_of=lambda hd: [vlt_ref[hd // grp], vct_ref[hd // grp]],
            addends_of=lambda hd: [[], []])


def _ga_attention(proj, vt, *, n_batch, seq, ctx_len):
    qd = GA_HEADS * GA_HEAD_DIM
    kvd = GA_KV_HEADS * GA_HEAD_DIM
    tq = GA_QTILE
    n_tiles = seq // tq
    n_lat = n_batch * seq
    k_col = qd // kvd
    ctx_blk0 = n_lat // ctx_len
    vrows = GA_HEAD_DIM + ONES_ROWS
    return pl.pallas_call(
        _ga_kernel,
        out_shape=jax.ShapeDtypeStruct((n_lat, qd), BF16),
        grid=(n_batch, n_tiles),
        in_specs=[
            pl.BlockSpec((tq, qd), lambda b, n: (b * n_tiles + n, 0)),
            pl.BlockSpec((seq, kvd), lambda b, n: (b, k_col)),
            pl.BlockSpec((ctx_len, kvd), lambda b, n: (ctx_blk0 + b, k_col)),
            pl.BlockSpec((GA_KV_HEADS, vrows, seq), lambda b, n: (0, 0, b)),
            pl.BlockSpec((GA_KV_HEADS, vrows, ctx_len), lambda b, n: (0, 0, ctx_blk0 + b)),
        ],
        out_specs=pl.BlockSpec((tq, qd), lambda b, n: (b * n_tiles + n, 0)),
        compiler_params=_params(2),
        name="ga_attention",
    )(proj, proj, proj, vt, vt)


def _conv_kernel(um_ref, up_ref, un_ref, w_ref, b_ref, lg_ref, lb_ref, o_ref, win_ref, sh_ref, acc_ref, *,
                 lat_tiles, ctx_tiles):
    j = pl.program_id(1)
    jc = j - lat_tiles
    first = jnp.where(j < lat_tiles, j == 0, jc == 0)
    last = jnp.where(j < lat_tiles, j == lat_tiles - 1, jc == ctx_tiles - 1)
    tl = um_ref.shape[0]
    d = um_ref.shape[1]
    sh_rows = sh_ref.shape[1]
    win_ref[0:CONV_HALO, :] = jnp.where(first, 0.0, up_ref[...])
    win_ref[CONV_HALO:CONV_HALO + tl, :] = um_ref[...]
    win_ref[CONV_HALO + tl:, :] = jnp.where(last, 0.0, un_ref[...])
    base = CONV_HALO - CONV_WIDTH // 2
    for s in range(d // LANES):
        ls = slice(s * LANES, (s + 1) * LANES)
        for r in range(1, SUBLANES):
            sh_ref[r] = win_ref[r:r + sh_rows, ls]
        for rb in range(tl // CONV_ROWS):
            acc = jnp.broadcast_to(b_ref[:, ls], (CONV_ROWS, LANES))
            for k in range(CONV_WIDTH):
                a8, r = divmod(base + k, SUBLANES)
                lo = a8 * SUBLANES + rb * CONV_ROWS
                src = win_ref[lo:lo + CONV_ROWS, ls] if r == 0 else sh_ref[r, lo:lo + CONV_ROWS, :]
                acc = acc + w_ref[k:k + 1, ls] * src
            acc_ref[rb * CONV_ROWS:(rb + 1) * CONV_ROWS, ls] = acc
    u = acc_ref[...]
    mu = jnp.mean(u, axis=-1, keepdims=True)
    var = jnp.mean(jnp.square(u - mu), axis=-1, keepdims=True)
    y = (u - mu) * lax.rsqrt(var + EPS) * lg_ref[...] + lb_ref[...]
    o_ref[...] = _silu(y).astype(BF16)


def _conv_core(u, w_dw, b_dw, ln_g, ln_b, *, n_batch, seq, ctx_len, with_ctx_out):
    d = u.shape[1]
    tl = CONV_TILE
    lat_tiles = seq // tl
    ctx_tiles = ctx_len // tl
    n_ctx = ctx_tiles if with_ctx_out else 0
    n_rows = (n_batch * seq + n_batch * ctx_len) if with_ctx_out else n_batch * seq
    halo_per_tile = tl // CONV_HALO
    n_halo_blocks = u.shape[0] // CONV_HALO
    sh_rows = tl + 2 * CONV_HALO - SUBLANES

    def main(b, j):
        return jnp.where(j < lat_tiles, b * lat_tiles + j, n_batch * lat_tiles + b * ctx_tiles + (j - lat_tiles))

    return pl.pallas_call(
        functools.partial(_conv_kernel, lat_tiles=lat_tiles, ctx_tiles=ctx_tiles),
        out_shape=jax.ShapeDtypeStruct((n_rows, d), BF16),
        grid=(n_batch, lat_tiles + n_ctx),
        in_specs=[
            pl.BlockSpec((tl, d), lambda b, j: (main(b, j), 0)),
            pl.BlockSpec((CONV_HALO, d), lambda b, j: (jnp.maximum(main(b, j) * halo_per_tile - 1, 0), 0)),
            pl.BlockSpec((CONV_HALO, d),
                         lambda b, j: (jnp.minimum((main(b, j) + 1) * halo_per_tile, n_halo_blocks - 1), 0)),
            pl.BlockSpec((CONV_WIDTH, d), lambda b, j: (0, 0)),
            pl.BlockSpec((1, d), lambda b, j: (0, 0)),
            pl.BlockSpec((1, d), lambda b, j: (0, 0)),
            pl.BlockSpec((1, d), lambda b, j: (0, 0)),
        ],
        out_specs=pl.BlockSpec((tl, d), lambda b, j: (main(b, j), 0)),
        scratch_shapes=[pltpu.VMEM((tl + 2 * CONV_HALO, d), F32), pltpu.VMEM((SUBLANES, sh_rows, LANES), F32),
                        pltpu.VMEM((tl, d), F32)],
        compiler_params=_params(2),
        name="conv_core",
    )(u, u, u, w_dw, b_dw.reshape(1, d), ln_g.reshape(1, d), ln_b.reshape(1, d))


def _out_kernel(x_ref, a_ref, m_ref, g_ref, w_ref, b_ref, o_ref):
    y = jnp.dot(a_ref[...], w_ref[...], preferred_element_type=F32) + b_ref[...]
    o_ref[...] = x_ref[...] + m_ref[5:6, :] * (_rms(y) * g_ref[3:4, :])


def _out_proj(x, a, mod, norm_g, w, bias, *, layer, n_rows, tm, seq, n_batch):
    d = x.shape[1]
    grp = _group_map(tm, seq, n_batch)
    return pl.pallas_call(
        _out_kernel,
        out_shape=jax.ShapeDtypeStruct((n_rows, d), F32),
        grid=(n_rows // tm,),
        in_specs=[
            pl.BlockSpec((tm, d), lambda i: (i, 0)),
            pl.BlockSpec((tm, a.shape[1]), lambda i: (i, 0)),
            pl.BlockSpec((None, None, N_MOD, d), lambda i: (layer, grp(i), 0, 0)),
            pl.BlockSpec((None, 6, d), lambda i: (layer, 0, 0)),
            _resident(w.shape, lambda i: (0, 0)),
            pl.BlockSpec((1, d), lambda i: (0, 0)),
        ],
        out_specs=pl.BlockSpec((tm, d), lambda i: (i, 0)),
        compiler_params=_params(1),
        name=f"out_proj_l{layer}",
    )(x, a, mod, norm_g, w, bias)


def kernel(x, c, ctx, c_ctx, mod_w, mod_b, norm_g, ffn_w_gate, ffn_w_up, ffn_w_down, na_w_qkv, na_w_o, na_rpb, sw_w_qkv, sw_w_o, sw_sink, cv_w_pw1, cv_b_pw1, cv_w_dw, cv_b_dw, cv_ln_g, cv_ln_b, cv_w_pw2, cv_b_pw2, ga_w_qkv, ga_w_o, ga_q_norm, ga_k_norm):
    n_batch, seq, d = x.shape
    ctx_len = ctx.shape[1]
    depth = mod_w.shape[0]
    n_lat = n_batch * seq
    n_ctx = n_batch * ctx_len
    n_all = n_lat + n_ctx
    assert d == D_MODEL and n_batch < MOD_ROWS and seq % CONV_TILE == 0 and ctx_len % CONV_TILE == 0
    assert seq % ctx_len == 0 and seq >= SW_KWIN and seq % SW_QTILE == 0 and ctx_len % SW_QTILE == 0
    assert (seq // GRID_W) % NA_QROWS == 0 and (seq // GRID_W) >= NA_KROWS and seq % GA_QTILE == 0
    tm = _row_tile(n_lat, n_ctx, seq)
    common = dict(tm=tm, seq=seq, n_batch=n_batch)

    c_all = jnp.concatenate([c, c_ctx[None, :], jnp.zeros((MOD_ROWS - n_batch - 1, d), F32)], axis=0)
    mod = _modulation(c_all, mod_w, mod_b).reshape(depth, MOD_ROWS, N_MOD, d)

    wg = ffn_w_gate.astype(BF16)
    wu = ffn_w_up.astype(BF16)
    wd = ffn_w_down.astype(BF16)
    zero_bias = jnp.zeros((1, d), F32)

    xs = x.reshape(n_lat, d)
    for i in range(depth):
        kind, j = i % N_MIXERS, i // N_MIXERS
        last = i == depth - 1
        ctx_in = not (last and kind == 2)
        ctx_out = not last
        rows_in = n_all if ctx_in else n_lat
        rows_out = n_all if ctx_out else n_lat
        if i == 0:
            xs = _ffn(xs, mod, norm_g, wg, wu, wd, layer=i, half=0, n_rows=rows_in, **common,
                      x2=ctx.reshape(n_ctx, d) if ctx_in else None)
        else:
            xs = _ffn(xs, mod, norm_g, wg, wu, wd, layer=i, half=0, n_rows=rows_in, **common)
        if kind == 0:
            qd = NA_HEADS * NA_HEAD_DIM
            qk, vt = _proj(_na_proj_kernel, xs, mod, norm_g, na_w_qkv[j].astype(BF16), [], [], layer=i,
                           n_out=2 * qd, out_dtype=BF16, name="na_proj", vt_heads=NA_HEADS, vt_dim=NA_HEAD_DIM,
                           **common)
            a = _na_attention(qk, vt, na_rpb[j], n_batch=n_batch, seq=seq, ctx_len=ctx_len, with_ctx_out=ctx_out)
            w_o, b_o = na_w_o[j], zero_bias
        elif kind == 1:
            qd, kvd = SW_HEADS * SW_HEAD_DIM, SW_KV_HEADS * SW_HEAD_DIM
            perm = np.concatenate([_half_split_perm(SW_HEADS, SW_HEAD_DIM),
                                   qd + _half_split_perm(SW_KV_HEADS, SW_HEAD_DIM),
                                   np.arange(qd + kvd, qd + 2 * kvd)])
            cos_t, sin_t = _rope_tables(seq, SW_HEAD_DIM, tm)
            proj, vt = _proj(_sw_proj_kernel, xs, mod, norm_g, sw_w_qkv[j][:, perm].astype(BF16), [cos_t, sin_t],
                             _rope_specs(tm, seq, n_lat), layer=i, n_out=qd + kvd, out_dtype=BF16,
                             name="sw_proj", vt_heads=SW_KV_HEADS, vt_dim=SW_HEAD_DIM, **common)
            a = _sw_attention(proj, vt, sw_sink[j], n_batch=n_batch, seq=seq, ctx_len=ctx_len,
                              with_ctx_out=ctx_out)
            w_o, b_o = sw_w_o[j], zero_bias
        elif kind == 2:
            u = _proj(_cv_proj_kernel, xs, mod, norm_g, cv_w_pw1[j].astype(BF16), [cv_b_pw1[j].reshape(1, -1)],
                      [pl.BlockSpec((1, 2 * d), lambda t: (0, 0))], layer=i, n_out=d, out_dtype=F32,
                      name="cv_proj", **common)
            a = _conv_core(u, cv_w_dw[j], cv_b_dw[j], cv_ln_g[j], cv_ln_b[j], n_batch=n_batch, seq=seq,
                           ctx_len=ctx_len, with_ctx_out=ctx_out)
            w_o, b_o = cv_w_pw2[j], cv_b_pw2[j].reshape(1, d)
        else:
            assert not ctx_out, "context outputs of the global mixer are not implemented"
            qd, kvd = GA_HEADS * GA_HEAD_DIM, GA_KV_HEADS * GA_HEAD_DIM
            perm = np.concatenate([_half_split_perm(GA_HEADS, GA_HEAD_DIM),
                                   qd + _half_split_perm(GA_KV_HEADS, GA_HEAD_DIM),
                                   np.arange(qd + kvd, qd + 2 * kvd)])
            hperm = _half_split_perm(1, GA_HEAD_DIM)
            cos_t, sin_t = _rope_tables(seq, GA_HEAD_DIM, tm)
            vec = pl.BlockSpec((1, GA_HEAD_DIM), lambda t: (0, 0))
            proj, vt = _proj(_ga_proj_kernel, xs, mod, norm_g, ga_w_qkv[j][:, perm].astype(BF16),
                             [cos_t, sin_t, ga_q_norm[j][hperm].reshape(1, -1), ga_k_norm[j][hperm].reshape(1, -1)],
                             _rope_specs(tm, seq, n_lat) + [vec, vec], layer=i, n_out=qd + kvd, out_dtype=BF16,
                             name="ga_proj", vt_heads=GA_KV_HEADS, vt_dim=GA_HEAD_DIM, **common)
            a = _ga_attention(proj, vt, n_batch=n_batch, seq=seq, ctx_len=ctx_len)
            w_o, b_o = ga_w_o[j], zero_bias
        xs = _out_proj(xs, a, mod, norm_g, w_o.astype(BF16), b_o, layer=i, n_rows=rows_out, **common)
        xs = _ffn(xs, mod, norm_g, wg, wu, wd, layer=i, half=1, n_rows=rows_out, **common)
    return xs[:n_lat].reshape(n_batch, seq, d)
```

```python
import functools

import numpy as np
import jax
import jax.numpy as jnp
from jax import lax
from jax.experimental import pallas as pl
from jax.experimental.pallas import tpu as pltpu

F32 = jnp.float32
BF16 = jnp.bfloat16

D_MODEL = 1024
DEPTH = 4
GRID_W = 64
N_MIXERS = 4
EPS = 1e-6
NEG = -1e30
LOG2E = 1.4426950408889634
ROPE_THETA = 10000.0
D_FF = 2816
FFN_RESIDUAL_WEIGHT = 0.5
N_MOD = 9
NA_HEADS = 16
NA_HEAD_DIM = 64
NA_WIN_R = 8
NA_WIN_C = 16
NA_QROWS = 4
NA_KROWS = NA_QROWS + NA_WIN_R
SW_HEADS = 16
SW_KV_HEADS = 2
SW_HEAD_DIM = 64
SW_WINDOW = 128
SW_QTILE = 256
SW_KWIN = SW_QTILE + 2 * SW_WINDOW
ONES_ROWS = 16
CONV_WIDTH = 31
GA_HEADS = 8
GA_KV_HEADS = 2
GA_HEAD_DIM = 128
GA_QTILE = 256

LANES = 128
SUBLANES = 8
CONV_HALO = 16
CONV_TILE = 256
CONV_ROWS = 128
MOD_ROWS = 16
VMEM_LIMIT = 56 * 1024 * 1024

_DN_T = (((1,), (1,)), ((), ()))


def _params(n_axes, vmem=VMEM_LIMIT):
    return pltpu.CompilerParams(dimension_semantics=("arbitrary",) * n_axes, vmem_limit_bytes=vmem)


def _resident(block_shape, index_map):
    return pl.BlockSpec(block_shape, index_map, pipeline_mode=pl.Buffered(1))


def _rms(x):
    return x * lax.rsqrt(jnp.mean(x * x, axis=-1, keepdims=True) + EPS)


def _silu(x):
    return x * jax.nn.sigmoid(x)


def _row_tile(n_lat_rows, n_ctx_rows, seq):
    for t in (512, 256, 128):
        if seq % t == 0 and n_ctx_rows % t == 0 and n_lat_rows % t == 0:
            return t
    raise ValueError("unsupported token counts")


def _mod_kernel(c_ref, w_ref, b_ref, o_ref):
    a = _silu(c_ref[...]).astype(BF16)
    o_ref[0] = jnp.dot(a, w_ref[0].astype(BF16), preferred_element_type=F32) + b_ref[0]


def _modulation(c_all, mod_w, mod_b):
    depth, d, n = mod_w.shape
    tn = 1152
    return pl.pallas_call(
        _mod_kernel,
        out_shape=jax.ShapeDtypeStruct((depth, MOD_ROWS, n), F32),
        grid=(depth, n // tn),
        in_specs=[
            pl.BlockSpec((MOD_ROWS, d), lambda i, j: (0, 0)),
            pl.BlockSpec((1, d, tn), lambda i, j: (i, 0, j)),
            pl.BlockSpec((1, 1, tn), lambda i, j: (i, 0, j)),
        ],
        out_specs=pl.BlockSpec((1, MOD_ROWS, tn), lambda i, j: (i, 0, j)),
        compiler_params=_params(2),
        name="modulation",
    )(c_all, mod_w, mod_b.reshape(depth, 1, n))


def _ffn_kernel(*refs, slot, split, mix, epilogue, n_in):
    ins, outs = iter(refs[:n_in]), refs[n_in:]
    x = next(ins)[...]
    if split is not None:
        x = jnp.where(pl.program_id(0) < split, x, next(ins)[...])
    if mix:
        a_ref, wo_ref, bo_ref = next(ins), next(ins), next(ins)
    m_ref, g_ref, wg_ref, wu_ref, wd_ref = (next(ins) for _ in range(5))
    if mix:
        ym = jnp.dot(a_ref[...], wo_ref[...], preferred_element_type=F32) + bo_ref[...]
        x = x + m_ref[5:6, :] * (_rms(ym) * g_ref[3:4, :])
    shift = m_ref[3 * slot:3 * slot + 1, :]
    scale = m_ref[3 * slot + 1:3 * slot + 2, :]
    gate = m_ref[3 * slot + 2:3 * slot + 3, :]
    g_pre = g_ref[2 * slot:2 * slot + 1, :]
    g_post = g_ref[2 * slot + 1:2 * slot + 2, :]
    h = (_rms(x) * g_pre * (1.0 + scale) + shift).astype(BF16)
    gt = jnp.dot(h, wg_ref[...], preferred_element_type=F32)
    up = jnp.dot(h, wu_ref[...], preferred_element_type=F32)
    a = (_silu(gt) * up).astype(BF16)
    y = jnp.dot(a, wd_ref[...], preferred_element_type=F32)
    x = x + (FFN_RESIDUAL_WEIGHT * gate) * (_rms(y) * g_post)
    outs[0][...] = x
    if epilogue is not None:
        wq_ref = next(ins)
        hq = (_rms(x) * g_ref[2:3, :] * (1.0 + m_ref[4:5, :]) + m_ref[3:4, :]).astype(BF16)
        epilogue(jnp.dot(hq, wq_ref[...], preferred_element_type=F32), list(ins), outs[1:])


def _group_map(tm, seq, n_batch):
    return lambda i: jnp.minimum((i * tm) // seq, n_batch)


def _ffn(x, mod, norm_g, wg, wu, wd, *, layer, half, n_rows, tm, seq, n_batch, x2=None, mixer_out=None, proj=None):
    d = x.shape[1]
    f = wg.shape[-1]
    slot = 2 * half
    grp = _group_map(tm, seq, n_batch)
    if x2 is None:
        split = None
        xs, x_specs = [x], [pl.BlockSpec((tm, d), lambda i: (i, 0))]
    else:
        split = x.shape[0] // tm
        xs = [x, x2]
        x_specs = [pl.BlockSpec((tm, d), lambda i: (jnp.minimum(i, split - 1), 0)),
                   pl.BlockSpec((tm, d), lambda i: (jnp.maximum(i - split, 0), 0))]
    if mixer_out is not None:
        a, w_o, b_o = mixer_out
        xs += [a, w_o, b_o]
        x_specs += [pl.BlockSpec((tm, a.shape[1]), lambda i: (i, 0)), _resident(w_o.shape, lambda i: (0, 0)),
                    pl.BlockSpec((1, d), lambda i: (0, 0))]
    xs += [mod, norm_g, wg, wu, wd]
    x_specs += [
        pl.BlockSpec((None, None, N_MOD, d), lambda i: (layer, grp(i), 0, 0)),
        pl.BlockSpec((None, 6, d), lambda i: (layer, 0, 0)),
        _resident((None, None, d, f), lambda i: (layer, half, 0, 0)),
        _resident((None, None, d, f), lambda i: (layer, half, 0, 0)),
        _resident((None, None, f, d), lambda i: (layer, half, 0, 0)),
    ]
    out_shape = [jax.ShapeDtypeStruct((n_rows, d), F32)]
    out_specs = [pl.BlockSpec((tm, d), lambda i: (i, 0))]
    epilogue = None
    if proj is not None:
        epilogue, w_qkv, extras, extra_specs, p_shapes, p_specs = proj
        xs += [w_qkv] + extras
        x_specs += [_resident(w_qkv.shape, lambda i: (0, 0))] + extra_specs
        out_shape += p_shapes
        out_specs += p_specs
    res = pl.pallas_call(
        functools.partial(_ffn_kernel, slot=slot, split=split, mix=mixer_out is not None, epilogue=epilogue,
                          n_in=len(xs)),
        out_shape=out_shape,
        grid=(n_rows // tm,),
        in_specs=x_specs,
        out_specs=out_specs,
        compiler_params=_params(1),
        name=f"ffn_l{layer}h{half}",
    )(*xs)
    return res[0] if proj is None else res


def _rotate_half(x, half):
    if 2 * half == LANES:
        return pltpu.roll(x, half, axis=1)
    lane = lax.broadcasted_iota(jnp.int32, x.shape, 1)
    return jnp.where(lane % (2 * half) < half, pltpu.roll(x, LANES - half, axis=1), pltpu.roll(x, half, axis=1))


def _store_values_transposed(vt_ref, v, dh):
    per = LANES // dh
    for j in range(v.shape[1] // LANES):
        t = v[:, j * LANES:(j + 1) * LANES].T
        for i in range(per):
            vt_ref[j * per + i, 0:dh, :] = t[i * dh:(i + 1) * dh, :].astype(BF16)
    vt_ref[:, dh:, :] = jnp.ones((vt_ref.shape[0], ONES_ROWS, vt_ref.shape[2]), BF16)


def _na_epilogue(p, extras, outs):
    o_ref, vt_ref = outs
    qd = NA_HEADS * NA_HEAD_DIM
    o_ref[:, :qd] = (p[:, :qd] * (NA_HEAD_DIM ** -0.5 * LOG2E)).astype(BF16)
    o_ref[:, qd:] = p[:, qd:2 * qd].astype(BF16)
    _store_values_transposed(vt_ref, p[:, 2 * qd:], NA_HEAD_DIM)


def _sw_epilogue(p, extras, outs):
    cos_ref, sin_ref = extras
    o_ref, vt_ref = outs
    cos = cos_ref[...]
    sin = sin_ref[...]
    n_q = SW_HEADS * SW_HEAD_DIM // LANES
    n_k = SW_KV_HEADS * SW_HEAD_DIM // LANES
    for j in range(n_q + n_k):
        xs = p[:, j * LANES:(j + 1) * LANES]
        r = xs * cos + _rotate_half(xs, SW_HEAD_DIM // 2) * sin
        if j < n_q:
            r = r * (SW_HEAD_DIM ** -0.5 * LOG2E)
        o_ref[:, j * LANES:(j + 1) * LANES] = r.astype(BF16)
    _store_values_transposed(vt_ref, p[:, (n_q + n_k) * LANES:], SW_HEAD_DIM)


def _ga_epilogue(p, extras, outs):
    cos_ref, sin_ref, qn_ref, kn_ref = extras
    o_ref, vt_ref = outs
    cos = cos_ref[...]
    sin = sin_ref[...]
    for j in range(GA_HEADS + GA_KV_HEADS):
        xs = _rms(p[:, j * LANES:(j + 1) * LANES])
        if j < GA_HEADS:
            xs = xs * (qn_ref[...] * (GA_HEAD_DIM ** -0.5 * LOG2E))
        else:
            xs = xs * kn_ref[...]
        r = xs * cos + _rotate_half(xs, GA_HEAD_DIM // 2) * sin
        o_ref[:, j * LANES:(j + 1) * LANES] = r.astype(BF16)
    _store_values_transposed(vt_ref, p[:, (GA_HEADS + GA_KV_HEADS) * LANES:], GA_HEAD_DIM)


def _cv_epilogue(p, extras, outs):
    (b_ref,), (o_ref,) = extras, outs
    u = p + b_ref[...]
    d = o_ref.shape[1]
    o_ref[...] = u[:, :d] * jax.nn.sigmoid(u[:, d:])


def _proj_outputs(n_rows, tm, n_out, out_dtype, vt_heads=0, vt_dim=0):
    shapes = [jax.ShapeDtypeStruct((n_rows, n_out), out_dtype)]
    specs = [pl.BlockSpec((tm, n_out), lambda i: (i, 0))]
    if vt_heads:
        vrows = vt_dim + ONES_ROWS
        shapes.append(jax.ShapeDtypeStruct((vt_heads, vrows, n_rows), BF16))
        specs.append(pl.BlockSpec((vt_heads, vrows, tm), lambda i: (0, 0, i)))
    return shapes, specs


def _rope_specs(tm, seq, n_lat_rows):
    n_seq_blocks = seq // tm
    idx = lambda i: (jnp.where(i * tm < n_lat_rows, i % n_seq_blocks, n_seq_blocks), 0)
    return [pl.BlockSpec((tm, LANES), idx), pl.BlockSpec((tm, LANES), idx)]


def _rope_tables(seq, head_dim, tm):
    t = np.arange(seq)
    row = (t // GRID_W).astype(np.float32)
    col = (t % GRID_W).astype(np.float32)
    n_pairs_axis = head_dim // 4
    freq = jnp.asarray(ROPE_THETA, F32) ** (-jnp.arange(n_pairs_axis, dtype=F32) / n_pairs_axis)
    ang = jnp.concatenate([row[:, None] * freq, col[:, None] * freq], axis=-1)
    cos, sin = jnp.cos(ang), jnp.sin(ang)
    reps = LANES // head_dim
    cos_t = jnp.tile(jnp.concatenate([cos, cos], axis=-1), (1, reps))
    sin_t = jnp.tile(jnp.concatenate([-sin, sin], axis=-1), (1, reps))
    cos_t = jnp.concatenate([cos_t, jnp.ones((tm, LANES), F32)], axis=0)
    sin_t = jnp.concatenate([sin_t, jnp.zeros((tm, LANES), F32)], axis=0)
    return cos_t, sin_t


def _half_split_perm(n_heads, head_dim):
    per_head = np.concatenate([np.arange(0, head_dim, 2), np.arange(1, head_dim, 2)])
    return (np.arange(n_heads)[:, None] * head_dim + per_head[None, :]).reshape(-1)


def _attend(q_ref, o_ref, *, n_heads, dh, keys_of, values_of, addends_of, floor_of=None):
    per_store = LANES // dh

    def scores(hd):
        qh = q_ref[:, hd * dh:(hd + 1) * dh]
        ss = []
        for kb, adds in zip(keys_of(hd), addends_of(hd)):
            s = lax.dot_general(kb, qh, _DN_T, preferred_element_type=F32)
            for a in adds:
                s = s + a
            ss.append(s)
        return ss

    def finish(hd, ss):
        m = ss[0].max(axis=0, keepdims=True)
        for s in ss[1:]:
            m = jnp.maximum(m, s.max(axis=0, keepdims=True))
        floor = None if floor_of is None else floor_of(hd)
        if floor is not None:
            m = jnp.maximum(m, floor)
        ot = None
        for s, vb in zip(ss, values_of(hd)):
            part = jnp.dot(vb, jnp.exp2(s - m).astype(BF16), preferred_element_type=F32)
            ot = part if ot is None else ot + part
        l = ot[dh:dh + 1, :]
        if floor is not None:
            l = l + jnp.exp2(floor - m)
        return ot[:dh, :] / l

    pending = scores(0)
    outs = []
    for hd in range(n_heads):
        ss = pending
        if hd + 1 < n_heads:
            pending = scores(hd + 1)
        outs.append(finish(hd, ss))
        if len(outs) == per_store:
            o2 = outs[0] if per_store == 1 else jnp.concatenate(outs, axis=0)
            outs = []
            o_ref[:, (hd + 1 - per_store) * dh:(hd + 1) * dh] = o2.T.astype(BF16)


def _na_kernel(q_ref, kl_ref, kc_ref, vlt_ref, vct_ref, val_ref, mask_ref, o_ref, *, n_groups, seq):
    g = pl.program_id(1)
    dh = NA_HEAD_DIM
    tq = NA_QROWS * GRID_W
    n_keys = NA_KROWS * GRID_W
    n_edge = NA_WIN_R * GRID_W
    half = (NA_WIN_R // 2) * GRID_W
    hs = lambda hd: slice(hd * dh, (hd + 1) * dh)
    edge = (g == 0) | (g == n_groups - 1)

    @pl.when((g < n_groups) & jnp.logical_not(edge))
    def _():
        off = pl.multiple_of(g * tq - half, tq)
        _attend(q_ref, o_ref, n_heads=NA_HEADS, dh=dh,
                keys_of=lambda hd: [kl_ref[pl.ds(off, n_keys), hs(hd)], kc_ref[:, hs(hd)]],
                values_of=lambda hd: [vlt_ref[hd, :, pl.ds(off, n_keys)], vct_ref[hd]],
                addends_of=lambda hd: [[val_ref[hd], mask_ref[...]], []])

    @pl.when((g < n_groups) & edge)
    def _():
        koff = pl.multiple_of(jnp.where(g == 0, 0, seq - n_edge), tq)
        toff = pl.multiple_of(jnp.where(g == 0, half, 0), tq)
        _attend(q_ref, o_ref, n_heads=NA_HEADS, dh=dh,
                keys_of=lambda hd: [kl_ref[pl.ds(koff, n_edge), hs(hd)], kc_ref[:, hs(hd)]],
                values_of=lambda hd: [vlt_ref[hd, :, pl.ds(koff, n_edge)], vct_ref[hd]],
                addends_of=lambda hd: [[val_ref[hd, pl.ds(toff, n_edge), :], mask_ref[pl.ds(toff, n_edge), :]], []])

    @pl.when(g >= n_groups)
    def _():
        _attend(q_ref, o_ref, n_heads=NA_HEADS, dh=dh,
                keys_of=lambda hd: [kc_ref[:, hs(hd)]],
                values_of=lambda hd: [vct_ref[hd]],
                addends_of=lambda hd: [[]])


def _na_bias_tables(rpb, rows):
    qr = np.arange(NA_QROWS)
    kri = np.arange(NA_KROWS)
    j = np.arange(GRID_W)
    kc = np.arange(GRID_W)
    half = NA_WIN_R // 2
    ridx = np.clip(kri[:, None] - half - qr[None, :] + NA_WIN_R - 1, 0, 2 * NA_WIN_R - 2)
    cidx = np.clip(kc[:, None] - j[None, :] + NA_WIN_C - 1, 0, 2 * NA_WIN_C - 2)
    n_k, n_q = NA_KROWS * GRID_W, NA_QROWS * GRID_W
    onehot = (cidx[None] == np.arange(2 * NA_WIN_C - 1)[:, None, None]).astype(np.float32)
    val = jnp.einsum("hkqd,dcj->hkcqj", rpb.astype(F32)[:, ridx] * LOG2E, jnp.asarray(onehot),
                     precision=lax.Precision.HIGHEST).reshape(rpb.shape[0], n_k, n_q)
    ws = np.clip(j - NA_WIN_C // 2, 0, GRID_W - NA_WIN_C)
    col_ok = (kc[:, None] >= ws[None, :]) & (kc[:, None] < ws[None, :] + NA_WIN_C)
    masks = []
    for r0 in (0, NA_QROWS, rows - NA_QROWS):
        r = r0 + qr
        rs = np.clip(r - half, 0, rows - NA_WIN_R)
        k_abs = r0 - half + kri
        row_ok = (k_abs[:, None] >= rs[None, :]) & (k_abs[:, None] < rs[None, :] + NA_WIN_R)
        ok = row_ok[:, None, :, None] & col_ok[None, :, None, :]
        masks.append(np.where(ok, 0.0, NEG).reshape(n_k, n_q))
    return val, jnp.asarray(np.stack(masks), F32)


def _na_attention(qk, vt, rpb, *, n_batch, seq, ctx_len, with_ctx_out):
    rows = seq // GRID_W
    qd = NA_HEADS * NA_HEAD_DIM
    tq = NA_QROWS * GRID_W
    n_keys = NA_KROWS * GRID_W
    n_groups = seq // tq
    ctx_tiles = ctx_len // tq
    n_ctx_tiles = ctx_tiles if with_ctx_out else 0
    n_lat = n_batch * seq
    n_rows = (n_lat + n_batch * ctx_len) if with_ctx_out else n_lat
    ctx_blk0 = n_lat // ctx_len
    vrows = NA_HEAD_DIM + ONES_ROWS
    val, mask = _na_bias_tables(rpb, rows)

    def q_idx(b, g):
        return (jnp.where(g < n_groups, b * n_groups + g, n_batch * n_groups + b * ctx_tiles + (g - n_groups)), 0)

    def mask_idx(b, g):
        return (jnp.where(g == 0, 0, jnp.where(g >= n_groups - 1, 2, 1)), 0, 0)

    return pl.pallas_call(
        functools.partial(_na_kernel, n_groups=n_groups, seq=seq),
        out_shape=jax.ShapeDtypeStruct((n_rows, qd), BF16),
        grid=(n_batch, n_groups + n_ctx_tiles),
        in_specs=[
            pl.BlockSpec((tq, qd), q_idx),
            pl.BlockSpec((seq, qd), lambda b, g: (b, 1)),
            pl.BlockSpec((ctx_len, qd), lambda b, g: (ctx_blk0 + b, 1)),
            pl.BlockSpec((NA_HEADS, vrows, seq), lambda b, g: (0, 0, b)),
            pl.BlockSpec((NA_HEADS, vrows, ctx_len), lambda b, g: (0, 0, ctx_blk0 + b)),
            _resident((NA_HEADS, n_keys, tq), lambda b, g: (0, 0, 0)),
            pl.BlockSpec((None, n_keys, tq), mask_idx),
        ],
        out_specs=pl.BlockSpec((tq, qd), q_idx),
        compiler_params=_params(2),
        name="na_attention",
    )(qk, qk, qk, vt, vt, val, mask)


def _sw_kernel(sink_ref, q_ref, kl_ref, kc_ref, vlt_ref, vct_ref, mask_ref, o_ref, *, n_tiles, seq):
    n = pl.program_id(1)
    dh = SW_HEAD_DIM
    grp = SW_HEADS // SW_KV_HEADS
    win = SW_KWIN
    ks = lambda hd: slice((hd // grp) * dh, (hd // grp + 1) * dh)
    floor_of = lambda hd: sink_ref[hd] * LOG2E

    @pl.when(n < n_tiles)
    def _():
        ws = pl.multiple_of(jnp.clip(n * SW_QTILE - SW_WINDOW, 0, seq - win), LANES)
        _attend(q_ref, o_ref, n_heads=SW_HEADS, dh=dh,
                keys_of=lambda hd: [kl_ref[pl.ds(ws, win), ks(hd)], kc_ref[:, ks(hd)]],
                values_of=lambda hd: [vlt_ref[hd // grp, :, pl.ds(ws, win)], vct_ref[hd // grp]],
                addends_of=lambda hd: [[mask_ref[...]], []], floor_of=floor_of)

    @pl.when(n >= n_tiles)
    def _():
        _attend(q_ref, o_ref, n_heads=SW_HEADS, dh=dh,
                keys_of=lambda hd: [kc_ref[:, ks(hd)]],
                values_of=lambda hd: [vct_ref[hd // grp]],
                addends_of=lambda hd: [[]], floor_of=floor_of)


def _sw_attention(proj, vt, sink, *, n_batch, seq, ctx_len, with_ctx_out):
    qd = SW_HEADS * SW_HEAD_DIM
    tq = SW_QTILE
    win = SW_KWIN
    n_tiles = seq // tq
    ctx_tiles = ctx_len // tq
    n_ctx_tiles = ctx_tiles if with_ctx_out else 0
    lat_tiles = n_batch * n_tiles
    n_lat = n_batch * seq
    n_rows = (n_lat + n_batch * ctx_len) if with_ctx_out else n_lat
    k_col = qd // LANES
    ctx_blk0 = n_lat // ctx_len
    vrows = SW_HEAD_DIM + ONES_ROWS
    ki = np.arange(win)[:, None]
    qj = np.arange(tq)[None, :]
    mask = np.stack([np.where(np.abs(ki - qj - off) <= SW_WINDOW, 0.0, NEG) for off in (0, SW_WINDOW, 2 * SW_WINDOW)])
    mask = jnp.asarray(mask, F32)

    def q_idx(b, n, s):
        return (jnp.where(n < n_tiles, b * n_tiles + n, lat_tiles + b * ctx_tiles + (n - n_tiles)), 0)

    def mask_idx(b, n, s):
        nn = jnp.minimum(n, n_tiles - 1)
        ws_blk = jnp.clip(nn * (tq // SW_WINDOW) - 1, 0, (seq - win) // SW_WINDOW)
        return (nn * (tq // SW_WINDOW) - ws_blk, 0, 0)

    grid_spec = pltpu.PrefetchScalarGridSpec(
        num_scalar_prefetch=1,
        grid=(n_batch, n_tiles + n_ctx_tiles),
        in_specs=[
            pl.BlockSpec((tq, qd), q_idx),
            pl.BlockSpec((seq, LANES), lambda b, n, s: (b, k_col)),
            pl.BlockSpec((ctx_len, LANES), lambda b, n, s: (ctx_blk0 + b, k_col)),
            pl.BlockSpec((SW_KV_HEADS, vrows, seq), lambda b, n, s: (0, 0, b)),
            pl.BlockSpec((SW_KV_HEADS, vrows, ctx_len), lambda b, n, s: (0, 0, ctx_blk0 + b)),
            pl.BlockSpec((None, win, tq), mask_idx),
        ],
        out_specs=pl.BlockSpec((tq, qd), q_idx),
    )
    return pl.pallas_call(
        functools.partial(_sw_kernel, n_tiles=n_tiles, seq=seq),
        out_shape=jax.ShapeDtypeStruct((n_rows, qd), BF16),
        grid_spec=grid_spec,
        compiler_params=_params(2),
        name="sw_attention",
    )(sink.astype(F32), proj, proj, proj, vt, vt, mask)


def _ga_kernel(q_ref, kl_ref, kc_ref, vlt_ref, vct_ref, o_ref):
    dh = GA_HEAD_DIM
    grp = GA_HEADS // GA_KV_HEADS
    ks = lambda hd: slice((hd // grp) * dh, (hd // grp + 1) * dh)
    _attend(q_ref, o_ref, n_heads=GA_HEADS, dh=dh,
            keys_of=lambda hd: [kl_ref[:, ks(hd)], kc_ref[:, ks(hd)]],
            values_of=lambda hd: [vlt_ref[hd // grp], vct_ref[hd // grp]],
            addends_of=lambda hd: [[], []])


def _ga_attention(proj, vt, *, n_batch, seq, ctx_len):
    qd = GA_HEADS * GA_HEAD_DIM
    kvd = GA_KV_HEADS * GA_HEAD_DIM
    tq = GA_QTILE
    n_tiles = seq // tq
    n_lat = n_batch * seq
    k_col = qd // kvd
    ctx_blk0 = n_lat // ctx_len
    vrows = GA_HEAD_DIM + ONES_ROWS
    return pl.pallas_call(
        _ga_kernel,
        out_shape=jax.ShapeDtypeStruct((n_lat, qd), BF16),
        grid=(n_batch, n_tiles),
        in_specs=[
            pl.BlockSpec((tq, qd), lambda b, n: (b * n_tiles + n, 0)),
            pl.BlockSpec((seq, kvd), lambda b, n: (b, k_col)),
            pl.BlockSpec((ctx_len, kvd), lambda b, n: (ctx_blk0 + b, k_col)),
            pl.BlockSpec((GA_KV_HEADS, vrows, seq), lambda b, n: (0, 0, b)),
            pl.BlockSpec((GA_KV_HEADS, vrows, ctx_len), lambda b, n: (0, 0, ctx_blk0 + b)),
        ],
        out_specs=pl.BlockSpec((tq, qd), lambda b, n: (b * n_tiles + n, 0)),
        compiler_params=_params(2),
        name="ga_attention",
    )(proj, proj, proj, vt, vt)


def _conv_kernel(um_ref, up_ref, un_ref, w_ref, b_ref, lg_ref, lb_ref, o_ref, win_ref, sh_ref, acc_ref, *,
                 lat_tiles, ctx_tiles):
    j = pl.program_id(1)
    jc = j - lat_tiles
    first = jnp.where(j < lat_tiles, j == 0, jc == 0)
    last = jnp.where(j < lat_tiles, j == lat_tiles - 1, jc == ctx_tiles - 1)
    tl = um_ref.shape[0]
    d = um_ref.shape[1]
    sh_rows = sh_ref.shape[1]
    win_ref[0:CONV_HALO, :] = jnp.where(first, 0.0, up_ref[...])
    win_ref[CONV_HALO:CONV_HALO + tl, :] = um_ref[...]
    win_ref[CONV_HALO + tl:, :] = jnp.where(last, 0.0, un_ref[...])
    base = CONV_HALO - CONV_WIDTH // 2

    def lane_slab(s, carry):
        ls = pl.ds(pl.multiple_of(s * LANES, LANES), LANES)
        for r in range(1, SUBLANES):
            sh_ref[r] = win_ref[r:r + sh_rows, ls]
        for rb in range(tl // CONV_ROWS):
            acc = jnp.broadcast_to(b_ref[:, ls], (CONV_ROWS, LANES))
            for k in range(CONV_WIDTH):
                a8, r = divmod(base + k, SUBLANES)
                lo = a8 * SUBLANES + rb * CONV_ROWS
                src = win_ref[lo:lo + CONV_ROWS, ls] if r == 0 else sh_ref[r, lo:lo + CONV_ROWS, :]
                acc = acc + w_ref[k:k + 1, ls] * src
            acc_ref[rb * CONV_ROWS:(rb + 1) * CONV_ROWS, ls] = acc
        return carry

    lax.fori_loop(0, d // LANES, lane_slab, 0)
    u = acc_ref[...]
    mu = jnp.mean(u, axis=-1, keepdims=True)
    var = jnp.mean(jnp.square(u - mu), axis=-1, keepdims=True)
    y = (u - mu) * lax.rsqrt(var + EPS) * lg_ref[...] + lb_ref[...]
    o_ref[...] = _silu(y).astype(BF16)


def _conv_core(u, w_dw, b_dw, ln_g, ln_b, *, n_batch, seq, ctx_len, with_ctx_out):
    d = u.shape[1]
    tl = CONV_TILE
    lat_tiles = seq // tl
    ctx_tiles = ctx_len // tl
    n_ctx = ctx_tiles if with_ctx_out else 0
    n_rows = (n_batch * seq + n_batch * ctx_len) if with_ctx_out else n_batch * seq
    halo_per_tile = tl // CONV_HALO
    n_halo_blocks = u.shape[0] // CONV_HALO
    sh_rows = tl + 2 * CONV_HALO - SUBLANES

    def main(b, j):
        return jnp.where(j < lat_tiles, b * lat_tiles + j, n_batch * lat_tiles + b * ctx_tiles + (j - lat_tiles))

    return pl.pallas_call(
        functools.partial(_conv_kernel, lat_tiles=lat_tiles, ctx_tiles=ctx_tiles),
        out_shape=jax.ShapeDtypeStruct((n_rows, d), BF16),
        grid=(n_batch, lat_tiles + n_ctx),
        in_specs=[
            pl.BlockSpec((tl, d), lambda b, j: (main(b, j), 0)),
            pl.BlockSpec((CONV_HALO, d), lambda b, j: (jnp.maximum(main(b, j) * halo_per_tile - 1, 0), 0)),
            pl.BlockSpec((CONV_HALO, d),
                         lambda b, j: (jnp.minimum((main(b, j) + 1) * halo_per_tile, n_halo_blocks - 1), 0)),
            pl.BlockSpec((CONV_WIDTH, d), lambda b, j: (0, 0)),
            pl.BlockSpec((1, d), lambda b, j: (0, 0)),
            pl.BlockSpec((1, d), lambda b, j: (0, 0)),
            pl.BlockSpec((1, d), lambda b, j: (0, 0)),
        ],
        out_specs=pl.BlockSpec((tl, d), lambda b, j: (main(b, j), 0)),
        scratch_shapes=[pltpu.VMEM((tl + 2 * CONV_HALO, d), F32), pltpu.VMEM((SUBLANES, sh_rows, LANES), F32),
                        pltpu.VMEM((tl, d), F32)],
        compiler_params=_params(2),
        name="conv_core",
    )(u, u, u, w_dw, b_dw.reshape(1, d), ln_g.reshape(1, d), ln_b.reshape(1, d))


def kernel(x, c, ctx, c_ctx, mod_w, mod_b, norm_g, ffn_w_gate, ffn_w_up, ffn_w_down, na_w_qkv, na_w_o, na_rpb, sw_w_qkv, sw_w_o, sw_sink, cv_w_pw1, cv_b_pw1, cv_w_dw, cv_b_dw, cv_ln_g, cv_ln_b, cv_w_pw2, cv_b_pw2, ga_w_qkv, ga_w_o, ga_q_norm, ga_k_norm):
    n_batch, seq, d = x.shape
    ctx_len = ctx.shape[1]
    depth = mod_w.shape[0]
    n_lat = n_batch * seq
    n_ctx = n_batch * ctx_len
    n_all = n_lat + n_ctx
    assert d == D_MODEL and n_batch < MOD_ROWS and seq % CONV_TILE == 0 and ctx_len % CONV_TILE == 0
    assert seq % ctx_len == 0 and seq >= SW_KWIN and seq % SW_QTILE == 0 and ctx_len % SW_QTILE == 0
    assert (seq // GRID_W) % NA_QROWS == 0 and (seq // GRID_W) >= NA_KROWS and seq % GA_QTILE == 0
    tm = _row_tile(n_lat, n_ctx, seq)
    common = dict(tm=tm, seq=seq, n_batch=n_batch)

    c_all = jnp.concatenate([c, c_ctx[None, :], jnp.zeros((MOD_ROWS - n_batch - 1, d), F32)], axis=0)
    mod = _modulation(c_all, mod_w, mod_b).reshape(depth, MOD_ROWS, N_MOD, d)

    wg = ffn_w_gate.astype(BF16)
    wu = ffn_w_up.astype(BF16)
    wd = ffn_w_down.astype(BF16)
    zero_bias = jnp.zeros((1, d), F32)

    xs = x.reshape(n_lat, d)
    for i in range(depth):
        kind, j = i % N_MIXERS, i // N_MIXERS
        last = i == depth - 1
        ctx_in = not (last and kind == 2)
        ctx_out = not last
        rows_in = n_all if ctx_in else n_lat
        rows_out = n_all if ctx_out else n_lat
        if kind == 0:
            qd = NA_HEADS * NA_HEAD_DIM
            proj = (_na_epilogue, na_w_qkv[j].astype(BF16), [], [],
                    *_proj_outputs(rows_in, tm, 2 * qd, BF16, NA_HEADS, NA_HEAD_DIM))
        elif kind == 1:
            qd, kvd = SW_HEADS * SW_HEAD_DIM, SW_KV_HEADS * SW_HEAD_DIM
            perm = np.concatenate([_half_split_perm(SW_HEADS, SW_HEAD_DIM),
                                   qd + _half_split_perm(SW_KV_HEADS, SW_HEAD_DIM),
                                   np.arange(qd + kvd, qd + 2 * kvd)])
            cos_t, sin_t = _rope_tables(seq, SW_HEAD_DIM, tm)
            proj = (_sw_epilogue, sw_w_qkv[j][:, perm].astype(BF16), [cos_t, sin_t], _rope_specs(tm, seq, n_lat),
                    *_proj_outputs(rows_in, tm, qd + kvd, BF16, SW_KV_HEADS, SW_HEAD_DIM))
        elif kind == 2:
            proj = (_cv_epilogue, cv_w_pw1[j].astype(BF16), [cv_b_pw1[j].reshape(1, -1)],
                    [pl.BlockSpec((1, 2 * d), lambda t: (0, 0))], *_proj_outputs(rows_in, tm, d, F32))
        else:
            assert not ctx_out, "context outputs of the global mixer are not implemented"
            qd, kvd = GA_HEADS * GA_HEAD_DIM, GA_KV_HEADS * GA_HEAD_DIM
            perm = np.concatenate([_half_split_perm(GA_HEADS, GA_HEAD_DIM),
                                   qd + _half_split_perm(GA_KV_HEADS, GA_HEAD_DIM),
                                   np.arange(qd + kvd, qd + 2 * kvd)])
            hperm = _half_split_perm(1, GA_HEAD_DIM)
            cos_t, sin_t = _rope_tables(seq, GA_HEAD_DIM, tm)
            vec = pl.BlockSpec((1, GA_HEAD_DIM), lambda t: (0, 0))
            proj = (_ga_epilogue, ga_w_qkv[j][:, perm].astype(BF16),
                    [cos_t, sin_t, ga_q_norm[j][hperm].reshape(1, -1), ga_k_norm[j][hperm].reshape(1, -1)],
                    _rope_specs(tm, seq, n_lat) + [vec, vec],
                    *_proj_outputs(rows_in, tm, qd + kvd, BF16, GA_KV_HEADS, GA_HEAD_DIM))
        x2 = ctx.reshape(n_ctx, d) if (i == 0 and ctx_in) else None
        xs, *pj = _ffn(xs, mod, norm_g, wg, wu, wd, layer=i, half=0, n_rows=rows_in, **common, x2=x2, proj=proj)
        if kind == 0:
            a = _na_attention(*pj, na_rpb[j], n_batch=n_batch, seq=seq, ctx_len=ctx_len, with_ctx_out=ctx_out)
            w_o, b_o = na_w_o[j], zero_bias
        elif kind == 1:
            a = _sw_attention(*pj, sw_sink[j], n_batch=n_batch, seq=seq, ctx_len=ctx_len, with_ctx_out=ctx_out)
            w_o, b_o = sw_w_o[j], zero_bias
        elif kind == 2:
            a = _conv_core(*pj, cv_w_dw[j], cv_b_dw[j], cv_ln_g[j], cv_ln_b[j], n_batch=n_batch, seq=seq,
                           ctx_len=ctx_len, with_ctx_out=ctx_out)
            w_o, b_o = cv_w_pw2[j], cv_b_pw2[j].reshape(1, d)
        else:
            a = _ga_attention(*pj, n_batch=n_batch, seq=seq, ctx_len=ctx_len)
            w_o, b_o = ga_w_o[j], zero_bias
        xs = _ffn(xs, mod, norm_g, wg, wu, wd, layer=i, half=1, n_rows=rows_out, **common,
                  mixer_out=(a, w_o.astype(BF16), b_o))
    return xs[:n_lat].reshape(n_batch, seq, d)
```

```python
import functools

import numpy as np
import jax
import jax.numpy as jnp
from jax import lax
from jax.experimental import pallas as pl
from jax.experimental.pallas import tpu as pltpu

F32 = jnp.float32
BF16 = jnp.bfloat16

D_MODEL = 1024
DEPTH = 4
GRID_W = 64
N_MIXERS = 4
EPS = 1e-6
NEG = -1e30
LOG2E = 1.4426950408889634
ROPE_THETA = 10000.0
D_FF = 2816
FFN_RESIDUAL_WEIGHT = 0.5
N_MOD = 9
NA_HEADS = 16
NA_HEAD_DIM = 64
NA_WIN_R = 8
NA_WIN_C = 16
NA_QROWS = 4
NA_KROWS = NA_QROWS + NA_WIN_R
SW_HEADS = 16
SW_KV_HEADS = 2
SW_HEAD_DIM = 64
SW_WINDOW = 128
SW_QTILE = 256
SW_KWIN = SW_QTILE + 2 * SW_WINDOW
ONES_ROWS = 16
NA_LOOKAHEAD, SW_LOOKAHEAD, GA_LOOKAHEAD = 2, 3, 2
CONV_WIDTH = 31
GA_HEADS = 8
GA_KV_HEADS = 2
GA_HEAD_DIM = 128
GA_QTILE = 256

LANES = 128
SUBLANES = 8
CONV_HALO = 16
CONV_TILE = 256
CONV_ROWS = 128
PROJ_SUB_BLOCKS = 2
MOD_ROWS = 16
VMEM_LIMIT = 56 * 1024 * 1024

_DN_T = (((1,), (1,)), ((), ()))


def _params(n_axes, vmem=VMEM_LIMIT):
    return pltpu.CompilerParams(dimension_semantics=("arbitrary",) * n_axes, vmem_limit_bytes=vmem)


def _resident(block_shape, index_map):
    return pl.BlockSpec(block_shape, index_map, pipeline_mode=pl.Buffered(1))


def _rms(x):
    return x * lax.rsqrt(jnp.mean(x * x, axis=-1, keepdims=True) + EPS)


def _silu(x):
    return x * jax.nn.sigmoid(x)


def _row_tile(n_lat_rows, n_ctx_rows, seq):
    for t in (512, 256, 128):
        if seq % t == 0 and n_ctx_rows % t == 0 and n_lat_rows % t == 0:
            return t
    raise ValueError("unsupported token counts")


def _mod_kernel(c_ref, w_ref, b_ref, o_ref):
    a = _silu(c_ref[...]).astype(BF16)
    o_ref[0] = jnp.dot(a, w_ref[0].astype(BF16), preferred_element_type=F32) + b_ref[0]


def _modulation(c_all, mod_w, mod_b):
    depth, d, n = mod_w.shape
    tn = 1152
    return pl.pallas_call(
        _mod_kernel,
        out_shape=jax.ShapeDtypeStruct((depth, MOD_ROWS, n), F32),
        grid=(depth, n // tn),
        in_specs=[
            pl.BlockSpec((MOD_ROWS, d), lambda i, j: (0, 0)),
            pl.BlockSpec((1, d, tn), lambda i, j: (i, 0, j)),
            pl.BlockSpec((1, 1, tn), lambda i, j: (i, 0, j)),
        ],
        out_specs=pl.BlockSpec((1, MOD_ROWS, tn), lambda i, j: (i, 0, j)),
        compiler_params=_params(2),
        name="modulation",
    )(c_all, mod_w, mod_b.reshape(depth, 1, n))


def _ffn_kernel(*refs, slot, split, mix, epilogue, n_in, n_sub):
    ins, outs = iter(refs[:n_in]), refs[n_in:]
    x_ref = next(ins)
    x2_ref = next(ins) if split is not None else None
    if mix:
        a_ref, wo_ref, bo_ref = next(ins), next(ins), next(ins)
    m_ref, g_ref, wg_ref, wu_ref, wd_ref = (next(ins) for _ in range(5))
    if epilogue is not None:
        wq_ref = next(ins)
        extras = list(ins)
    shift = m_ref[3 * slot:3 * slot + 1, :]
    scale = m_ref[3 * slot + 1:3 * slot + 2, :]
    gate = m_ref[3 * slot + 2:3 * slot + 3, :]
    g_pre = g_ref[2 * slot:2 * slot + 1, :]
    g_post = g_ref[2 * slot + 1:2 * slot + 2, :]
    sub = x_ref.shape[0] // n_sub
    for r in range(n_sub):
        rows = slice(r * sub, (r + 1) * sub)
        x = x_ref[rows, :]
        if split is not None:
            x = jnp.where(pl.program_id(0) < split, x, x2_ref[rows, :])
        if mix:
            ym = jnp.dot(a_ref[rows, :], wo_ref[...], preferred_element_type=F32) + bo_ref[...]
            x = x + m_ref[5:6, :] * (_rms(ym) * g_ref[3:4, :])
        h = (_rms(x) * g_pre * (1.0 + scale) + shift).astype(BF16)
        gt = jnp.dot(h, wg_ref[...], preferred_element_type=F32)
        up = jnp.dot(h, wu_ref[...], preferred_element_type=F32)
        a = (_silu(gt) * up).astype(BF16)
        y = jnp.dot(a, wd_ref[...], preferred_element_type=F32)
        x = x + (FFN_RESIDUAL_WEIGHT * gate) * (_rms(y) * g_post)
        outs[0][rows, :] = x
        if epilogue is not None:
            hq = (_rms(x) * g_ref[2:3, :] * (1.0 + m_ref[4:5, :]) + m_ref[3:4, :]).astype(BF16)
            epilogue(jnp.dot(hq, wq_ref[...], preferred_element_type=F32), extras, outs[1:], rows)


def _group_map(tm, seq, n_batch):
    return lambda i: jnp.minimum((i * tm) // seq, n_batch)


def _ffn(x, mod, norm_g, wg, wu, wd, *, layer, half, n_rows, tm, seq, n_batch, x2=None, mixer_out=None, proj=None):
    d = x.shape[1]
    f = wg.shape[-1]
    slot = 2 * half
    grp = _group_map(tm, seq, n_batch)
    if x2 is None:
        split = None
        xs, x_specs = [x], [pl.BlockSpec((tm, d), lambda i: (i, 0))]
    else:
        split = x.shape[0] // tm
        xs = [x, x2]
        x_specs = [pl.BlockSpec((tm, d), lambda i: (jnp.minimum(i, split - 1), 0)),
                   pl.BlockSpec((tm, d), lambda i: (jnp.maximum(i - split, 0), 0))]
    if mixer_out is not None:
        a, w_o, b_o = mixer_out
        xs += [a, w_o, b_o]
        x_specs += [pl.BlockSpec((tm, a.shape[1]), lambda i: (i, 0)), _resident(w_o.shape, lambda i: (0, 0)),
                    pl.BlockSpec((1, d), lambda i: (0, 0))]
    xs += [mod, norm_g, wg, wu, wd]
    x_specs += [
        pl.BlockSpec((None, None, N_MOD, d), lambda i: (layer, grp(i), 0, 0)),
        pl.BlockSpec((None, 6, d), lambda i: (layer, 0, 0)),
        _resident((None, None, d, f), lambda i: (layer, half, 0, 0)),
        _resident((None, None, d, f), lambda i: (layer, half, 0, 0)),
        _resident((None, None, f, d), lambda i: (layer, half, 0, 0)),
    ]
    out_shape = [jax.ShapeDtypeStruct((n_rows, d), F32)]
    out_specs = [pl.BlockSpec((tm, d), lambda i: (i, 0))]
    epilogue = None
    if proj is not None:
        epilogue, w_qkv, extras, extra_specs, p_shapes, p_specs = proj
        xs += [w_qkv] + extras
        x_specs += [_resident(w_qkv.shape, lambda i: (0, 0))] + extra_specs
        out_shape += p_shapes
        out_specs += p_specs
    res = pl.pallas_call(
        functools.partial(_ffn_kernel, slot=slot, split=split, mix=mixer_out is not None, epilogue=epilogue,
                          n_in=len(xs), n_sub=PROJ_SUB_BLOCKS if (proj is not None and tm >= 512) else 1),
        out_shape=out_shape,
        grid=(n_rows // tm,),
        in_specs=x_specs,
        out_specs=out_specs,
        compiler_params=_params(1),
        name=f"ffn_l{layer}h{half}",
    )(*xs)
    return res[0] if proj is None else res


def _rotate_half(x, half):
    if 2 * half == LANES:
        return pltpu.roll(x, half, axis=1)
    lane = lax.broadcasted_iota(jnp.int32, x.shape, 1)
    return jnp.where(lane % (2 * half) < half, pltpu.roll(x, LANES - half, axis=1), pltpu.roll(x, half, axis=1))


def _store_values_transposed(vt_ref, v, dh, rows):
    per = LANES // dh
    for j in range(v.shape[1] // LANES):
        t = v[:, j * LANES:(j + 1) * LANES].T
        for i in range(per):
            vt_ref[j * per + i, 0:dh, rows] = t[i * dh:(i + 1) * dh, :].astype(BF16)
    vt_ref[:, dh:, rows] = jnp.ones((vt_ref.shape[0], ONES_ROWS, v.shape[0]), BF16)


def _na_epilogue(p, extras, outs, rows):
    o_ref, vt_ref = outs
    qd = NA_HEADS * NA_HEAD_DIM
    o_ref[rows, :qd] = (p[:, :qd] * (NA_HEAD_DIM ** -0.5 * LOG2E)).astype(BF16)
    o_ref[rows, qd:] = p[:, qd:2 * qd].astype(BF16)
    _store_values_transposed(vt_ref, p[:, 2 * qd:], NA_HEAD_DIM, rows)


def _sw_epilogue(p, extras, outs, rows):
    cos_ref, sin_ref = extras
    o_ref, vt_ref = outs
    cos = cos_ref[rows, :]
    sin = sin_ref[rows, :]
    n_q = SW_HEADS * SW_HEAD_DIM // LANES
    n_k = SW_KV_HEADS * SW_HEAD_DIM // LANES
    for j in range(n_q + n_k):
        xs = p[:, j * LANES:(j + 1) * LANES]
        r = xs * cos + _rotate_half(xs, SW_HEAD_DIM // 2) * sin
        if j < n_q:
            r = r * (SW_HEAD_DIM ** -0.5 * LOG2E)
        o_ref[rows, j * LANES:(j + 1) * LANES] = r.astype(BF16)
    _store_values_transposed(vt_ref, p[:, (n_q + n_k) * LANES:], SW_HEAD_DIM, rows)


def _ga_epilogue(p, extras, outs, rows):
    cos_ref, sin_ref, qn_ref, kn_ref = extras
    o_ref, vt_ref = outs
    cos = cos_ref[rows, :]
    sin = sin_ref[rows, :]
    for j in range(GA_HEADS + GA_KV_HEADS):
        xs = _rms(p[:, j * LANES:(j + 1) * LANES])
        if j < GA_HEADS:
            xs = xs * (qn_ref[...] * (GA_HEAD_DIM ** -0.5 * LOG2E))
        else:
            xs = xs * kn_ref[...]
        r = xs * cos + _rotate_half(xs, GA_HEAD_DIM // 2) * sin
        o_ref[rows, j * LANES:(j + 1) * LANES] = r.astype(BF16)
    _store_values_transposed(vt_ref, p[:, (GA_HEADS + GA_KV_HEADS) * LANES:], GA_HEAD_DIM, rows)


def _cv_epilogue(p, extras, outs, rows):
    (b_ref,), (o_ref,) = extras, outs
    u = p + b_ref[...]
    d = o_ref.shape[1]
    o_ref[rows, :] = u[:, :d] * jax.nn.sigmoid(u[:, d:])


def _proj_outputs(n_rows, tm, n_out, out_dtype, vt_heads=0, vt_dim=0):
    shapes = [jax.ShapeDtypeStruct((n_rows, n_out), out_dtype)]
    specs = [pl.BlockSpec((tm, n_out), lambda i: (i, 0))]
    if vt_heads:
        vrows = vt_dim + ONES_ROWS
        shapes.append(jax.ShapeDtypeStruct((vt_heads, vrows, n_rows), BF16))
        specs.append(pl.BlockSpec((vt_heads, vrows, tm), lambda i: (0, 0, i)))
    return shapes, specs


def _rope_specs(tm, seq, n_lat_rows):
    n_seq_blocks = seq // tm
    idx = lambda i: (jnp.where(i * tm < n_lat_rows, i % n_seq_blocks, n_seq_blocks), 0)
    return [pl.BlockSpec((tm, LANES), idx), pl.BlockSpec((tm, LANES), idx)]


def _rope_tables(seq, head_dim, tm):
    t = np.arange(seq)
    row = (t // GRID_W).astype(np.float32)
    col = (t % GRID_W).astype(np.float32)
    n_pairs_axis = head_dim // 4
    freq = jnp.asarray(ROPE_THETA, F32) ** (-jnp.arange(n_pairs_axis, dtype=F32) / n_pairs_axis)
    ang = jnp.concatenate([row[:, None] * freq, col[:, None] * freq], axis=-1)
    cos, sin = jnp.cos(ang), jnp.sin(ang)
    reps = LANES // head_dim
    cos_t = jnp.tile(jnp.concatenate([cos, cos], axis=-1), (1, reps))
    sin_t = jnp.tile(jnp.concatenate([-sin, sin], axis=-1), (1, reps))
    cos_t = jnp.concatenate([cos_t, jnp.ones((tm, LANES), F32)], axis=0)
    sin_t = jnp.concatenate([sin_t, jnp.zeros((tm, LANES), F32)], axis=0)
    return cos_t, sin_t


def _half_split_perm(n_heads, head_dim):
    per_head = np.concatenate([np.arange(0, head_dim, 2), np.arange(1, head_dim, 2)])
    return (np.arange(n_heads)[:, None] * head_dim + per_head[None, :]).reshape(-1)


def _attend(q_ref, o_ref, *, n_heads, dh, lookahead, keys_of, values_of, addends_of, floor_of=None):
    per_store = LANES // dh

    def scores(hd):
        qh = q_ref[:, hd * dh:(hd + 1) * dh]
        ss = []
        for kb, adds in zip(keys_of(hd), addends_of(hd)):
            s = lax.dot_general(kb, qh, _DN_T, preferred_element_type=F32)
            for a in adds:
                s = s + a
            ss.append(s)
        return ss

    def finish(hd, ss):
        m = ss[0].max(axis=0, keepdims=True)
        for s in ss[1:]:
            m = jnp.maximum(m, s.max(axis=0, keepdims=True))
        floor = None if floor_of is None else floor_of(hd)
        if floor is not None:
            m = jnp.maximum(m, floor)
        ot = None
        for s, vb in zip(ss, values_of(hd)):
            part = jnp.dot(vb, jnp.exp2(s - m).astype(BF16), preferred_element_type=F32)
            ot = part if ot is None else ot + part
        l = ot[dh:dh + 1, :]
        if floor is not None:
            l = l + jnp.exp2(floor - m)
        return ot[:dh, :] / l

    pending = [scores(hd) for hd in range(min(lookahead, n_heads))]
    outs = []
    for hd in range(n_heads):
        ss = pending.pop(0)
        if hd + lookahead < n_heads:
            pending.append(scores(hd + lookahead))
        outs.append(finish(hd, ss))
        if len(outs) == per_store:
            o2 = outs[0] if per_store == 1 else jnp.concatenate(outs, axis=0)
            outs = []
            o_ref[:, (hd + 1 - per_store) * dh:(hd + 1) * dh] = o2.T.astype(BF16)


def _na_kernel(q_ref, kl_ref, kc_ref, vlt_ref, vct_ref, val_ref, mask_ref, o_ref, *, n_groups, seq):
    g = pl.program_id(1)
    dh = NA_HEAD_DIM
    tq = NA_QROWS * GRID_W
    n_keys = NA_KROWS * GRID_W
    n_edge = NA_WIN_R * GRID_W
    half = (NA_WIN_R // 2) * GRID_W
    hs = lambda hd: slice(hd * dh, (hd + 1) * dh)
    edge = (g == 0) | (g == n_groups - 1)

    @pl.when((g < n_groups) & jnp.logical_not(edge))
    def _():
        off = pl.multiple_of(g * tq - half, tq)
        _attend(q_ref, o_ref, n_heads=NA_HEADS, dh=dh, lookahead=NA_LOOKAHEAD,
                keys_of=lambda hd: [kl_ref[pl.ds(off, n_keys), hs(hd)], kc_ref[:, hs(hd)]],
                values_of=lambda hd: [vlt_ref[hd, :, pl.ds(off, n_keys)], vct_ref[hd]],
                addends_of=lambda hd: [[val_ref[hd], mask_ref[...]], []])

    @pl.when((g < n_groups) & edge)
    def _():
        koff = pl.multiple_of(jnp.where(g == 0, 0, seq - n_edge), tq)
        toff = pl.multiple_of(jnp.where(g == 0, half, 0), tq)
        _attend(q_ref, o_ref, n_heads=NA_HEADS, dh=dh, lookahead=NA_LOOKAHEAD,
                keys_of=lambda hd: [kl_ref[pl.ds(koff, n_edge), hs(hd)], kc_ref[:, hs(hd)]],
                values_of=lambda hd: [vlt_ref[hd, :, pl.ds(koff, n_edge)], vct_ref[hd]],
                addends_of=lambda hd: [[val_ref[hd, pl.ds(toff, n_edge), :], mask_ref[pl.ds(toff, n_edge), :]], []])

    @pl.when(g >= n_groups)
    def _():
        _attend(q_ref, o_ref, n_heads=NA_HEADS, dh=dh, lookahead=NA_LOOKAHEAD,
                keys_of=lambda hd: [kc_ref[:, hs(hd)]],
                values_of=lambda hd: [vct_ref[hd]],
                addends_of=lambda hd: [[]])


def _na_bias_tables(rpb, rows):
    qr = np.arange(NA_QROWS)
    kri = np.arange(NA_KROWS)
    j = np.arange(GRID_W)
    kc = np.arange(GRID_W)
    half = NA_WIN_R // 2
    ridx = np.clip(kri[:, None] - half - qr[None, :] + NA_WIN_R - 1, 0, 2 * NA_WIN_R - 2)
    cidx = np.clip(kc[:, None] - j[None, :] + NA_WIN_C - 1, 0, 2 * NA_WIN_C - 2)
    n_k, n_q = NA_KROWS * GRID_W, NA_QROWS * GRID_W
    onehot = (cidx[None] == np.arange(2 * NA_WIN_C - 1)[:, None, None]).astype(np.float32)
    val = jnp.einsum("hkqd,dcj->hkcqj", rpb.astype(F32)[:, ridx] * LOG2E, jnp.asarray(onehot),
                     precision=lax.Precision.HIGHEST).reshape(rpb.shape[0], n_k, n_q)
    ws = np.clip(j - NA_WIN_C // 2, 0, GRID_W - NA_WIN_C)
    col_ok = (kc[:, None] >= ws[None, :]) & (kc[:, None] < ws[None, :] + NA_WIN_C)
    masks = []
    for r0 in (0, NA_QROWS, rows - NA_QROWS):
        r = r0 + qr
        rs = np.clip(r - half, 0, rows - NA_WIN_R)
        k_abs = r0 - half + kri
        row_ok = (k_abs[:, None] >= rs[None, :]) & (k_abs[:, None] < rs[None, :] + NA_WIN_R)
        ok = row_ok[:, None, :, None] & col_ok[None, :, None, :]
        masks.append(np.where(ok, 0.0, NEG).reshape(n_k, n_q))
    return val, jnp.asarray(np.stack(masks), F32)


def _na_attention(qk, vt, rpb, *, n_batch, seq, ctx_len, with_ctx_out):
    rows = seq // GRID_W
    qd = NA_HEADS * NA_HEAD_DIM
    tq = NA_QROWS * GRID_W
    n_keys = NA_KROWS * GRID_W
    n_groups = seq // tq
    ctx_tiles = ctx_len // tq
    n_ctx_tiles = ctx_tiles if with_ctx_out else 0
    n_lat = n_batch * seq
    n_rows = (n_lat + n_batch * ctx_len) if with_ctx_out else n_lat
    ctx_blk0 = n_lat // ctx_len
    vrows = NA_HEAD_DIM + ONES_ROWS
    val, mask = _na_bias_tables(rpb, rows)

    def q_idx(b, g):
        return (jnp.where(g < n_groups, b * n_groups + g, n_batch * n_groups + b * ctx_tiles + (g - n_groups)), 0)

    def mask_idx(b, g):
        return (jnp.where(g == 0, 0, jnp.where(g >= n_groups - 1, 2, 1)), 0, 0)

    return pl.pallas_call(
        functools.partial(_na_kernel, n_groups=n_groups, seq=seq),
        out_shape=jax.ShapeDtypeStruct((n_rows, qd), BF16),
        grid=(n_batch, n_groups + n_ctx_tiles),
        in_specs=[
            pl.BlockSpec((tq, qd), q_idx),
            pl.BlockSpec((seq, qd), lambda b, g: (b, 1)),
            pl.BlockSpec((ctx_len, qd), lambda b, g: (ctx_blk0 + b, 1)),
            pl.BlockSpec((NA_HEADS, vrows, seq), lambda b, g: (0, 0, b)),
            pl.BlockSpec((NA_HEADS, vrows, ctx_len), lambda b, g: (0, 0, ctx_blk0 + b)),
            _resident((NA_HEADS, n_keys, tq), lambda b, g: (0, 0, 0)),
            pl.BlockSpec((None, n_keys, tq), mask_idx),
        ],
        out_specs=pl.BlockSpec((tq, qd), q_idx),
        compiler_params=_params(2),
        name="na_attention",
    )(qk, qk, qk, vt, vt, val, mask)


def _sw_kernel(sink_ref, q_ref, kl_ref, kc_ref, vlt_ref, vct_ref, mask_ref, o_ref, *, n_tiles, seq):
    n = pl.program_id(1)
    dh = SW_HEAD_DIM
    grp = SW_HEADS // SW_KV_HEADS
    win = SW_KWIN
    ks = lambda hd: slice((hd // grp) * dh, (hd // grp + 1) * dh)
    floor_of = lambda hd: sink_ref[hd] * LOG2E

    @pl.when(n < n_tiles)
    def _():
        ws = pl.multiple_of(jnp.clip(n * SW_QTILE - SW_WINDOW, 0, seq - win), LANES)
        _attend(q_ref, o_ref, n_heads=SW_HEADS, dh=dh, lookahead=SW_LOOKAHEAD,
                keys_of=lambda hd: [kl_ref[pl.ds(ws, win), ks(hd)], kc_ref[:, ks(hd)]],
                values_of=lambda hd: [vlt_ref[hd // grp, :, pl.ds(ws, win)], vct_ref[hd // grp]],
                addends_of=lambda hd: [[mask_ref[...]], []], floor_of=floor_of)

    @pl.when(n >= n_tiles)
    def _():
        _attend(q_ref, o_ref, n_heads=SW_HEADS, dh=dh, lookahead=SW_LOOKAHEAD,
                keys_of=lambda hd: [kc_ref[:, ks(hd)]],
                values_of=lambda hd: [vct_ref[hd // grp]],
                addends_of=lambda hd: [[]], floor_of=floor_of)


def _sw_attention(proj, vt, sink, *, n_batch, seq, ctx_len, with_ctx_out):
    qd = SW_HEADS * SW_HEAD_DIM
    tq = SW_QTILE
    win = SW_KWIN
    n_tiles = seq // tq
    ctx_tiles = ctx_len // tq
    n_ctx_tiles = ctx_tiles if with_ctx_out else 0
    lat_tiles = n_batch * n_tiles
    n_lat = n_batch * seq
    n_rows = (n_lat + n_batch * ctx_len) if with_ctx_out else n_lat
    k_col = qd // LANES
    ctx_blk0 = n_lat // ctx_len
    vrows = SW_HEAD_DIM + ONES_ROWS
    ki = np.arange(win)[:, None]
    qj = np.arange(tq)[None, :]
    mask = np.stack([np.where(np.abs(ki - qj - off) <= SW_WINDOW, 0.0, NEG) for off in (0, SW_WINDOW, 2 * SW_WINDOW)])
    mask = jnp.asarray(mask, F32)

    def q_idx(b, n, s):
        return (jnp.where(n < n_tiles, b * n_tiles + n, lat_tiles + b * ctx_tiles + (n - n_tiles)), 0)

    def mask_idx(b, n, s):
        nn = jnp.minimum(n, n_tiles - 1)
        ws_blk = jnp.clip(nn * (tq // SW_WINDOW) - 1, 0, (seq - win) // SW_WINDOW)
        return (nn * (tq // SW_WINDOW) - ws_blk, 0, 0)

    grid_spec = pltpu.PrefetchScalarGridSpec(
        num_scalar_prefetch=1,
        grid=(n_batch, n_tiles + n_ctx_tiles),
        in_specs=[
            pl.BlockSpec((tq, qd), q_idx),
            pl.BlockSpec((seq, LANES), lambda b, n, s: (b, k_col)),
            pl.BlockSpec((ctx_len, LANES), lambda b, n, s: (ctx_blk0 + b, k_col)),
            pl.BlockSpec((SW_KV_HEADS, vrows, seq), lambda b, n, s: (0, 0, b)),
            pl.BlockSpec((SW_KV_HEADS, vrows, ctx_len), lambda b, n, s: (0, 0, ctx_blk0 + b)),
            pl.BlockSpec((None, win, tq), mask_idx),
        ],
        out_specs=pl.BlockSpec((tq, qd), q_idx),
    )
    return pl.pallas_call(
        functools.partial(_sw_kernel, n_tiles=n_tiles, seq=seq),
        out_shape=jax.ShapeDtypeStruct((n_rows, qd), BF16),
        grid_spec=grid_spec,
        compiler_params=_params(2),
        name="sw_attention",
    )(sink.astype(F32), proj, proj, proj, vt, vt, mask)


def _ga_kernel(q_ref, kl_ref, kc_ref, vlt_ref, vct_ref, o_ref):
    dh = GA_HEAD_DIM
    grp = GA_HEADS // GA_KV_HEADS
    ks = lambda hd: slice((hd // grp) * dh, (hd // grp + 1) * dh)
    _attend(q_ref, o_ref, n_heads=GA_HEADS, dh=dh, lookahead=GA_LOOKAHEAD,
            keys_of=lambda hd: [kl_ref[:, ks(hd)], kc_ref[:, ks(hd)]],
            values_of=lambda hd: [vlt_ref[hd // grp], vct_ref[hd // grp]],
            addends_of=lambda hd: [[], []])


def _ga_attention(proj, vt, *, n_batch, seq, ctx_len):
    qd = GA_HEADS * GA_HEAD_DIM
    kvd = GA_KV_HEADS * GA_HEAD_DIM
    tq = GA_QTILE
    n_tiles = seq // tq
    n_lat = n_batch * seq
    k_col = qd // kvd
    ctx_blk0 = n_lat // ctx_len
    vrows = GA_HEAD_DIM + ONES_ROWS
    return pl.pallas_call(
        _ga_kernel,
        out_shape=jax.ShapeDtypeStruct((n_lat, qd), BF16),
        grid=(n_batch, n_tiles),
        in_specs=[
            pl.BlockSpec((tq, qd), lambda b, n: (b * n_tiles + n, 0)),
            pl.BlockSpec((seq, kvd), lambda b, n: (b, k_col)),
            pl.BlockSpec((ctx_len, kvd), lambda b, n: (ctx_blk0 + b, k_col)),
            pl.BlockSpec((GA_KV_HEADS, vrows, seq), lambda b, n: (0, 0, b)),
            pl.BlockSpec((GA_KV_HEADS, vrows, ctx_len), lambda b, n: (0, 0, ctx_blk0 + b)),
        ],
        out_specs=pl.BlockSpec((tq, qd), lambda b, n: (b * n_tiles + n, 0)),
        compiler_params=_params(2),
        name="ga_attention",
    )(proj, proj, proj, vt, vt)


def _conv_kernel(um_ref, up_ref, un_ref, w_ref, b_ref, lg_ref, lb_ref, o_ref, win_ref, sh_ref, acc_ref, *,
                 lat_tiles, ctx_tiles):
    j = pl.program_id(1)
    jc = j - lat_tiles
    first = jnp.where(j < lat_tiles, j == 0, jc == 0)
    last = jnp.where(j < lat_tiles, j == lat_tiles - 1, jc == ctx_tiles - 1)
    tl = um_ref.shape[0]
    d = um_ref.shape[1]
    sh_rows = sh_ref.shape[1]
    win_ref[0:CONV_HALO, :] = jnp.where(first, 0.0, up_ref[...])
    win_ref[CONV_HALO:CONV_HALO + tl, :] = um_ref[...]
    win_ref[CONV_HALO + tl:, :] = jnp.where(last, 0.0, un_ref[...])
    base = CONV_HALO - CONV_WIDTH // 2

    def lane_slab(s, carry):
        ls = pl.ds(pl.multiple_of(s * LANES, LANES), LANES)
        for r in range(1, SUBLANES):
            sh_ref[r] = win_ref[r:r + sh_rows, ls]
        for rb in range(tl // CONV_ROWS):
            acc = jnp.broadcast_to(b_ref[:, ls], (CONV_ROWS, LANES))
            for k in range(CONV_WIDTH):
                a8, r = divmod(base + k, SUBLANES)
                lo = a8 * SUBLANES + rb * CONV_ROWS
                src = win_ref[lo:lo + CONV_ROWS, ls] if r == 0 else sh_ref[r, lo:lo + CONV_ROWS, :]
                acc = acc + w_ref[k:k + 1, ls] * src
            acc_ref[rb * CONV_ROWS:(rb + 1) * CONV_ROWS, ls] = acc
        return carry

    lax.fori_loop(0, d // LANES, lane_slab, 0)
    u = acc_ref[...]
    mu = jnp.mean(u, axis=-1, keepdims=True)
    var = jnp.mean(jnp.square(u - mu), axis=-1, keepdims=True)
    y = (u - mu) * lax.rsqrt(var + EPS) * lg_ref[...] + lb_ref[...]
    o_ref[...] = _silu(y).astype(BF16)


def _conv_core(u, w_dw, b_dw, ln_g, ln_b, *, n_batch, seq, ctx_len, with_ctx_out):
    d = u.shape[1]
    tl = CONV_TILE
    lat_tiles = seq // tl
    ctx_tiles = ctx_len // tl
    n_ctx = ctx_tiles if with_ctx_out else 0
    n_rows = (n_batch * seq + n_batch * ctx_len) if with_ctx_out else n_batch * seq
    halo_per_tile = tl // CONV_HALO
    n_halo_blocks = u.shape[0] // CONV_HALO
    sh_rows = tl + 2 * CONV_HALO - SUBLANES

    def main(b, j):
        return jnp.where(j < lat_tiles, b * lat_tiles + j, n_batch * lat_tiles + b * ctx_tiles + (j - lat_tiles))

    return pl.pallas_call(
        functools.partial(_conv_kernel, lat_tiles=lat_tiles, ctx_tiles=ctx_tiles),
        out_shape=jax.ShapeDtypeStruct((n_rows, d), BF16),
        grid=(n_batch, lat_tiles + n_ctx),
        in_specs=[
            pl.BlockSpec((tl, d), lambda b, j: (main(b, j), 0)),
            pl.BlockSpec((CONV_HALO, d), lambda b, j: (jnp.maximum(main(b, j) * halo_per_tile - 1, 0), 0)),
            pl.BlockSpec((CONV_HALO, d),
                         lambda b, j: (jnp.minimum((main(b, j) + 1) * halo_per_tile, n_halo_blocks - 1), 0)),
            pl.BlockSpec((CONV_WIDTH, d), lambda b, j: (0, 0)),
            pl.BlockSpec((1, d), lambda b, j: (0, 0)),
            pl.BlockSpec((1, d), lambda b, j: (0, 0)),
            pl.BlockSpec((1, d), lambda b, j: (0, 0)),
        ],
        out_specs=pl.BlockSpec((tl, d), lambda b, j: (main(b, j), 0)),
        scratch_shapes=[pltpu.VMEM((tl + 2 * CONV_HALO, d), F32), pltpu.VMEM((SUBLANES, sh_rows, LANES), F32),
                        pltpu.VMEM((tl, d), F32)],
        compiler_params=_params(2),
        name="conv_core",
    )(u, u, u, w_dw, b_dw.reshape(1, d), ln_g.reshape(1, d), ln_b.reshape(1, d))


def kernel(x, c, ctx, c_ctx, mod_w, mod_b, norm_g, ffn_w_gate, ffn_w_up, ffn_w_down, na_w_qkv, na_w_o, na_rpb, sw_w_qkv, sw_w_o, sw_sink, cv_w_pw1, cv_b_pw1, cv_w_dw, cv_b_dw, cv_ln_g, cv_ln_b, cv_w_pw2, cv_b_pw2, ga_w_qkv, ga_w_o, ga_q_norm, ga_k_norm):
    n_batch, seq, d = x.shape
    ctx_len = ctx.shape[1]
    depth = mod_w.shape[0]
    n_lat = n_batch * seq
    n_ctx = n_batch * ctx_len
    n_all = n_lat + n_ctx
    assert d == D_MODEL and n_batch < MOD_ROWS and seq % CONV_TILE == 0 and ctx_len % CONV_TILE == 0
    assert seq % ctx_len == 0 and seq >= SW_KWIN and seq % SW_QTILE == 0 and ctx_len % SW_QTILE == 0
    assert (seq // GRID_W) % NA_QROWS == 0 and (seq // GRID_W) >= NA_KROWS and seq % GA_QTILE == 0
    tm = _row_tile(n_lat, n_ctx, seq)
    common = dict(tm=tm, seq=seq, n_batch=n_batch)

    c_all = jnp.concatenate([c, c_ctx[None, :], jnp.zeros((MOD_ROWS - n_batch - 1, d), F32)], axis=0)
    mod = _modulation(c_all, mod_w, mod_b).reshape(depth, MOD_ROWS, N_MOD, d)

    wg = ffn_w_gate.astype(BF16)
    wu = ffn_w_up.astype(BF16)
    wd = ffn_w_down.astype(BF16)
    zero_bias = jnp.zeros((1, d), F32)

    xs = x.reshape(n_lat, d)
    for i in range(depth):
        kind, j = i % N_MIXERS, i // N_MIXERS
        last = i == depth - 1
        ctx_in = not (last and kind == 2)
        ctx_out = not last
        rows_in = n_all if ctx_in else n_lat
        rows_out = n_all if ctx_out else n_lat
        if kind == 0:
            qd = NA_HEADS * NA_HEAD_DIM
            proj = (_na_epilogue, na_w_qkv[j].astype(BF16), [], [],
                    *_proj_outputs(rows_in, tm, 2 * qd, BF16, NA_HEADS, NA_HEAD_DIM))
        elif kind == 1:
            qd, kvd = SW_HEADS * SW_HEAD_DIM, SW_KV_HEADS * SW_HEAD_DIM
            perm = np.concatenate([_half_split_perm(SW_HEADS, SW_HEAD_DIM),
                                   qd + _half_split_perm(SW_KV_HEADS, SW_HEAD_DIM),
                                   np.arange(qd + kvd, qd + 2 * kvd)])
            cos_t, sin_t = _rope_tables(seq, SW_HEAD_DIM, tm)
            proj = (_sw_epilogue, sw_w_qkv[j][:, perm].astype(BF16), [cos_t, sin_t], _rope_specs(tm, seq, n_lat),
                    *_proj_outputs(rows_in, tm, qd + kvd, BF16, SW_KV_HEADS, SW_HEAD_DIM))
        elif kind == 2:
            proj = (_cv_epilogue, cv_w_pw1[j].astype(BF16), [cv_b_pw1[j].reshape(1, -1)],
                    [pl.BlockSpec((1, 2 * d), lambda t: (0, 0))], *_proj_outputs(rows_in, tm, d, F32))
        else:
            assert not ctx_out, "context outputs of the global mixer are not implemented"
            qd, kvd = GA_HEADS * GA_HEAD_DIM, GA_KV_HEADS * GA_HEAD_DIM
            perm = np.concatenate([_half_split_perm(GA_HEADS, GA_HEAD_DIM),
                                   qd + _half_split_perm(GA_KV_HEADS, GA_HEAD_DIM),
                                   np.arange(qd + kvd, qd + 2 * kvd)])
            hperm = _half_split_perm(1, GA_HEAD_DIM)
            cos_t, sin_t = _rope_tables(seq, GA_HEAD_DIM, tm)
            vec = pl.BlockSpec((1, GA_HEAD_DIM), lambda t: (0, 0))
            proj = (_ga_epilogue, ga_w_qkv[j][:, perm].astype(BF16),
                    [cos_t, sin_t, ga_q_norm[j][hperm].reshape(1, -1), ga_k_norm[j][hperm].reshape(1, -1)],
                    _rope_specs(tm, seq, n_lat) + [vec, vec],
                    *_proj_outputs(rows_in, tm, qd + kvd, BF16, GA_KV_HEADS, GA_HEAD_DIM))
        x2 = ctx.reshape(n_ctx, d) if (i == 0 and ctx_in) else None
        xs, *pj = _ffn(xs, mod, norm_g, wg, wu, wd, layer=i, half=0, n_rows=rows_in, **common, x2=x2, proj=proj)
        if kind == 0:
            a = _na_attention(*pj, na_rpb[j], n_batch=n_batch, seq=seq, ctx_len=ctx_len, with_ctx_out=ctx_out)
            w_o, b_o = na_w_o[j], zero_bias
        elif kind == 1:
            a = _sw_attention(*pj, sw_sink[j], n_batch=n_batch, seq=seq, ctx_len=ctx_len, with_ctx_out=ctx_out)
            w_o, b_o = sw_w_o[j], zero_bias
        elif kind == 2:
            a = _conv_core(*pj, cv_w_dw[j], cv_b_dw[j], cv_ln_g[j], cv_ln_b[j], n_batch=n_batch, seq=seq,
                           ctx_len=ctx_len, with_ctx_out=ctx_out)
            w_o, b_o = cv_w_pw2[j], cv_b_pw2[j].reshape(1, d)
        else:
            a = _ga_attention(*pj, n_batch=n_batch, seq=seq, ctx_len=ctx_len)
            w_o, b_o = ga_w_o[j], zero_bias
        xs = _ffn(xs, mod, norm_g, wg, wu, wd, layer=i, half=1, n_rows=rows_out, **common,
                  mixer_out=(a, w_o.astype(BF16), b_o))
    return xs[:n_lat].reshape(n_batch, seq, d)
```

```python
import functools

import numpy as np
import jax
import jax.numpy as jnp
from jax import lax
from jax.experimental import pallas as pl
from jax.experimental.pallas import tpu as pltpu

F32 = jnp.float32
BF16 = jnp.bfloat16

D_MODEL = 1024
DEPTH = 4
GRID_W = 64
N_MIXERS = 4
EPS = 1e-6
NEG = -1e30
LOG2E = 1.4426950408889634
ROPE_THETA = 10000.0
D_FF = 2816
FFN_RESIDUAL_WEIGHT = 0.5
N_MOD = 9
NA_HEADS = 16
NA_HEAD_DIM = 64
NA_WIN_R = 8
NA_WIN_C = 16
NA_QROWS = 4
NA_KROWS = NA_QROWS + NA_WIN_R
SW_HEADS = 16
SW_KV_HEADS = 2
SW_HEAD_DIM = 64
SW_WINDOW = 128
SW_QTILE = 256
SW_KWIN = SW_QTILE + 2 * SW_WINDOW
ONES_ROWS = 16
NA_LOOKAHEAD, SW_LOOKAHEAD, GA_LOOKAHEAD = 2, 3, 2
CONV_WIDTH = 31
GA_HEADS = 8
GA_KV_HEADS = 2
GA_HEAD_DIM = 128
GA_QTILE = 256

LANES = 128
SUBLANES = 8
CONV_HALO = 16
CONV_TILE = 256
CONV_ROWS = 128
W_CHUNKS = 16
PROJ_SUB_BLOCKS = 2
MOD_ROWS = 16
VMEM_LIMIT = 56 * 1024 * 1024

_DN_T = (((1,), (1,)), ((), ()))


def _params(n_axes, vmem=VMEM_LIMIT):
    return pltpu.CompilerParams(dimension_semantics=("arbitrary",) * n_axes, vmem_limit_bytes=vmem)


def _resident(block_shape, index_map):
    return pl.BlockSpec(block_shape, index_map, pipeline_mode=pl.Buffered(1))


def _rms(x):
    return x * lax.rsqrt(jnp.mean(x * x, axis=-1, keepdims=True) + EPS)


def _silu(x):
    return x * jax.nn.sigmoid(x)


def _row_tile(n_lat_rows, n_ctx_rows, seq):
    for t in (512, 256, 128):
        if seq % t == 0 and n_ctx_rows % t == 0 and n_lat_rows % t == 0:
            return t
    raise ValueError("unsupported token counts")


def _mod_kernel(c_ref, w_ref, b_ref, o_ref):
    a = _silu(c_ref[...]).astype(BF16)
    o_ref[0] = jnp.dot(a, w_ref[0].astype(BF16), preferred_element_type=F32) + b_ref[0]


def _modulation(c_all, mod_w, mod_b):
    depth, d, n = mod_w.shape
    tn = 1152
    return pl.pallas_call(
        _mod_kernel,
        out_shape=jax.ShapeDtypeStruct((depth, MOD_ROWS, n), F32),
        grid=(depth, n // tn),
        in_specs=[
            pl.BlockSpec((MOD_ROWS, d), lambda i, j: (0, 0)),
            pl.BlockSpec((1, d, tn), lambda i, j: (i, 0, j)),
            pl.BlockSpec((1, 1, tn), lambda i, j: (i, 0, j)),
        ],
        out_specs=pl.BlockSpec((1, MOD_ROWS, tn), lambda i, j: (i, 0, j)),
        compiler_params=_params(2),
        name="modulation",
    )(c_all, mod_w, mod_b.reshape(depth, 1, n))


def _weight_copy(w_hbm, layer, half, chunk, stage_ref, sem_ref, slot):
    rows = stage_ref.shape[1]
    return pltpu.make_async_copy(w_hbm.at[layer, half, pl.ds(chunk * rows, rows), :], stage_ref.at[slot],
                                 sem_ref.at[slot])


def _load_weight_bf16(w_hbm, layer, half, dst_ref, stage_ref, sem_ref):
    rows = stage_ref.shape[1]
    n_chunks = dst_ref.shape[0] // rows
    _weight_copy(w_hbm, layer, half, 0, stage_ref, sem_ref, 0).start()
    for c in range(n_chunks):
        slot = c % 2
        if c + 1 < n_chunks:
            _weight_copy(w_hbm, layer, half, c + 1, stage_ref, sem_ref, 1 - slot).start()
        _weight_copy(w_hbm, layer, half, c, stage_ref, sem_ref, slot).wait()
        dst_ref[c * rows:(c + 1) * rows, :] = stage_ref[slot].astype(BF16)


def _ffn_kernel(*refs, layer, half, split, mix, epilogue, n_in, n_out, n_sub):
    ins, outs = iter(refs[:n_in]), refs[n_in:n_in + n_out]
    wg_ref, wu_ref, wd_ref, stage_a, stage_b, sem_a, sem_b = refs[n_in + n_out:]
    slot = 2 * half
    x_ref = next(ins)
    x2_ref = next(ins) if split is not None else None
    if mix:
        a_ref, wo_ref, bo_ref = next(ins), next(ins), next(ins)
    m_ref, g_ref, wg_hbm, wu_hbm, wd_hbm = (next(ins) for _ in range(5))
    if epilogue is not None:
        wq_ref = next(ins)
        extras = list(ins)

    @pl.when(pl.program_id(0) == 0)
    def _():
        _load_weight_bf16(wg_hbm, layer, half, wg_ref, stage_a, sem_a)
        _load_weight_bf16(wu_hbm, layer, half, wu_ref, stage_a, sem_a)
        _load_weight_bf16(wd_hbm, layer, half, wd_ref, stage_b, sem_b)

    shift = m_ref[3 * slot:3 * slot + 1, :]
    scale = m_ref[3 * slot + 1:3 * slot + 2, :]
    gate = m_ref[3 * slot + 2:3 * slot + 3, :]
    g_pre = g_ref[2 * slot:2 * slot + 1, :]
    g_post = g_ref[2 * slot + 1:2 * slot + 2, :]
    sub = x_ref.shape[0] // n_sub
    for r in range(n_sub):
        rows = slice(r * sub, (r + 1) * sub)
        x = x_ref[rows, :]
        if split is not None:
            x = jnp.where(pl.program_id(0) < split, x, x2_ref[rows, :])
        if mix:
            ym = jnp.dot(a_ref[rows, :], wo_ref[...], preferred_element_type=F32) + bo_ref[...]
            x = x + m_ref[5:6, :] * (_rms(ym) * g_ref[3:4, :])
        h = (_rms(x) * g_pre * (1.0 + scale) + shift).astype(BF16)
        gt = jnp.dot(h, wg_ref[...], preferred_element_type=F32)
        up = jnp.dot(h, wu_ref[...], preferred_element_type=F32)
        a = (_silu(gt) * up).astype(BF16)
        y = jnp.dot(a, wd_ref[...], preferred_element_type=F32)
        x = x + (FFN_RESIDUAL_WEIGHT * gate) * (_rms(y) * g_post)
        outs[0][rows, :] = x
        if epilogue is not None:
            hq = (_rms(x) * g_ref[2:3, :] * (1.0 + m_ref[4:5, :]) + m_ref[3:4, :]).astype(BF16)
            epilogue(jnp.dot(hq, wq_ref[...], preferred_element_type=F32), extras, outs[1:], rows)


def _group_map(tm, seq, n_batch):
    return lambda i: jnp.minimum((i * tm) // seq, n_batch)


def _ffn(x, mod, norm_g, wg, wu, wd, *, layer, half, n_rows, tm, seq, n_batch, x2=None, mixer_out=None, proj=None):
    d = x.shape[1]
    f = wg.shape[-1]
    slot = 2 * half
    grp = _group_map(tm, seq, n_batch)
    if x2 is None:
        split = None
        xs, x_specs = [x], [pl.BlockSpec((tm, d), lambda i: (i, 0))]
    else:
        split = x.shape[0] // tm
        xs = [x, x2]
        x_specs = [pl.BlockSpec((tm, d), lambda i: (jnp.minimum(i, split - 1), 0)),
                   pl.BlockSpec((tm, d), lambda i: (jnp.maximum(i - split, 0), 0))]
    if mixer_out is not None:
        a, w_o, b_o = mixer_out
        xs += [a, w_o, b_o]
        x_specs += [pl.BlockSpec((tm, a.shape[1]), lambda i: (i, 0)), _resident(w_o.shape, lambda i: (0, 0)),
                    pl.BlockSpec((1, d), lambda i: (0, 0))]
    xs += [mod, norm_g, wg, wu, wd]
    x_specs += [
        pl.BlockSpec((None, None, N_MOD, d), lambda i: (layer, grp(i), 0, 0)),
        pl.BlockSpec((None, 6, d), lambda i: (layer, 0, 0)),
        pl.BlockSpec(memory_space=pl.ANY),
        pl.BlockSpec(memory_space=pl.ANY),
        pl.BlockSpec(memory_space=pl.ANY),
    ]
    scratch = [pltpu.VMEM((d, f), BF16), pltpu.VMEM((d, f), BF16), pltpu.VMEM((f, d), BF16),
               pltpu.VMEM((2, d // W_CHUNKS, f), F32), pltpu.VMEM((2, f // W_CHUNKS, d), F32),
               pltpu.SemaphoreType.DMA((2,)), pltpu.SemaphoreType.DMA((2,))]
    out_shape = [jax.ShapeDtypeStruct((n_rows, d), F32)]
    out_specs = [pl.BlockSpec((tm, d), lambda i: (i, 0))]
    epilogue = None
    n_sub = 1
    if proj is not None:
        epilogue, w_qkv, extras, extra_specs, p_shapes, p_specs = proj
        n_sub = PROJ_SUB_BLOCKS if (tm >= 512 and epilogue is not _na_epilogue) else 1
        xs += [w_qkv] + extras
        x_specs += [_resident(w_qkv.shape, lambda i: (0, 0))] + extra_specs
        out_shape += p_shapes
        out_specs += p_specs
    res = pl.pallas_call(
        functools.partial(_ffn_kernel, layer=layer, half=half, split=split, mix=mixer_out is not None,
                          epilogue=epilogue, n_in=len(xs), n_out=len(out_shape), n_sub=n_sub),
        out_shape=out_shape,
        grid=(n_rows // tm,),
        in_specs=x_specs,
        out_specs=out_specs,
        scratch_shapes=scratch,
        compiler_params=_params(1),
        name=f"ffn_l{layer}h{half}",
    )(*xs)
    return res[0] if proj is None else res


def _rotate_half(x, half):
    if 2 * half == LANES:
        return pltpu.roll(x, half, axis=1)
    lane = lax.broadcasted_iota(jnp.int32, x.shape, 1)
    return jnp.where(lane % (2 * half) < half, pltpu.roll(x, LANES - half, axis=1), pltpu.roll(x, half, axis=1))


def _store_values_transposed(vt_ref, v, dh, rows):
    per = LANES // dh
    for j in range(v.shape[1] // LANES):
        t = v[:, j * LANES:(j + 1) * LANES].T
        for i in range(per):
            vt_ref[j * per + i, 0:dh, rows] = t[i * dh:(i + 1) * dh, :].astype(BF16)
    vt_ref[:, dh:, rows] = jnp.ones((vt_ref.shape[0], ONES_ROWS, v.shape[0]), BF16)


def _na_epilogue(p, extras, outs, rows):
    o_ref, vt_ref = outs
    qd = NA_HEADS * NA_HEAD_DIM
    o_ref[rows, :qd] = (p[:, :qd] * (NA_HEAD_DIM ** -0.5 * LOG2E)).astype(BF16)
    o_ref[rows, qd:] = p[:, qd:2 * qd].astype(BF16)
    _store_values_transposed(vt_ref, p[:, 2 * qd:], NA_HEAD_DIM, rows)


def _sw_epilogue(p, extras, outs, rows):
    cos_ref, sin_ref = extras
    o_ref, vt_ref = outs
    cos = cos_ref[rows, :]
    sin = sin_ref[rows, :]
    n_q = SW_HEADS * SW_HEAD_DIM // LANES
    n_k = SW_KV_HEADS * SW_HEAD_DIM // LANES
    for j in range(n_q + n_k):
        xs = p[:, j * LANES:(j + 1) * LANES]
        r = xs * cos + _rotate_half(xs, SW_HEAD_DIM // 2) * sin
        if j < n_q:
            r = r * (SW_HEAD_DIM ** -0.5 * LOG2E)
        o_ref[rows, j * LANES:(j + 1) * LANES] = r.astype(BF16)
    _store_values_transposed(vt_ref, p[:, (n_q + n_k) * LANES:], SW_HEAD_DIM, rows)


def _ga_epilogue(p, extras, outs, rows):
    cos_ref, sin_ref, qn_ref, kn_ref = extras
    o_ref, vt_ref = outs
    cos = cos_ref[rows, :]
    sin = sin_ref[rows, :]
    for j in range(GA_HEADS + GA_KV_HEADS):
        xs = _rms(p[:, j * LANES:(j + 1) * LANES])
        if j < GA_HEADS:
            xs = xs * (qn_ref[...] * (GA_HEAD_DIM ** -0.5 * LOG2E))
        else:
            xs = xs * kn_ref[...]
        r = xs * cos + _rotate_half(xs, GA_HEAD_DIM // 2) * sin
        o_ref[rows, j * LANES:(j + 1) * LANES] = r.astype(BF16)
    _store_values_transposed(vt_ref, p[:, (GA_HEADS + GA_KV_HEADS) * LANES:], GA_HEAD_DIM, rows)


def _cv_epilogue(p, extras, outs, rows):
    (b_ref,), (o_ref,) = extras, outs
    u = p + b_ref[...]
    d = o_ref.shape[1]
    o_ref[rows, :] = u[:, :d] * jax.nn.sigmoid(u[:, d:])


def _proj_outputs(n_rows, tm, n_out, out_dtype, vt_heads=0, vt_dim=0):
    shapes = [jax.ShapeDtypeStruct((n_rows, n_out), out_dtype)]
    specs = [pl.BlockSpec((tm, n_out), lambda i: (i, 0))]
    if vt_heads:
        vrows = vt_dim + ONES_ROWS
        shapes.append(jax.ShapeDtypeStruct((vt_heads, vrows, n_rows), BF16))
        specs.append(pl.BlockSpec((vt_heads, vrows, tm), lambda i: (0, 0, i)))
    return shapes, specs


def _rope_specs(tm, seq, n_lat_rows):
    n_seq_blocks = seq // tm
    idx = lambda i: (jnp.where(i * tm < n_lat_rows, i % n_seq_blocks, n_seq_blocks), 0)
    return [pl.BlockSpec((tm, LANES), idx), pl.BlockSpec((tm, LANES), idx)]


def _rope_tables(seq, head_dim, tm):
    t = np.arange(seq)
    row = (t // GRID_W).astype(np.float32)
    col = (t % GRID_W).astype(np.float32)
    n_pairs_axis = head_dim // 4
    freq = jnp.asarray(ROPE_THETA, F32) ** (-jnp.arange(n_pairs_axis, dtype=F32) / n_pairs_axis)
    ang = jnp.concatenate([row[:, None] * freq, col[:, None] * freq], axis=-1)
    cos, sin = jnp.cos(ang), jnp.sin(ang)
    reps = LANES // head_dim
    cos_t = jnp.tile(jnp.concatenate([cos, cos], axis=-1), (1, reps))
    sin_t = jnp.tile(jnp.concatenate([-sin, sin], axis=-1), (1, reps))
    cos_t = jnp.concatenate([cos_t, jnp.ones((tm, LANES), F32)], axis=0)
    sin_t = jnp.concatenate([sin_t, jnp.zeros((tm, LANES), F32)], axis=0)
    return cos_t, sin_t


def _half_split_perm(n_heads, head_dim):
    per_head = np.concatenate([np.arange(0, head_dim, 2), np.arange(1, head_dim, 2)])
    return (np.arange(n_heads)[:, None] * head_dim + per_head[None, :]).reshape(-1)


def _attend(q_ref, o_ref, *, n_heads, dh, lookahead, keys_of, values_of, addends_of, floor_of=None):
    per_store = LANES // dh

    def scores(hd):
        qh = q_ref[:, hd * dh:(hd + 1) * dh]
        ss = []
        for kb, adds in zip(keys_of(hd), addends_of(hd)):
            s = lax.dot_general(kb, qh, _DN_T, preferred_element_type=F32)
            for a in adds:
                s = s + a
            ss.append(s)
        return ss

    def finish(hd, ss):
        m = ss[0].max(axis=0, keepdims=True)
        for s in ss[1:]:
            m = jnp.maximum(m, s.max(axis=0, keepdims=True))
        floor = None if floor_of is None else floor_of(hd)
        if floor is not None:
            m = jnp.maximum(m, floor)
        ot = None
        for s, vb in zip(ss, values_of(hd)):
            part = jnp.dot(vb, jnp.exp2(s - m).astype(BF16), preferred_element_type=F32)
            ot = part if ot is None else ot + part
        l = ot[dh:dh + 1, :]
        if floor is not None:
            l = l + jnp.exp2(floor - m)
        return ot[:dh, :] / l

    pending = [scores(hd) for hd in range(min(lookahead, n_heads))]
    outs = []
    for hd in range(n_heads):
        ss = pending.pop(0)
        if hd + lookahead < n_heads:
            pending.append(scores(hd + lookahead))
        outs.append(finish(hd, ss))
        if len(outs) == per_store:
            o2 = outs[0] if per_store == 1 else jnp.concatenate(outs, axis=0)
            outs = []
            o_ref[:, (hd + 1 - per_store) * dh:(hd + 1) * dh] = o2.T.astype(BF16)


def _na_kernel(q_ref, kl_ref, kc_ref, vlt_ref, vct_ref, val_ref, mask_ref, o_ref, *, n_groups, seq):
    g = pl.program_id(1)
    dh = NA_HEAD_DIM
    tq = NA_QROWS * GRID_W
    n_keys = NA_KROWS * GRID_W
    n_edge = NA_WIN_R * GRID_W
    half = (NA_WIN_R // 2) * GRID_W
    hs = lambda hd: slice(hd * dh, (hd + 1) * dh)
    edge = (g == 0) | (g == n_groups - 1)

    @pl.when((g < n_groups) & jnp.logical_not(edge))
    def _():
        off = pl.multiple_of(g * tq - half, tq)
        _attend(q_ref, o_ref, n_heads=NA_HEADS, dh=dh, lookahead=NA_LOOKAHEAD,
                keys_of=lambda hd: [kl_ref[pl.ds(off, n_keys), hs(hd)], kc_ref[:, hs(hd)]],
                values_of=lambda hd: [vlt_ref[hd, :, pl.ds(off, n_keys)], vct_ref[hd]],
                addends_of=lambda hd: [[val_ref[hd], mask_ref[...]], []])

    @pl.when((g < n_groups) & edge)
    def _():
        koff = pl.multiple_of(jnp.where(g == 0, 0, seq - n_edge), tq)
        toff = pl.multiple_of(jnp.where(g == 0, half, 0), tq)
        _attend(q_ref, o_ref, n_heads=NA_HEADS, dh=dh, lookahead=NA_LOOKAHEAD,
                keys_of=lambda hd: [kl_ref[pl.ds(koff, n_edge), hs(hd)], kc_ref[:, hs(hd)]],
                values_of=lambda hd: [vlt_ref[hd, :, pl.ds(koff, n_edge)], vct_ref[hd]],
                addends_of=lambda hd: [[val_ref[hd, pl.ds(toff, n_edge), :], mask_ref[pl.ds(toff, n_edge), :]], []])

    @pl.when(g >= n_groups)
    def _():
        _attend(q_ref, o_ref, n_heads=NA_HEADS, dh=dh, lookahead=NA_LOOKAHEAD,
                keys_of=lambda hd: [kc_ref[:, hs(hd)]],
                values_of=lambda hd: [vct_ref[hd]],
                addends_of=lambda hd: [[]])


def _na_bias_tables(rpb, rows):
    qr = np.arange(NA_QROWS)
    kri = np.arange(NA_KROWS)
    j = np.arange(GRID_W)
    kc = np.arange(GRID_W)
    half = NA_WIN_R // 2
    ridx = np.clip(kri[:, None] - half - qr[None, :] + NA_WIN_R - 1, 0, 2 * NA_WIN_R - 2)
    cidx = np.clip(kc[:, None] - j[None, :] + NA_WIN_C - 1, 0, 2 * NA_WIN_C - 2)
    n_k, n_q = NA_KROWS * GRID_W, NA_QROWS * GRID_W
    onehot = (cidx[None] == np.arange(2 * NA_WIN_C - 1)[:, None, None]).astype(np.float32)
    val = jnp.einsum("hkqd,dcj->hkcqj", rpb.astype(F32)[:, ridx] * LOG2E, jnp.asarray(onehot),
                     precision=lax.Precision.HIGHEST).reshape(rpb.shape[0], n_k, n_q)
    ws = np.clip(j - NA_WIN_C // 2, 0, GRID_W - NA_WIN_C)
    col_ok = (kc[:, None] >= ws[None, :]) & (kc[:, None] < ws[None, :] + NA_WIN_C)
    masks = []
    for r0 in (0, NA_QROWS, rows - NA_QROWS):
        r = r0 + qr
        rs = np.clip(r - half, 0, rows - NA_WIN_R)
        k_abs = r0 - half + kri
        row_ok = (k_abs[:, None] >= rs[None, :]) & (k_abs[:, None] < rs[None, :] + NA_WIN_R)
        ok = row_ok[:, None, :, None] & col_ok[None, :, None, :]
        masks.append(np.where(ok, 0.0, NEG).reshape(n_k, n_q))
    return val, jnp.asarray(np.stack(masks), F32)


def _na_attention(qk, vt, rpb, *, n_batch, seq, ctx_len, with_ctx_out):
    rows = seq // GRID_W
    qd = NA_HEADS * NA_HEAD_DIM
    tq = NA_QROWS * GRID_W
    n_keys = NA_KROWS * GRID_W
    n_groups = seq // tq
    ctx_tiles = ctx_len // tq
    n_ctx_tiles = ctx_tiles if with_ctx_out else 0
    n_lat = n_batch * seq
    n_rows = (n_lat + n_batch * ctx_len) if with_ctx_out else n_lat
    ctx_blk0 = n_lat // ctx_len
    vrows = NA_HEAD_DIM + ONES_ROWS
    val, mask = _na_bias_tables(rpb, rows)

    def q_idx(b, g):
        return (jnp.where(g < n_groups, b * n_groups + g, n_batch * n_groups + b * ctx_tiles + (g - n_groups)), 0)

    def mask_idx(b, g):
        return (jnp.where(g == 0, 0, jnp.where(g >= n_groups - 1, 2, 1)), 0, 0)

    return pl.pallas_call(
        functools.partial(_na_kernel, n_groups=n_groups, seq=seq),
        out_shape=jax.ShapeDtypeStruct((n_rows, qd), BF16),
        grid=(n_batch, n_groups + n_ctx_tiles),
        in_specs=[
            pl.BlockSpec((tq, qd), q_idx),
            pl.BlockSpec((seq, qd), lambda b, g: (b, 1)),
            pl.BlockSpec((ctx_len, qd), lambda b, g: (ctx_blk0 + b, 1)),
            pl.BlockSpec((NA_HEADS, vrows, seq), lambda b, g: (0, 0, b)),
            pl.BlockSpec((NA_HEADS, vrows, ctx_len), lambda b, g: (0, 0, ctx_blk0 + b)),
            _resident((NA_HEADS, n_keys, tq), lambda b, g: (0, 0, 0)),
            pl.BlockSpec((None, n_keys, tq), mask_idx),
        ],
        out_specs=pl.BlockSpec((tq, qd), q_idx),
        compiler_params=_params(2),
        name="na_attention",
    )(qk, qk, qk, vt, vt, val, mask)


def _sw_kernel(sink_ref, q_ref, kl_ref, kc_ref, vlt_ref, vct_ref, mask_ref, o_ref, *, n_tiles, seq):
    n = pl.program_id(1)
    dh = SW_HEAD_DIM
    grp = SW_HEADS // SW_KV_HEADS
    win = SW_KWIN
    ks = lambda hd: slice((hd // grp) * dh, (hd // grp + 1) * dh)
    floor_of = lambda hd: sink_ref[hd] * LOG2E

    @pl.when(n < n_tiles)
    def _():
        ws = pl.multiple_of(jnp.clip(n * SW_QTILE - SW_WINDOW, 0, seq - win), LANES)
        _attend(q_ref, o_ref, n_heads=SW_HEADS, dh=dh, lookahead=SW_LOOKAHEAD,
                keys_of=lambda hd: [kl_ref[pl.ds(ws, win), ks(hd)], kc_ref[:, ks(hd)]],
                values_of=lambda hd: [vlt_ref[hd // grp, :, pl.ds(ws, win)], vct_ref[hd // grp]],
                addends_of=lambda hd: [[mask_ref[...]], []], floor_of=floor_of)

    @pl.when(n >= n_tiles)
    def _():
        _attend(q_ref, o_ref, n_heads=SW_HEADS, dh=dh, lookahead=SW_LOOKAHEAD,
                keys_of=lambda hd: [kc_ref[:, ks(hd)]],
                values_of=lambda hd: [vct_ref[hd // grp]],
                addends_of=lambda hd: [[]], floor_of=floor_of)


def _sw_attention(proj, vt, sink, *, n_batch, seq, ctx_len, with_ctx_out):
    qd = SW_HEADS * SW_HEAD_DIM
    tq = SW_QTILE
    win = SW_KWIN
    n_tiles = seq // tq
    ctx_tiles = ctx_len // tq
    n_ctx_tiles = ctx_tiles if with_ctx_out else 0
    lat_tiles = n_batch * n_tiles
    n_lat = n_batch * seq
    n_rows = (n_lat + n_batch * ctx_len) if with_ctx_out else n_lat
    k_col = qd // LANES
    ctx_blk0 = n_lat // ctx_len
    vrows = SW_HEAD_DIM + ONES_ROWS
    ki = np.arange(win)[:, None]
    qj = np.arange(tq)[None, :]
    mask = np.stack([np.where(np.abs(ki - qj - off) <= SW_WINDOW, 0.0, NEG) for off in (0, SW_WINDOW, 2 * SW_WINDOW)])
    mask = jnp.asarray(mask, F32)

    def q_idx(b, n, s):
        return (jnp.where(n < n_tiles, b * n_tiles + n, lat_tiles + b * ctx_tiles + (n - n_tiles)), 0)

    def mask_idx(b, n, s):
        nn = jnp.minimum(n, n_tiles - 1)
        ws_blk = jnp.clip(nn * (tq // SW_WINDOW) - 1, 0, (seq - win) // SW_WINDOW)
        return (nn * (tq // SW_WINDOW) - ws_blk, 0, 0)

    grid_spec = pltpu.PrefetchScalarGridSpec(
        num_scalar_prefetch=1,
        grid=(n_batch, n_tiles + n_ctx_tiles),
        in_specs=[
            pl.BlockSpec((tq, qd), q_idx),
            pl.BlockSpec((seq, LANES), lambda b, n, s: (b, k_col)),
            pl.BlockSpec((ctx_len, LANES), lambda b, n, s: (ctx_blk0 + b, k_col)),
            pl.BlockSpec((SW_KV_HEADS, vrows, seq), lambda b, n, s: (0, 0, b)),
            pl.BlockSpec((SW_KV_HEADS, vrows, ctx_len), lambda b, n, s: (0, 0, ctx_blk0 + b)),
            pl.BlockSpec((None, win, tq), mask_idx),
        ],
        out_specs=pl.BlockSpec((tq, qd), q_idx),
    )
    return pl.pallas_call(
        functools.partial(_sw_kernel, n_tiles=n_tiles, seq=seq),
        out_shape=jax.ShapeDtypeStruct((n_rows, qd), BF16),
        grid_spec=grid_spec,
        compiler_params=_params(2),
        name="sw_attention",
    )(sink.astype(F32), proj, proj, proj, vt, vt, mask)


def _ga_kernel(q_ref, kl_ref, kc_ref, vlt_ref, vct_ref, o_ref):
    dh = GA_HEAD_DIM
    grp = GA_HEADS // GA_KV_HEADS
    ks = lambda hd: slice((hd // grp) * dh, (hd // grp + 1) * dh)
    _attend(q_ref, o_ref, n_heads=GA_HEADS, dh=dh, lookahead=GA_LOOKAHEAD,
            keys_of=lambda hd: [kl_ref[:, ks(hd)], kc_ref[:, ks(hd)]],
            values_of=lambda hd: [vlt_ref[hd // grp], vct_ref[hd // grp]],
            addends_of=lambda hd: [[], []])


def _ga_attention(proj, vt, *, n_batch, seq, ctx_len):
    qd = GA_HEADS * GA_HEAD_DIM
    kvd = GA_KV_HEADS * GA_HEAD_DIM
    tq = GA_QTILE
    n_tiles = seq // tq
    n_lat = n_batch * seq
    k_col = qd // kvd
    ctx_blk0 = n_lat // ctx_len
    vrows = GA_HEAD_DIM + ONES_ROWS
    return pl.pallas_call(
        _ga_kernel,
        out_shape=jax.ShapeDtypeStruct((n_lat, qd), BF16),
        grid=(n_batch, n_tiles),
        in_specs=[
            pl.BlockSpec((tq, qd), lambda b, n: (b * n_tiles + n, 0)),
            pl.BlockSpec((seq, kvd), lambda b, n: (b, k_col)),
            pl.BlockSpec((ctx_len, kvd), lambda b, n: (ctx_blk0 + b, k_col)),
            pl.BlockSpec((GA_KV_HEADS, vrows, seq), lambda b, n: (0, 0, b)),
            pl.BlockSpec((GA_KV_HEADS, vrows, ctx_len), lambda b, n: (0, 0, ctx_blk0 + b)),
        ],
        out_specs=pl.BlockSpec((tq, qd), lambda b, n: (b * n_tiles + n, 0)),
        compiler_params=_params(2),
        name="ga_attention",
    )(proj, proj, proj, vt, vt)


def _conv_kernel(um_ref, up_ref, un_ref, w_ref, b_ref, lg_ref, lb_ref, o_ref, win_ref, sh_ref, acc_ref, *,
                 lat_tiles, ctx_tiles):
    j = pl.program_id(1)
    jc = j - lat_tiles
    first = jnp.where(j < lat_tiles, j == 0, jc == 0)
    last = jnp.where(j < lat_tiles, j == lat_tiles - 1, jc == ctx_tiles - 1)
    tl = um_ref.shape[0]
    d = um_ref.shape[1]
    sh_rows = sh_ref.shape[1]
    win_ref[0:CONV_HALO, :] = jnp.where(first, 0.0, up_ref[...])
    win_ref[CONV_HALO:CONV_HALO + tl, :] = um_ref[...]
    win_ref[CONV_HALO + tl:, :] = jnp.where(last, 0.0, un_ref[...])
    base = CONV_HALO - CONV_WIDTH // 2

    def lane_slab(s, carry):
        ls = pl.ds(pl.multiple_of(s * LANES, LANES), LANES)
        for r in range(1, SUBLANES):
            sh_ref[r] = win_ref[r:r + sh_rows, ls]
        for rb in range(tl // CONV_ROWS):
            acc = jnp.broadcast_to(b_ref[:, ls], (CONV_ROWS, LANES))
            for k in range(CONV_WIDTH):
                a8, r = divmod(base + k, SUBLANES)
                lo = a8 * SUBLANES + rb * CONV_ROWS
                src = win_ref[lo:lo + CONV_ROWS, ls] if r == 0 else sh_ref[r, lo:lo + CONV_ROWS, :]
                acc = acc + w_ref[k:k + 1, ls] * src
            acc_ref[rb * CONV_ROWS:(rb + 1) * CONV_ROWS, ls] = acc
        return carry

    lax.fori_loop(0, d // LANES, lane_slab, 0)
    u = acc_ref[...]
    mu = jnp.mean(u, axis=-1, keepdims=True)
    var = jnp.mean(jnp.square(u - mu), axis=-1, keepdims=True)
    y = (u - mu) * lax.rsqrt(var + EPS) * lg_ref[...] + lb_ref[...]
    o_ref[...] = _silu(y).astype(BF16)


def _conv_core(u, w_dw, b_dw, ln_g, ln_b, *, n_batch, seq, ctx_len, with_ctx_out):
    d = u.shape[1]
    tl = CONV_TILE
    lat_tiles = seq // tl
    ctx_tiles = ctx_len // tl
    n_ctx = ctx_tiles if with_ctx_out else 0
    n_rows = (n_batch * seq + n_batch * ctx_len) if with_ctx_out else n_batch * seq
    halo_per_tile = tl // CONV_HALO
    n_halo_blocks = u.shape[0] // CONV_HALO
    sh_rows = tl + 2 * CONV_HALO - SUBLANES

    def main(b, j):
        return jnp.where(j < lat_tiles, b * lat_tiles + j, n_batch * lat_tiles + b * ctx_tiles + (j - lat_tiles))

    return pl.pallas_call(
        functools.partial(_conv_kernel, lat_tiles=lat_tiles, ctx_tiles=ctx_tiles),
        out_shape=jax.ShapeDtypeStruct((n_rows, d), BF16),
        grid=(n_batch, lat_tiles + n_ctx),
        in_specs=[
            pl.BlockSpec((tl, d), lambda b, j: (main(b, j), 0)),
            pl.BlockSpec((CONV_HALO, d), lambda b, j: (jnp.maximum(main(b, j) * halo_per_tile - 1, 0), 0)),
            pl.BlockSpec((CONV_HALO, d),
                         lambda b, j: (jnp.minimum((main(b, j) + 1) * halo_per_tile, n_halo_blocks - 1), 0)),
            pl.BlockSpec((CONV_WIDTH, d), lambda b, j: (0, 0)),
            pl.BlockSpec((1, d), lambda b, j: (0, 0)),
            pl.BlockSpec((1, d), lambda b, j: (0, 0)),
            pl.BlockSpec((1, d), lambda b, j: (0, 0)),
        ],
        out_specs=pl.BlockSpec((tl, d), lambda b, j: (main(b, j), 0)),
        scratch_shapes=[pltpu.VMEM((tl + 2 * CONV_HALO, d), F32), pltpu.VMEM((SUBLANES, sh_rows, LANES), F32),
                        pltpu.VMEM((tl, d), F32)],
        compiler_params=_params(2),
        name="conv_core",
    )(u, u, u, w_dw, b_dw.reshape(1, d), ln_g.reshape(1, d), ln_b.reshape(1, d))


def kernel(x, c, ctx, c_ctx, mod_w, mod_b, norm_g, ffn_w_gate, ffn_w_up, ffn_w_down, na_w_qkv, na_w_o, na_rpb, sw_w_qkv, sw_w_o, sw_sink, cv_w_pw1, cv_b_pw1, cv_w_dw, cv_b_dw, cv_ln_g, cv_ln_b, cv_w_pw2, cv_b_pw2, ga_w_qkv, ga_w_o, ga_q_norm, ga_k_norm):
    n_batch, seq, d = x.shape
    ctx_len = ctx.shape[1]
    depth = mod_w.shape[0]
    n_lat = n_batch * seq
    n_ctx = n_batch * ctx_len
    n_all = n_lat + n_ctx
    assert d == D_MODEL and n_batch < MOD_ROWS and seq % CONV_TILE == 0 and ctx_len % CONV_TILE == 0
    assert seq % ctx_len == 0 and seq >= SW_KWIN and seq % SW_QTILE == 0 and ctx_len % SW_QTILE == 0
    assert (seq // GRID_W) % NA_QROWS == 0 and (seq // GRID_W) >= NA_KROWS and seq % GA_QTILE == 0
    tm = _row_tile(n_lat, n_ctx, seq)
    common = dict(tm=tm, seq=seq, n_batch=n_batch)

    c_all = jnp.concatenate([c, c_ctx[None, :], jnp.zeros((MOD_ROWS - n_batch - 1, d), F32)], axis=0)
    mod = _modulation(c_all, mod_w, mod_b).reshape(depth, MOD_ROWS, N_MOD, d)

    wg, wu, wd = ffn_w_gate, ffn_w_up, ffn_w_down
    zero_bias = jnp.zeros((1, d), F32)

    xs = x.reshape(n_lat, d)
    for i in range(depth):
        kind, j = i % N_MIXERS, i // N_MIXERS
        last = i == depth - 1
        ctx_in = not (last and kind == 2)
        ctx_out = not last
        rows_in = n_all if ctx_in else n_lat
        rows_out = n_all if ctx_out else n_lat
        if kind == 0:
            qd = NA_HEADS * NA_HEAD_DIM
            proj = (_na_epilogue, na_w_qkv[j].astype(BF16), [], [],
                    *_proj_outputs(rows_in, tm, 2 * qd, BF16, NA_HEADS, NA_HEAD_DIM))
        elif kind == 1:
            qd, kvd = SW_HEADS * SW_HEAD_DIM, SW_KV_HEADS * SW_HEAD_DIM
            perm = np.concatenate([_half_split_perm(SW_HEADS, SW_HEAD_DIM),
                                   qd + _half_split_perm(SW_KV_HEADS, SW_HEAD_DIM),
                                   np.arange(qd + kvd, qd + 2 * kvd)])
            cos_t, sin_t = _rope_tables(seq, SW_HEAD_DIM, tm)
            proj = (_sw_epilogue, sw_w_qkv[j][:, perm].astype(BF16), [cos_t, sin_t], _rope_specs(tm, seq, n_lat),
                    *_proj_outputs(rows_in, tm, qd + kvd, BF16, SW_KV_HEADS, SW_HEAD_DIM))
        elif kind == 2:
            proj = (_cv_epilogue, cv_w_pw1[j].astype(BF16), [cv_b_pw1[j].reshape(1, -1)],
                    [pl.BlockSpec((1, 2 * d), lambda t: (0, 0))], *_proj_outputs(rows_in, tm, d, F32))
        else:
            assert not ctx_out, "context outputs of the global mixer are not implemented"
            qd, kvd = GA_HEADS * GA_HEAD_DIM, GA_KV_HEADS * GA_HEAD_DIM
            perm = np.concatenate([_half_split_perm(GA_HEADS, GA_HEAD_DIM),
                                   qd + _half_split_perm(GA_KV_HEADS, GA_HEAD_DIM),
                                   np.arange(qd + kvd, qd + 2 * kvd)])
            hperm = _half_split_perm(1, GA_HEAD_DIM)
            cos_t, sin_t = _rope_tables(seq, GA_HEAD_DIM, tm)
            vec = pl.BlockSpec((1, GA_HEAD_DIM), lambda t: (0, 0))
            proj = (_ga_epilogue, ga_w_qkv[j][:, perm].astype(BF16),
                    [cos_t, sin_t, ga_q_norm[j][hperm].reshape(1, -1), ga_k_norm[j][hperm].reshape(1, -1)],
                    _rope_specs(tm, seq, n_lat) + [vec, vec],
                    *_proj_outputs(rows_in, tm, qd + kvd, BF16, GA_KV_HEADS, GA_HEAD_DIM))
        x2 = ctx.reshape(n_ctx, d) if (i == 0 and ctx_in) else None
        xs, *pj = _ffn(xs, mod, norm_g, wg, wu, wd, layer=i, half=0, n_rows=rows_in, **common, x2=x2, proj=proj)
        if kind == 0:
            a = _na_attention(*pj, na_rpb[j], n_batch=n_batch, seq=seq, ctx_len=ctx_len, with_ctx_out=ctx_out)
            w_o, b_o = na_w_o[j], zero_bias
        elif kind == 1:
            a = _sw_attention(*pj, sw_sink[j], n_batch=n_batch, seq=seq, ctx_len=ctx_len, with_ctx_out=ctx_out)
            w_o, b_o = sw_w_o[j], zero_bias
        elif kind == 2:
            a = _conv_core(*pj, cv_w_dw[j], cv_b_dw[j], cv_ln_g[j], cv_ln_b[j], n_batch=n_batch, seq=seq,
                           ctx_len=ctx_len, with_ctx_out=ctx_out)
            w_o, b_o = cv_w_pw2[j], cv_b_pw2[j].reshape(1, d)
        else:
            a = _ga_attention(*pj, n_batch=n_batch, seq=seq, ctx_len=ctx_len)
            w_o, b_o = ga_w_o[j], zero_bias
        xs = _ffn(xs, mod, norm_g, wg, wu, wd, layer=i, half=1, n_rows=rows_out, **common,
                  mixer_out=(a, w_o.astype(BF16), b_o))
    return xs[:n_lat].reshape(n_batch, seq, d)
```

```python
import functools

import numpy as np
import jax
import jax.numpy as jnp
from jax import lax
from jax.experimental import pallas as pl
from jax.experimental.pallas import tpu as pltpu

F32 = jnp.float32
BF16 = jnp.bfloat16

D_MODEL = 1024
DEPTH = 4
GRID_W = 64
N_MIXERS = 4
EPS = 1e-6
NEG = -1e30
LOG2E = 1.4426950408889634
ROPE_THETA = 10000.0
D_FF = 2816
FFN_RESIDUAL_WEIGHT = 0.5
N_MOD = 9
NA_HEADS = 16
NA_HEAD_DIM = 64
NA_WIN_R = 8
NA_WIN_C = 16
NA_QROWS = 4
NA_KROWS = NA_QROWS + NA_WIN_R
SW_HEADS = 16
SW_KV_HEADS = 2
SW_HEAD_DIM = 64
SW_WINDOW = 128
SW_QTILE = 256
SW_KWIN = SW_QTILE + 2 * SW_WINDOW
ONES_ROWS = 16
NA_LOOKAHEAD, SW_LOOKAHEAD, GA_LOOKAHEAD = 2, 3, 2
CONV_WIDTH = 31
GA_HEADS = 8
GA_KV_HEADS = 2
GA_HEAD_DIM = 128
GA_QTILE = 256

LANES = 128
SUBLANES = 8
CONV_HALO = 16
CONV_TILE = 256
CONV_ROWS = 128
W_CHUNKS = 16
W_SLOTS = 4
PROJ_SUB_BLOCKS = 2
MOD_ROWS = 16
VMEM_LIMIT = 60 * 1024 * 1024

_DN_T = (((1,), (1,)), ((), ()))


def _params(n_axes, vmem=VMEM_LIMIT):
    return pltpu.CompilerParams(dimension_semantics=("arbitrary",) * n_axes, vmem_limit_bytes=vmem)


def _resident(block_shape, index_map):
    return pl.BlockSpec(block_shape, index_map, pipeline_mode=pl.Buffered(1))


def _rms(x):
    return x * lax.rsqrt(jnp.mean(x * x, axis=-1, keepdims=True) + EPS)


def _silu(x):
    return x * jax.nn.sigmoid(x)


def _row_tile(n_lat_rows, n_ctx_rows, seq):
    for t in (512, 256, 128):
        if seq % t == 0 and n_ctx_rows % t == 0 and n_lat_rows % t == 0:
            return t
    raise ValueError("unsupported token counts")


def _mod_kernel(c_ref, w_ref, b_ref, o_ref):
    a = _silu(c_ref[...]).astype(BF16)
    o_ref[0] = jnp.dot(a, w_ref[0].astype(BF16), preferred_element_type=F32) + b_ref[0]


def _modulation(c_all, mod_w, mod_b):
    depth, d, n = mod_w.shape
    tn = 1152
    return pl.pallas_call(
        _mod_kernel,
        out_shape=jax.ShapeDtypeStruct((depth, MOD_ROWS, n), F32),
        grid=(depth, n // tn),
        in_specs=[
            pl.BlockSpec((MOD_ROWS, d), lambda i, j: (0, 0)),
            pl.BlockSpec((1, d, tn), lambda i, j: (i, 0, j)),
            pl.BlockSpec((1, 1, tn), lambda i, j: (i, 0, j)),
        ],
        out_specs=pl.BlockSpec((1, MOD_ROWS, tn), lambda i, j: (i, 0, j)),
        compiler_params=_params(2),
        name="modulation",
    )(c_all, mod_w, mod_b.reshape(depth, 1, n))


def _weight_copy(w_hbm, layer, half, chunk, stage_ref, sem_ref, slot):
    rows = stage_ref.shape[1]
    return pltpu.make_async_copy(w_hbm.at[layer, half, pl.ds(chunk * rows, rows), :], stage_ref.at[slot],
                                 sem_ref.at[slot])


def _load_weight_bf16(w_hbm, layer, half, dst_ref, stage_ref, sem_ref):
    n_slots, rows = stage_ref.shape[0], stage_ref.shape[1]
    n_chunks = dst_ref.shape[0] // rows
    for c in range(min(n_slots - 1, n_chunks)):
        _weight_copy(w_hbm, layer, half, c, stage_ref, sem_ref, c % n_slots).start()
    for c in range(n_chunks):
        nxt = c + n_slots - 1
        if nxt < n_chunks:
            _weight_copy(w_hbm, layer, half, nxt, stage_ref, sem_ref, nxt % n_slots).start()
        _weight_copy(w_hbm, layer, half, c, stage_ref, sem_ref, c % n_slots).wait()
        dst_ref[c * rows:(c + 1) * rows, :] = stage_ref[c % n_slots].astype(BF16)


def _ffn_kernel(*refs, layer, half, split, mix, epilogue, n_in, n_out, n_sub):
    ins, outs = iter(refs[:n_in]), refs[n_in:n_in + n_out]
    wg_ref, wu_ref, wd_ref, stage_a, stage_b, sem_a, sem_b = refs[n_in + n_out:]
    slot = 2 * half
    x_ref = next(ins)
    x2_ref = next(ins) if split is not None else None
    if mix:
        a_ref, wo_ref, bo_ref = next(ins), next(ins), next(ins)
    m_ref, g_ref, wg_hbm, wu_hbm, wd_hbm = (next(ins) for _ in range(5))
    if epilogue is not None:
        wq_ref = next(ins)
        extras = list(ins)

    @pl.when(pl.program_id(0) == 0)
    def _():
        _load_weight_bf16(wg_hbm, layer, half, wg_ref, stage_a, sem_a)
        _load_weight_bf16(wu_hbm, layer, half, wu_ref, stage_a, sem_a)
        _load_weight_bf16(wd_hbm, layer, half, wd_ref, stage_b, sem_b)

    shift = m_ref[3 * slot:3 * slot + 1, :]
    scale = m_ref[3 * slot + 1:3 * slot + 2, :]
    gate = m_ref[3 * slot + 2:3 * slot + 3, :]
    g_pre = g_ref[2 * slot:2 * slot + 1, :]
    g_post = g_ref[2 * slot + 1:2 * slot + 2, :]
    sub = x_ref.shape[0] // n_sub
    for r in range(n_sub):
        rows = slice(r * sub, (r + 1) * sub)
        x = x_ref[rows, :]
        if split is not None:
            x = jnp.where(pl.program_id(0) < split, x, x2_ref[rows, :])
        if mix:
            ym = jnp.dot(a_ref[rows, :], wo_ref[...], preferred_element_type=F32) + bo_ref[...]
            x = x + m_ref[5:6, :] * (_rms(ym) * g_ref[3:4, :])
        h = (_rms(x) * g_pre * (1.0 + scale) + shift).astype(BF16)
        gt = jnp.dot(h, wg_ref[...], preferred_element_type=F32)
        up = jnp.dot(h, wu_ref[...], preferred_element_type=F32)
        a = (_silu(gt) * up).astype(BF16)
        y = jnp.dot(a, wd_ref[...], preferred_element_type=F32)
        x = x + (FFN_RESIDUAL_WEIGHT * gate) * (_rms(y) * g_post)
        outs[0][rows, :] = x
        if epilogue is not None:
            hq = (_rms(x) * g_ref[2:3, :] * (1.0 + m_ref[4:5, :]) + m_ref[3:4, :]).astype(BF16)
            epilogue(jnp.dot(hq, wq_ref[...], preferred_element_type=F32), extras, outs[1:], rows)


def _group_map(tm, seq, n_batch):
    return lambda i: jnp.minimum((i * tm) // seq, n_batch)


def _ffn(x, mod, norm_g, wg, wu, wd, *, layer, half, n_rows, tm, seq, n_batch, x2=None, mixer_out=None, proj=None):
    d = x.shape[1]
    f = wg.shape[-1]
    slot = 2 * half
    grp = _group_map(tm, seq, n_batch)
    if x2 is None:
        split = None
        xs, x_specs = [x], [pl.BlockSpec((tm, d), lambda i: (i, 0))]
    else:
        split = x.shape[0] // tm
        xs = [x, x2]
        x_specs = [pl.BlockSpec((tm, d), lambda i: (jnp.minimum(i, split - 1), 0)),
                   pl.BlockSpec((tm, d), lambda i: (jnp.maximum(i - split, 0), 0))]
    if mixer_out is not None:
        a, w_o, b_o = mixer_out
        xs += [a, w_o, b_o]
        x_specs += [pl.BlockSpec((tm, a.shape[1]), lambda i: (i, 0)), _resident(w_o.shape, lambda i: (0, 0)),
                    pl.BlockSpec((1, d), lambda i: (0, 0))]
    xs += [mod, norm_g, wg, wu, wd]
    x_specs += [
        pl.BlockSpec((None, None, N_MOD, d), lambda i: (layer, grp(i), 0, 0)),
        pl.BlockSpec((None, 6, d), lambda i: (layer, 0, 0)),
        pl.BlockSpec(memory_space=pl.ANY),
        pl.BlockSpec(memory_space=pl.ANY),
        pl.BlockSpec(memory_space=pl.ANY),
    ]
    scratch = [pltpu.VMEM((d, f), BF16), pltpu.VMEM((d, f), BF16), pltpu.VMEM((f, d), BF16),
               pltpu.VMEM((W_SLOTS, d // W_CHUNKS, f), F32), pltpu.VMEM((W_SLOTS, f // W_CHUNKS, d), F32),
               pltpu.SemaphoreType.DMA((W_SLOTS,)), pltpu.SemaphoreType.DMA((W_SLOTS,))]
    out_shape = [jax.ShapeDtypeStruct((n_rows, d), F32)]
    out_specs = [pl.BlockSpec((tm, d), lambda i: (i, 0))]
    epilogue = None
    n_sub = 1
    if proj is not None:
        epilogue, w_qkv, extras, extra_specs, p_shapes, p_specs = proj
        n_sub = PROJ_SUB_BLOCKS if (tm >= 512 and epilogue is not _na_epilogue) else 1
        xs += [w_qkv] + extras
        x_specs += [_resident(w_qkv.shape, lambda i: (0, 0))] + extra_specs
        out_shape += p_shapes
        out_specs += p_specs
    res = pl.pallas_call(
        functools.partial(_ffn_kernel, layer=layer, half=half, split=split, mix=mixer_out is not None,
                          epilogue=epilogue, n_in=len(xs), n_out=len(out_shape), n_sub=n_sub),
        out_shape=out_shape,
        grid=(n_rows // tm,),
        in_specs=x_specs,
        out_specs=out_specs,
        scratch_shapes=scratch,
        compiler_params=_params(1),
        name=f"ffn_l{layer}h{half}",
    )(*xs)
    return res[0] if proj is None else res


def _rotate_half(x, half):
    if 2 * half == LANES:
        return pltpu.roll(x, half, axis=1)
    lane = lax.broadcasted_iota(jnp.int32, x.shape, 1)
    return jnp.where(lane % (2 * half) < half, pltpu.roll(x, LANES - half, axis=1), pltpu.roll(x, half, axis=1))


def _store_values_transposed(vt_ref, v, dh, rows):
    per = LANES // dh
    for j in range(v.shape[1] // LANES):
        t = v[:, j * LANES:(j + 1) * LANES].T
        for i in range(per):
            vt_ref[j * per + i, 0:dh, rows] = t[i * dh:(i + 1) * dh, :].astype(BF16)
    vt_ref[:, dh:, rows] = jnp.ones((vt_ref.shape[0], ONES_ROWS, v.shape[0]), BF16)


def _na_epilogue(p, extras, outs, rows):
    o_ref, vt_ref = outs
    qd = NA_HEADS * NA_HEAD_DIM
    o_ref[rows, :qd] = (p[:, :qd] * (NA_HEAD_DIM ** -0.5 * LOG2E)).astype(BF16)
    o_ref[rows, qd:] = p[:, qd:2 * qd].astype(BF16)
    _store_values_transposed(vt_ref, p[:, 2 * qd:], NA_HEAD_DIM, rows)


def _sw_epilogue(p, extras, outs, rows):
    cos_ref, sin_ref = extras
    o_ref, vt_ref = outs
    cos = cos_ref[rows, :]
    sin = sin_ref[rows, :]
    n_q = SW_HEADS * SW_HEAD_DIM // LANES
    n_k = SW_KV_HEADS * SW_HEAD_DIM // LANES
    for j in range(n_q + n_k):
        xs = p[:, j * LANES:(j + 1) * LANES]
        r = xs * cos + _rotate_half(xs, SW_HEAD_DIM // 2) * sin
        if j < n_q:
            r = r * (SW_HEAD_DIM ** -0.5 * LOG2E)
        o_ref[rows, j * LANES:(j + 1) * LANES] = r.astype(BF16)
    _store_values_transposed(vt_ref, p[:, (n_q + n_k) * LANES:], SW_HEAD_DIM, rows)


def _ga_epilogue(p, extras, outs, rows):
    cos_ref, sin_ref, qn_ref, kn_ref = extras
    o_ref, vt_ref = outs
    cos = cos_ref[rows, :]
    sin = sin_ref[rows, :]
    for j in range(GA_HEADS + GA_KV_HEADS):
        xs = _rms(p[:, j * LANES:(j + 1) * LANES])
        if j < GA_HEADS:
            xs = xs * (qn_ref[...] * (GA_HEAD_DIM ** -0.5 * LOG2E))
        else:
            xs = xs * kn_ref[...]
        r = xs * cos + _rotate_half(xs, GA_HEAD_DIM // 2) * sin
        o_ref[rows, j * LANES:(j + 1) * LANES] = r.astype(BF16)
    _store_values_transposed(vt_ref, p[:, (GA_HEADS + GA_KV_HEADS) * LANES:], GA_HEAD_DIM, rows)


def _cv_epilogue(p, extras, outs, rows):
    (b_ref,), (o_ref,) = extras, outs
    u = p + b_ref[...]
    d = o_ref.shape[1]
    o_ref[rows, :] = u[:, :d] * jax.nn.sigmoid(u[:, d:])


def _proj_outputs(n_rows, tm, n_out, out_dtype, vt_heads=0, vt_dim=0):
    shapes = [jax.ShapeDtypeStruct((n_rows, n_out), out_dtype)]
    specs = [pl.BlockSpec((tm, n_out), lambda i: (i, 0))]
    if vt_heads:
        vrows = vt_dim + ONES_ROWS
        shapes.append(jax.ShapeDtypeStruct((vt_heads, vrows, n_rows), BF16))
        specs.append(pl.BlockSpec((vt_heads, vrows, tm), lambda i: (0, 0, i)))
    return shapes, specs


def _rope_specs(tm, seq, n_lat_rows):
    n_seq_blocks = seq // tm
    idx = lambda i: (jnp.where(i * tm < n_lat_rows, i % n_seq_blocks, n_seq_blocks), 0)
    return [pl.BlockSpec((tm, LANES), idx), pl.BlockSpec((tm, LANES), idx)]


def _rope_tables(seq, head_dim, tm):
    t = np.arange(seq)
    row = (t // GRID_W).astype(np.float32)
    col = (t % GRID_W).astype(np.float32)
    n_pairs_axis = head_dim // 4
    freq = jnp.asarray(ROPE_THETA, F32) ** (-jnp.arange(n_pairs_axis, dtype=F32) / n_pairs_axis)
    ang = jnp.concatenate([row[:, None] * freq, col[:, None] * freq], axis=-1)
    cos, sin = jnp.cos(ang), jnp.sin(ang)
    reps = LANES // head_dim
    cos_t = jnp.tile(jnp.concatenate([cos, cos], axis=-1), (1, reps))
    sin_t = jnp.tile(jnp.concatenate([-sin, sin], axis=-1), (1, reps))
    cos_t = jnp.concatenate([cos_t, jnp.ones((tm, LANES), F32)], axis=0)
    sin_t = jnp.concatenate([sin_t, jnp.zeros((tm, LANES), F32)], axis=0)
    return cos_t, sin_t


def _half_split_perm(n_heads, head_dim):
    per_head = np.concatenate([np.arange(0, head_dim, 2), np.arange(1, head_dim, 2)])
    return (np.arange(n_heads)[:, None] * head_dim + per_head[None, :]).reshape(-1)


def _attend(q_ref, o_ref, *, n_heads, dh, lookahead, keys_of, values_of, addends_of, floor_of=None):
    per_store = LANES // dh

    def scores(hd):
        qh = q_ref[:, hd * dh:(hd + 1) * dh]
        ss = []
        for kb, adds in zip(keys_of(hd), addends_of(hd)):
            s = lax.dot_general(kb, qh, _DN_T, preferred_element_type=F32)
            for a in adds:
                s = s + a
            ss.append(s)
        return ss

    def finish(hd, ss):
        m = ss[0].max(axis=0, keepdims=True)
        for s in ss[1:]:
            m = jnp.maximum(m, s.max(axis=0, keepdims=True))
        floor = None if floor_of is None else floor_of(hd)
        if floor is not None:
            m = jnp.maximum(m, floor)
        ot = None
        for s, vb in zip(ss, values_of(hd)):
            part = jnp.dot(vb, jnp.exp2(s - m).astype(BF16), preferred_element_type=F32)
            ot = part if ot is None else ot + part
        l = ot[dh:dh + 1, :]
        if floor is not None:
            l = l + jnp.exp2(floor - m)
        return ot[:dh, :] / l

    pending = [scores(hd) for hd in range(min(lookahead, n_heads))]
    outs = []
    for hd in range(n_heads):
        ss = pending.pop(0)
        if hd + lookahead < n_heads:
            pending.append(scores(hd + lookahead))
        outs.append(finish(hd, ss))
        if len(outs) == per_store:
            o2 = outs[0] if per_store == 1 else jnp.concatenate(outs, axis=0)
            outs = []
            o_ref[:, (hd + 1 - per_store) * dh:(hd + 1) * dh] = o2.T.astype(BF16)


def _na_kernel(q_ref, kl_ref, kc_ref, vlt_ref, vct_ref, val_ref, mask_ref, o_ref, *, n_groups, seq):
    g = pl.program_id(1)
    dh = NA_HEAD_DIM
    tq = NA_QROWS * GRID_W
    n_keys = NA_KROWS * GRID_W
    n_edge = NA_WIN_R * GRID_W
    half = (NA_WIN_R // 2) * GRID_W
    hs = lambda hd: slice(hd * dh, (hd + 1) * dh)
    edge = (g == 0) | (g == n_groups - 1)

    @pl.when((g < n_groups) & jnp.logical_not(edge))
    def _():
        off = pl.multiple_of(g * tq - half, tq)
        _attend(q_ref, o_ref, n_heads=NA_HEADS, dh=dh, lookahead=NA_LOOKAHEAD,
                keys_of=lambda hd: [kl_ref[pl.ds(off, n_keys), hs(hd)], kc_ref[:, hs(hd)]],
                values_of=lambda hd: [vlt_ref[hd, :, pl.ds(off, n_keys)], vct_ref[hd]],
                addends_of=lambda hd: [[val_ref[hd], mask_ref[...]], []])

    @pl.when((g < n_groups) & edge)
    def _():
        koff = pl.multiple_of(jnp.where(g == 0, 0, seq - n_edge), tq)
        toff = pl.multiple_of(jnp.where(g == 0, half, 0), tq)
        _attend(q_ref, o_ref, n_heads=NA_HEADS, dh=dh, lookahead=NA_LOOKAHEAD,
                keys_of=lambda hd: [kl_ref[pl.ds(koff, n_edge), hs(hd)], kc_ref[:, hs(hd)]],
                values_of=lambda hd: [vlt_ref[hd, :, pl.ds(koff, n_edge)], vct_ref[hd]],
                addends_of=lambda hd: [[val_ref[hd, pl.ds(toff, n_edge), :], mask_ref[pl.ds(toff, n_edge), :]], []])

    @pl.when(g >= n_groups)
    def _():
        _attend(q_ref, o_ref, n_heads=NA_HEADS, dh=dh, lookahead=NA_LOOKAHEAD,
                keys_of=lambda hd: [kc_ref[:, hs(hd)]],
                values_of=lambda hd: [vct_ref[hd]],
                addends_of=lambda hd: [[]])


def _na_bias_tables(rpb, rows):
    qr = np.arange(NA_QROWS)
    kri = np.arange(NA_KROWS)
    j = np.arange(GRID_W)
    kc = np.arange(GRID_W)
    half = NA_WIN_R // 2
    ridx = np.clip(kri[:, None] - half - qr[None, :] + NA_WIN_R - 1, 0, 2 * NA_WIN_R - 2)
    cidx = np.clip(kc[:, None] - j[None, :] + NA_WIN_C - 1, 0, 2 * NA_WIN_C - 2)
    n_k, n_q = NA_KROWS * GRID_W, NA_QROWS * GRID_W
    onehot = (cidx[None] == np.arange(2 * NA_WIN_C - 1)[:, None, None]).astype(np.float32)
    val = jnp.einsum("hkqd,dcj->hkcqj", rpb.astype(F32)[:, ridx] * LOG2E, jnp.asarray(onehot),
                     precision=lax.Precision.HIGHEST).reshape(rpb.shape[0], n_k, n_q)
    ws = np.clip(j - NA_WIN_C // 2, 0, GRID_W - NA_WIN_C)
    col_ok = (kc[:, None] >= ws[None, :]) & (kc[:, None] < ws[None, :] + NA_WIN_C)
    masks = []
    for r0 in (0, NA_QROWS, rows - NA_QROWS):
        r = r0 + qr
        rs = np.clip(r - half, 0, rows - NA_WIN_R)
        k_abs = r0 - half + kri
        row_ok = (k_abs[:, None] >= rs[None, :]) & (k_abs[:, None] < rs[None, :] + NA_WIN_R)
        ok = row_ok[:, None, :, None] & col_ok[None, :, None, :]
        masks.append(np.where(ok, 0.0, NEG).reshape(n_k, n_q))
    return val, jnp.asarray(np.stack(masks), F32)


def _na_attention(qk, vt, rpb, *, n_batch, seq, ctx_len, with_ctx_out):
    rows = seq // GRID_W
    qd = NA_HEADS * NA_HEAD_DIM
    tq = NA_QROWS * GRID_W
    n_keys = NA_KROWS * GRID_W
    n_groups = seq // tq
    ctx_tiles = ctx_len // tq
    n_ctx_tiles = ctx_tiles if with_ctx_out else 0
    n_lat = n_batch * seq
    n_rows = (n_lat + n_batch * ctx_len) if with_ctx_out else n_lat
    ctx_blk0 = n_lat // ctx_len
    vrows = NA_HEAD_DIM + ONES_ROWS
    val, mask = _na_bias_tables(rpb, rows)

    def q_idx(b, g):
        return (jnp.where(g < n_groups, b * n_groups + g, n_batch * n_groups + b * ctx_tiles + (g - n_groups)), 0)

    def mask_idx(b, g):
        return (jnp.where(g == 0, 0, jnp.where(g >= n_groups - 1, 2, 1)), 0, 0)

    return pl.pallas_call(
        functools.partial(_na_kernel, n_groups=n_groups, seq=seq),
        out_shape=jax.ShapeDtypeStruct((n_rows, qd), BF16),
        grid=(n_batch, n_groups + n_ctx_tiles),
        in_specs=[
            pl.BlockSpec((tq, qd), q_idx),
            pl.BlockSpec((seq, qd), lambda b, g: (b, 1)),
            pl.BlockSpec((ctx_len, qd), lambda b, g: (ctx_blk0 + b, 1)),
            pl.BlockSpec((NA_HEADS, vrows, seq), lambda b, g: (0, 0, b)),
            pl.BlockSpec((NA_HEADS, vrows, ctx_len), lambda b, g: (0, 0, ctx_blk0 + b)),
            _resident((NA_HEADS, n_keys, tq), lambda b, g: (0, 0, 0)),
            pl.BlockSpec((None, n_keys, tq), mask_idx),
        ],
        out_specs=pl.BlockSpec((tq, qd), q_idx),
        compiler_params=_params(2),
        name="na_attention",
    )(qk, qk, qk, vt, vt, val, mask)


def _sw_kernel(sink_ref, q_ref, kl_ref, kc_ref, vlt_ref, vct_ref, mask_ref, o_ref, *, n_tiles, seq):
    n = pl.program_id(1)
    dh = SW_HEAD_DIM
    grp = SW_HEADS // SW_KV_HEADS
    win = SW_KWIN
    ks = lambda hd: slice((hd // grp) * dh, (hd // grp + 1) * dh)
    floor_of = lambda hd: sink_ref[hd] * LOG2E

    @pl.when(n < n_tiles)
    def _():
        ws = pl.multiple_of(jnp.clip(n * SW_QTILE - SW_WINDOW, 0, seq - win), LANES)
        _attend(q_ref, o_ref, n_heads=SW_HEADS, dh=dh, lookahead=SW_LOOKAHEAD,
                keys_of=lambda hd: [kl_ref[pl.ds(ws, win), ks(hd)], kc_ref[:, ks(hd)]],
                values_of=lambda hd: [vlt_ref[hd // grp, :, pl.ds(ws, win)], vct_ref[hd // grp]],
                addends_of=lambda hd: [[mask_ref[...]], []], floor_of=floor_of)

    @pl.when(n >= n_tiles)
    def _():
        _attend(q_ref, o_ref, n_heads=SW_HEADS, dh=dh, lookahead=SW_LOOKAHEAD,
                keys_of=lambda hd: [kc_ref[:, ks(hd)]],
                values_of=lambda hd: [vct_ref[hd // grp]],
                addends_of=lambda hd: [[]], floor_of=floor_of)


def _sw_attention(proj, vt, sink, *, n_batch, seq, ctx_len, with_ctx_out):
    qd = SW_HEADS * SW_HEAD_DIM
    tq = SW_QTILE
    win = SW_KWIN
    n_tiles = seq // tq
    ctx_tiles = ctx_len // tq
    n_ctx_tiles = ctx_tiles if with_ctx_out else 0
    lat_tiles = n_batch * n_tiles
    n_lat = n_batch * seq
    n_rows = (n_lat + n_batch * ctx_len) if with_ctx_out else n_lat
    k_col = qd // LANES
    ctx_blk0 = n_lat // ctx_len
    vrows = SW_HEAD_DIM + ONES_ROWS
    ki = np.arange(win)[:, None]
    qj = np.arange(tq)[None, :]
    mask = np.stack([np.where(np.abs(ki - qj - off) <= SW_WINDOW, 0.0, NEG) for off in (0, SW_WINDOW, 2 * SW_WINDOW)])
    mask = jnp.asarray(mask, F32)

    def q_idx(b, n, s):
        return (jnp.where(n < n_tiles, b * n_tiles + n, lat_tiles + b * ctx_tiles + (n - n_tiles)), 0)

    def mask_idx(b, n, s):
        nn = jnp.minimum(n, n_tiles - 1)
        ws_blk = jnp.clip(nn * (tq // SW_WINDOW) - 1, 0, (seq - win) // SW_WINDOW)
        return (nn * (tq // SW_WINDOW) - ws_blk, 0, 0)

    grid_spec = pltpu.PrefetchScalarGridSpec(
        num_scalar_prefetch=1,
        grid=(n_batch, n_tiles + n_ctx_tiles),
        in_specs=[
            pl.BlockSpec((tq, qd), q_idx),
            pl.BlockSpec((seq, LANES), lambda b, n, s: (b, k_col)),
            pl.BlockSpec((ctx_len, LANES), lambda b, n, s: (ctx_blk0 + b, k_col)),
            pl.BlockSpec((SW_KV_HEADS, vrows, seq), lambda b, n, s: (0, 0, b)),
            pl.BlockSpec((SW_KV_HEADS, vrows, ctx_len), lambda b, n, s: (0, 0, ctx_blk0 + b)),
            pl.BlockSpec((None, win, tq), mask_idx),
        ],
        out_specs=pl.BlockSpec((tq, qd), q_idx),
    )
    return pl.pallas_call(
        functools.partial(_sw_kernel, n_tiles=n_tiles, seq=seq),
        out_shape=jax.ShapeDtypeStruct((n_rows, qd), BF16),
        grid_spec=grid_spec,
        compiler_params=_params(2),
        name="sw_attention",
    )(sink.astype(F32), proj, proj, proj, vt, vt, mask)


def _ga_kernel(q_ref, kl_ref, kc_ref, vlt_ref, vct_ref, o_ref):
    dh = GA_HEAD_DIM
    grp = GA_HEADS // GA_KV_HEADS
    ks = lambda hd: slice((hd // grp) * dh, (hd // grp + 1) * dh)
    _attend(q_ref, o_ref, n_heads=GA_HEADS, dh=dh, lookahead=GA_LOOKAHEAD,
            keys_of=lambda hd: [kl_ref[:, ks(hd)], kc_ref[:, ks(hd)]],
            values_of=lambda hd: [vlt_ref[hd // grp], vct_ref[hd // grp]],
            addends_of=lambda hd: [[], []])


def _ga_attention(proj, vt, *, n_batch, seq, ctx_len):
    qd = GA_HEADS * GA_HEAD_DIM
    kvd = GA_KV_HEADS * GA_HEAD_DIM
    tq = GA_QTILE
    n_tiles = seq // tq
    n_lat = n_batch * seq
    k_col = qd // kvd
    ctx_blk0 = n_lat // ctx_len
    vrows = GA_HEAD_DIM + ONES_ROWS
    return pl.pallas_call(
        _ga_kernel,
        out_shape=jax.ShapeDtypeStruct((n_lat, qd), BF16),
        grid=(n_batch, n_tiles),
        in_specs=[
            pl.BlockSpec((tq, qd), lambda b, n: (b * n_tiles + n, 0)),
            pl.BlockSpec((seq, kvd), lambda b, n: (b, k_col)),
            pl.BlockSpec((ctx_len, kvd), lambda b, n: (ctx_blk0 + b, k_col)),
            pl.BlockSpec((GA_KV_HEADS, vrows, seq), lambda b, n: (0, 0, b)),
            pl.BlockSpec((GA_KV_HEADS, vrows, ctx_len), lambda b, n: (0, 0, ctx_blk0 + b)),
        ],
        out_specs=pl.BlockSpec((tq, qd), lambda b, n: (b * n_tiles + n, 0)),
        compiler_params=_params(2),
        name="ga_attention",
    )(proj, proj, proj, vt, vt)


def _conv_kernel(um_ref, up_ref, un_ref, w_ref, b_ref, lg_ref, lb_ref, o_ref, win_ref, sh_ref, acc_ref, *,
                 lat_tiles, ctx_tiles):
    j = pl.program_id(1)
    jc = j - lat_tiles
    first = jnp.where(j < lat_tiles, j == 0, jc == 0)
    last = jnp.where(j < lat_tiles, j == lat_tiles - 1, jc == ctx_tiles - 1)
    tl = um_ref.shape[0]
    d = um_ref.shape[1]
    sh_rows = sh_ref.shape[1]
    win_ref[0:CONV_HALO, :] = jnp.where(first, 0.0, up_ref[...])
    win_ref[CONV_HALO:CONV_HALO + tl, :] = um_ref[...]
    win_ref[CONV_HALO + tl:, :] = jnp.where(last, 0.0, un_ref[...])
    base = CONV_HALO - CONV_WIDTH // 2

    def lane_slab(s, carry):
        ls = pl.ds(pl.multiple_of(s * LANES, LANES), LANES)
        for r in range(1, SUBLANES):
            sh_ref[r] = win_ref[r:r + sh_rows, ls]
        for rb in range(tl // CONV_ROWS):
            acc = jnp.broadcast_to(b_ref[:, ls], (CONV_ROWS, LANES))
            for k in range(CONV_WIDTH):
                a8, r = divmod(base + k, SUBLANES)
                lo = a8 * SUBLANES + rb * CONV_ROWS
                src = win_ref[lo:lo + CONV_ROWS, ls] if r == 0 else sh_ref[r, lo:lo + CONV_ROWS, :]
                acc = acc + w_ref[k:k + 1, ls] * src
            acc_ref[rb * CONV_ROWS:(rb + 1) * CONV_ROWS, ls] = acc
        return carry

    lax.fori_loop(0, d // LANES, lane_slab, 0)
    u = acc_ref[...]
    mu = jnp.mean(u, axis=-1, keepdims=True)
    var = jnp.mean(jnp.square(u - mu), axis=-1, keepdims=True)
    y = (u - mu) * lax.rsqrt(var + EPS) * lg_ref[...] + lb_ref[...]
    o_ref[...] = _silu(y).astype(BF16)


def _conv_core(u, w_dw, b_dw, ln_g, ln_b, *, n_batch, seq, ctx_len, with_ctx_out):
    d = u.shape[1]
    tl = CONV_TILE
    lat_tiles = seq // tl
    ctx_tiles = ctx_len // tl
    n_ctx = ctx_tiles if with_ctx_out else 0
    n_rows = (n_batch * seq + n_batch * ctx_len) if with_ctx_out else n_batch * seq
    halo_per_tile = tl // CONV_HALO
    n_halo_blocks = u.shape[0] // CONV_HALO
    sh_rows = tl + 2 * CONV_HALO - SUBLANES

    def main(b, j):
        return jnp.where(j < lat_tiles, b * lat_tiles + j, n_batch * lat_tiles + b * ctx_tiles + (j - lat_tiles))

    return pl.pallas_call(
        functools.partial(_conv_kernel, lat_tiles=lat_tiles, ctx_tiles=ctx_tiles),
        out_shape=jax.ShapeDtypeStruct((n_rows, d), BF16),
        grid=(n_batch, lat_tiles + n_ctx),
        in_specs=[
            pl.BlockSpec((tl, d), lambda b, j: (main(b, j), 0)),
            pl.BlockSpec((CONV_HALO, d), lambda b, j: (jnp.maximum(main(b, j) * halo_per_tile - 1, 0), 0)),
            pl.BlockSpec((CONV_HALO, d),
                         lambda b, j: (jnp.minimum((main(b, j) + 1) * halo_per_tile, n_halo_blocks - 1), 0)),
            pl.BlockSpec((CONV_WIDTH, d), lambda b, j: (0, 0)),
            pl.BlockSpec((1, d), lambda b, j: (0, 0)),
            pl.BlockSpec((1, d), lambda b, j: (0, 0)),
            pl.BlockSpec((1, d), lambda b, j: (0, 0)),
        ],
        out_specs=pl.BlockSpec((tl, d), lambda b, j: (main(b, j), 0)),
        scratch_shapes=[pltpu.VMEM((tl + 2 * CONV_HALO, d), F32), pltpu.VMEM((SUBLANES, sh_rows, LANES), F32),
                        pltpu.VMEM((tl, d), F32)],
        compiler_params=_params(2),
        name="conv_core",
    )(u, u, u, w_dw, b_dw.reshape(1, d), ln_g.reshape(1, d), ln_b.reshape(1, d))


def kernel(x, c, ctx, c_ctx, mod_w, mod_b, norm_g, ffn_w_gate, ffn_w_up, ffn_w_down, na_w_qkv, na_w_o, na_rpb, sw_w_qkv, sw_w_o, sw_sink, cv_w_pw1, cv_b_pw1, cv_w_dw, cv_b_dw, cv_ln_g, cv_ln_b, cv_w_pw2, cv_b_pw2, ga_w_qkv, ga_w_o, ga_q_norm, ga_k_norm):
    n_batch, seq, d = x.shape
    ctx_len = ctx.shape[1]
    depth = mod_w.shape[0]
    n_lat = n_batch * seq
    n_ctx = n_batch * ctx_len
    n_all = n_lat + n_ctx
    assert d == D_MODEL and n_batch < MOD_ROWS and seq % CONV_TILE == 0 and ctx_len % CONV_TILE == 0
    assert seq % ctx_len == 0 and seq >= SW_KWIN and seq % SW_QTILE == 0 and ctx_len % SW_QTILE == 0
    assert (seq // GRID_W) % NA_QROWS == 0 and (seq // GRID_W) >= NA_KROWS and seq % GA_QTILE == 0
    tm = _row_tile(n_lat, n_ctx, seq)
    common = dict(tm=tm, seq=seq, n_batch=n_batch)

    c_all = jnp.concatenate([c, c_ctx[None, :], jnp.zeros((MOD_ROWS - n_batch - 1, d), F32)], axis=0)
    mod = _modulation(c_all, mod_w, mod_b).reshape(depth, MOD_ROWS, N_MOD, d)

    wg, wu, wd = ffn_w_gate, ffn_w_up, ffn_w_down
    zero_bias = jnp.zeros((1, d), F32)

    xs = x.reshape(n_lat, d)
    for i in range(depth):
        kind, j = i % N_MIXERS, i // N_MIXERS
        last = i == depth - 1
        ctx_in = not (last and kind == 2)
        ctx_out = not last
        rows_in = n_all if ctx_in else n_lat
        rows_out = n_all if ctx_out else n_lat
        if kind == 0:
            qd = NA_HEADS * NA_HEAD_DIM
            proj = (_na_epilogue, na_w_qkv[j].astype(BF16), [], [],
                    *_proj_outputs(rows_in, tm, 2 * qd, BF16, NA_HEADS, NA_HEAD_DIM))
        elif kind == 1:
            qd, kvd = SW_HEADS * SW_HEAD_DIM, SW_KV_HEADS * SW_HEAD_DIM
            perm = np.concatenate([_half_split_perm(SW_HEADS, SW_HEAD_DIM),
                                   qd + _half_split_perm(SW_KV_HEADS, SW_HEAD_DIM),
                                   np.arange(qd + kvd, qd + 2 * kvd)])
            cos_t, sin_t = _rope_tables(seq, SW_HEAD_DIM, tm)
            proj = (_sw_epilogue, sw_w_qkv[j][:, perm].astype(BF16), [cos_t, sin_t], _rope_specs(tm, seq, n_lat),
                    *_proj_outputs(rows_in, tm, qd + kvd, BF16, SW_KV_HEADS, SW_HEAD_DIM))
        elif kind == 2:
            proj = (_cv_epilogue, cv_w_pw1[j].astype(BF16), [cv_b_pw1[j].reshape(1, -1)],
                    [pl.BlockSpec((1, 2 * d), lambda t: (0, 0))], *_proj_outputs(rows_in, tm, d, F32))
        else:
            assert not ctx_out, "context outputs of the global mixer are not implemented"
            qd, kvd = GA_HEADS * GA_HEAD_DIM, GA_KV_HEADS * GA_HEAD_DIM
            perm = np.concatenate([_half_split_perm(GA_HEADS, GA_HEAD_DIM),
                                   qd + _half_split_perm(GA_KV_HEADS, GA_HEAD_DIM),
                                   np.arange(qd + kvd, qd + 2 * kvd)])
            hperm = _half_split_perm(1, GA_HEAD_DIM)
            cos_t, sin_t = _rope_tables(seq, GA_HEAD_DIM, tm)
            vec = pl.BlockSpec((1, GA_HEAD_DIM), lambda t: (0, 0))
            proj = (_ga_epilogue, ga_w_qkv[j][:, perm].astype(BF16),
                    [cos_t, sin_t, ga_q_norm[j][hperm].reshape(1, -1), ga_k_norm[j][hperm].reshape(1, -1)],
                    _rope_specs(tm, seq, n_lat) + [vec, vec],
                    *_proj_outputs(rows_in, tm, qd + kvd, BF16, GA_KV_HEADS, GA_HEAD_DIM))
        x2 = ctx.reshape(n_ctx, d) if (i == 0 and ctx_in) else None
        xs, *pj = _ffn(xs, mod, norm_g, wg, wu, wd, layer=i, half=0, n_rows=rows_in, **common, x2=x2, proj=proj)
        if kind == 0:
            a = _na_attention(*pj, na_rpb[j], n_batch=n_batch, seq=seq, ctx_len=ctx_len, with_ctx_out=ctx_out)
            w_o, b_o = na_w_o[j], zero_bias
        elif kind == 1:
            a = _sw_attention(*pj, sw_sink[j], n_batch=n_batch, seq=seq, ctx_len=ctx_len, with_ctx_out=ctx_out)
            w_o, b_o = sw_w_o[j], zero_bias
        elif kind == 2:
            a = _conv_core(*pj, cv_w_dw[j], cv_b_dw[j], cv_ln_g[j], cv_ln_b[j], n_batch=n_batch, seq=seq,
                           ctx_len=ctx_len, with_ctx_out=ctx_out)
            w_o, b_o = cv_w_pw2[j], cv_b_pw2[j].reshape(1, d)
        else:
            a = _ga_attention(*pj, n_batch=n_batch, seq=seq, ctx_len=ctx_len)
            w_o, b_o = ga_w_o[j], zero_bias
        xs = _ffn(xs, mod, norm_g, wg, wu, wd, layer=i, half=1, n_rows=rows_out, **common,
                  mixer_out=(a, w_o.astype(BF16), b_o))
    return xs[:n_lat].reshape(n_batch, seq, d)
```

```python
import functools

import numpy as np
import jax
import jax.numpy as jnp
from jax import lax
from jax.experimental import pallas as pl
from jax.experimental.pallas import tpu as pltpu

F32 = jnp.float32
BF16 = jnp.bfloat16

D_MODEL = 1024
DEPTH = 4
GRID_W = 64
N_MIXERS = 4
EPS = 1e-6
NEG = -1e30
LOG2E = 1.4426950408889634
ROPE_THETA = 10000.0
D_FF = 2816
FFN_RESIDUAL_WEIGHT = 0.5
N_MOD = 9
NA_HEADS = 16
NA_HEAD_DIM = 64
NA_WIN_R = 8
NA_WIN_C = 16
NA_QROWS = 4
NA_KROWS = NA_QROWS + NA_WIN_R
SW_HEADS = 16
SW_KV_HEADS = 2
SW_HEAD_DIM = 64
SW_WINDOW = 128
SW_QTILE = 256
SW_KWIN = SW_QTILE + 2 * SW_WINDOW
ONES_ROWS = 16
NA_LOOKAHEAD, SW_LOOKAHEAD, GA_LOOKAHEAD = 2, 3, 2
CONV_WIDTH = 31
GA_HEADS = 8
GA_KV_HEADS = 2
GA_HEAD_DIM = 128
GA_QTILE = 256

LANES = 128
SUBLANES = 8
CONV_HALO = 16
CONV_TILE = 256
CONV_ROWS = 128
W_CHUNKS = 16
W_SLOTS = 4
PROJ_SUB_BLOCKS = 2
MOD_ROWS = 16
VMEM_LIMIT = 60 * 1024 * 1024

_DN_T = (((1,), (1,)), ((), ()))


def _params(n_axes, vmem=VMEM_LIMIT):
    return pltpu.CompilerParams(dimension_semantics=("arbitrary",) * n_axes, vmem_limit_bytes=vmem)


def _resident(block_shape, index_map):
    return pl.BlockSpec(block_shape, index_map, pipeline_mode=pl.Buffered(1))


def _rms(x):
    return x * lax.rsqrt(jnp.mean(x * x, axis=-1, keepdims=True) + EPS)


def _silu(x):
    return x * jax.nn.sigmoid(x)


def _row_tile(n_lat_rows, n_ctx_rows, seq):
    for t in (512, 256, 128):
        if seq % t == 0 and n_ctx_rows % t == 0 and n_lat_rows % t == 0:
            return t
    raise ValueError("unsupported token counts")


def _mod_kernel(c_ref, w_ref, b_ref, o_ref):
    a = _silu(c_ref[...]).astype(BF16)
    o_ref[0] = jnp.dot(a, w_ref[0].astype(BF16), preferred_element_type=F32) + b_ref[0]


def _modulation(c_all, mod_w, mod_b):
    depth, d, n = mod_w.shape
    tn = 1152
    return pl.pallas_call(
        _mod_kernel,
        out_shape=jax.ShapeDtypeStruct((depth, MOD_ROWS, n), F32),
        grid=(depth, n // tn),
        in_specs=[
            pl.BlockSpec((MOD_ROWS, d), lambda i, j: (0, 0)),
            pl.BlockSpec((1, d, tn), lambda i, j: (i, 0, j)),
            pl.BlockSpec((1, 1, tn), lambda i, j: (i, 0, j)),
        ],
        out_specs=pl.BlockSpec((1, MOD_ROWS, tn), lambda i, j: (i, 0, j)),
        compiler_params=_params(2),
        name="modulation",
    )(c_all, mod_w, mod_b.reshape(depth, 1, n))


def _load_weights_bf16(layer, half, streams):
    jobs = []
    ring_pos = {}
    for w_hbm, dst_ref, stage_ref, sem_ref in streams:
        rows = stage_ref.shape[1]
        for c in range(dst_ref.shape[0] // rows):
            slot = ring_pos.get(id(stage_ref), 0) % stage_ref.shape[0]
            ring_pos[id(stage_ref)] = ring_pos.get(id(stage_ref), 0) + 1
            copy = pltpu.make_async_copy(w_hbm.at[layer, half, pl.ds(c * rows, rows), :], stage_ref.at[slot],
                                         sem_ref.at[slot])
            jobs.append((copy, dst_ref, stage_ref, slot, c * rows, rows))
    ahead = W_SLOTS - 1
    for copy, *_ in jobs[:ahead]:
        copy.start()
    for i, (copy, dst_ref, stage_ref, slot, r0, rows) in enumerate(jobs):
        if i + ahead < len(jobs):
            jobs[i + ahead][0].start()
        copy.wait()
        dst_ref[r0:r0 + rows, :] = stage_ref[slot].astype(BF16)


def _ffn_kernel(*refs, layer, half, split, mix, epilogue, n_in, n_out, n_sub):
    ins, outs = iter(refs[:n_in]), refs[n_in:n_in + n_out]
    wg_ref, wu_ref, wd_ref, stage_a, stage_b, sem_a, sem_b = refs[n_in + n_out:]
    slot = 2 * half
    x_ref = next(ins)
    x2_ref = next(ins) if split is not None else None
    if mix:
        a_ref, wo_ref, bo_ref = next(ins), next(ins), next(ins)
    m_ref, g_ref, wg_hbm, wu_hbm, wd_hbm = (next(ins) for _ in range(5))
    if epilogue is not None:
        wq_ref = next(ins)
        extras = list(ins)

    @pl.when(pl.program_id(0) == 0)
    def _():
        _load_weights_bf16(layer, half, [(wg_hbm, wg_ref, stage_a, sem_a), (wu_hbm, wu_ref, stage_a, sem_a),
                                         (wd_hbm, wd_ref, stage_b, sem_b)])

    shift = m_ref[3 * slot:3 * slot + 1, :]
    scale = m_ref[3 * slot + 1:3 * slot + 2, :]
    gate = m_ref[3 * slot + 2:3 * slot + 3, :]
    g_pre = g_ref[2 * slot:2 * slot + 1, :]
    g_post = g_ref[2 * slot + 1:2 * slot + 2, :]
    sub = x_ref.shape[0] // n_sub
    for r in range(n_sub):
        rows = slice(r * sub, (r + 1) * sub)
        x = x_ref[rows, :]
        if split is not None:
            x = jnp.where(pl.program_id(0) < split, x, x2_ref[rows, :])
        if mix:
            ym = jnp.dot(a_ref[rows, :], wo_ref[...], preferred_element_type=F32) + bo_ref[...]
            x = x + m_ref[5:6, :] * (_rms(ym) * g_ref[3:4, :])
        h = (_rms(x) * g_pre * (1.0 + scale) + shift).astype(BF16)
        gt = jnp.dot(h, wg_ref[...], preferred_element_type=F32)
        up = jnp.dot(h, wu_ref[...], preferred_element_type=F32)
        a = (_silu(gt) * up).astype(BF16)
        y = jnp.dot(a, wd_ref[...], preferred_element_type=F32)
        x = x + (FFN_RESIDUAL_WEIGHT * gate) * (_rms(y) * g_post)
        outs[0][rows, :] = x
        if epilogue is not None:
            hq = (_rms(x) * g_ref[2:3, :] * (1.0 + m_ref[4:5, :]) + m_ref[3:4, :]).astype(BF16)
            epilogue(jnp.dot(hq, wq_ref[...], preferred_element_type=F32), extras, outs[1:], rows)


def _group_map(tm, seq, n_batch):
    return lambda i: jnp.minimum((i * tm) // seq, n_batch)


def _ffn(x, mod, norm_g, wg, wu, wd, *, layer, half, n_rows, tm, seq, n_batch, x2=None, mixer_out=None, proj=None):
    d = x.shape[1]
    f = wg.shape[-1]
    slot = 2 * half
    grp = _group_map(tm, seq, n_batch)
    if x2 is None:
        split = None
        xs, x_specs = [x], [pl.BlockSpec((tm, d), lambda i: (i, 0))]
    else:
        split = x.shape[0] // tm
        xs = [x, x2]
        x_specs = [pl.BlockSpec((tm, d), lambda i: (jnp.minimum(i, split - 1), 0)),
                   pl.BlockSpec((tm, d), lambda i: (jnp.maximum(i - split, 0), 0))]
    if mixer_out is not None:
        a, w_o, b_o = mixer_out
        xs += [a, w_o, b_o]
        x_specs += [pl.BlockSpec((tm, a.shape[1]), lambda i: (i, 0)), _resident(w_o.shape, lambda i: (0, 0)),
                    pl.BlockSpec((1, d), lambda i: (0, 0))]
    xs += [mod, norm_g, wg, wu, wd]
    x_specs += [
        pl.BlockSpec((None, None, N_MOD, d), lambda i: (layer, grp(i), 0, 0)),
        pl.BlockSpec((None, 6, d), lambda i: (layer, 0, 0)),
        pl.BlockSpec(memory_space=pl.ANY),
        pl.BlockSpec(memory_space=pl.ANY),
        pl.BlockSpec(memory_space=pl.ANY),
    ]
    scratch = [pltpu.VMEM((d, f), BF16), pltpu.VMEM((d, f), BF16), pltpu.VMEM((f, d), BF16),
               pltpu.VMEM((W_SLOTS, d // W_CHUNKS, f), F32), pltpu.VMEM((W_SLOTS, f // W_CHUNKS, d), F32),
               pltpu.SemaphoreType.DMA((W_SLOTS,)), pltpu.SemaphoreType.DMA((W_SLOTS,))]
    out_shape = [jax.ShapeDtypeStruct((n_rows, d), F32)]
    out_specs = [pl.BlockSpec((tm, d), lambda i: (i, 0))]
    epilogue = None
    n_sub = 1
    if proj is not None:
        epilogue, w_qkv, extras, extra_specs, p_shapes, p_specs = proj
        n_sub = PROJ_SUB_BLOCKS if (tm >= 512 and epilogue is not _na_epilogue) else 1
        xs += [w_qkv] + extras
        x_specs += [_resident(w_qkv.shape, lambda i: (0, 0))] + extra_specs
        out_shape += p_shapes
        out_specs += p_specs
    res = pl.pallas_call(
        functools.partial(_ffn_kernel, layer=layer, half=half, split=split, mix=mixer_out is not None,
                          epilogue=epilogue, n_in=len(xs), n_out=len(out_shape), n_sub=n_sub),
        out_shape=out_shape,
        grid=(n_rows // tm,),
        in_specs=x_specs,
        out_specs=out_specs,
        scratch_shapes=scratch,
        compiler_params=_params(1),
        name=f"ffn_l{layer}h{half}",
    )(*xs)
    return res[0] if proj is None else res


def _rotate_half(x, half):
    if 2 * half == LANES:
        return pltpu.roll(x, half, axis=1)
    lane = lax.broadcasted_iota(jnp.int32, x.shape, 1)
    return jnp.where(lane % (2 * half) < half, pltpu.roll(x, LANES - half, axis=1), pltpu.roll(x, half, axis=1))


def _store_values_transposed(vt_ref, v, dh, rows):
    per = LANES // dh
    for j in range(v.shape[1] // LANES):
        t = v[:, j * LANES:(j + 1) * LANES].T
        for i in range(per):
            vt_ref[j * per + i, 0:dh, rows] = t[i * dh:(i + 1) * dh, :].astype(BF16)
    vt_ref[:, dh:, rows] = jnp.ones((vt_ref.shape[0], ONES_ROWS, v.shape[0]), BF16)


def _na_epilogue(p, extras, outs, rows):
    o_ref, vt_ref = outs
    qd = NA_HEADS * NA_HEAD_DIM
    o_ref[rows, :qd] = (p[:, :qd] * (NA_HEAD_DIM ** -0.5 * LOG2E)).astype(BF16)
    o_ref[rows, qd:] = p[:, qd:2 * qd].astype(BF16)
    _store_values_transposed(vt_ref, p[:, 2 * qd:], NA_HEAD_DIM, rows)


def _sw_epilogue(p, extras, outs, rows):
    cos_ref, sin_ref = extras
    o_ref, vt_ref = outs
    cos = cos_ref[rows, :]
    sin = sin_ref[rows, :]
    n_q = SW_HEADS * SW_HEAD_DIM // LANES
    n_k = SW_KV_HEADS * SW_HEAD_DIM // LANES
    for j in range(n_q + n_k):
        xs = p[:, j * LANES:(j + 1) * LANES]
        r = xs * cos + _rotate_half(xs, SW_HEAD_DIM // 2) * sin
        if j < n_q:
            r = r * (SW_HEAD_DIM ** -0.5 * LOG2E)
        o_ref[rows, j * LANES:(j + 1) * LANES] = r.astype(BF16)
    _store_values_transposed(vt_ref, p[:, (n_q + n_k) * LANES:], SW_HEAD_DIM, rows)


def _ga_epilogue(p, extras, outs, rows):
    cos_ref, sin_ref, qn_ref, kn_ref = extras
    o_ref, vt_ref = outs
    cos = cos_ref[rows, :]
    sin = sin_ref[rows, :]
    for j in range(GA_HEADS + GA_KV_HEADS):
        xs = _rms(p[:, j * LANES:(j + 1) * LANES])
        if j < GA_HEADS:
            xs = xs * (qn_ref[...] * (GA_HEAD_DIM ** -0.5 * LOG2E))
        else:
            xs = xs * kn_ref[...]
        r = xs * cos + _rotate_half(xs, GA_HEAD_DIM // 2) * sin
        o_ref[rows, j * LANES:(j + 1) * LANES] = r.astype(BF16)
    _store_values_transposed(vt_ref, p[:, (GA_HEADS + GA_KV_HEADS) * LANES:], GA_HEAD_DIM, rows)


def _cv_epilogue(p, extras, outs, rows):
    (b_ref,), (o_ref,) = extras, outs
    u = p + b_ref[...]
    d = o_ref.shape[1]
    o_ref[rows, :] = u[:, :d] * jax.nn.sigmoid(u[:, d:])


def _proj_outputs(n_rows, tm, n_out, out_dtype, vt_heads=0, vt_dim=0):
    shapes = [jax.ShapeDtypeStruct((n_rows, n_out), out_dtype)]
    specs = [pl.BlockSpec((tm, n_out), lambda i: (i, 0))]
    if vt_heads:
        vrows = vt_dim + ONES_ROWS
        shapes.append(jax.ShapeDtypeStruct((vt_heads, vrows, n_rows), BF16))
        specs.append(pl.BlockSpec((vt_heads, vrows, tm), lambda i: (0, 0, i)))
    return shapes, specs


def _rope_specs(tm, seq, n_lat_rows):
    n_seq_blocks = seq // tm
    idx = lambda i: (jnp.where(i * tm < n_lat_rows, i % n_seq_blocks, n_seq_blocks), 0)
    return [pl.BlockSpec((tm, LANES), idx), pl.BlockSpec((tm, LANES), idx)]


def _rope_tables(seq, head_dim, tm):
    t = np.arange(seq)
    row = (t // GRID_W).astype(np.float32)
    col = (t % GRID_W).astype(np.float32)
    n_pairs_axis = head_dim // 4
    freq = jnp.asarray(ROPE_THETA, F32) ** (-jnp.arange(n_pairs_axis, dtype=F32) / n_pairs_axis)
    ang = jnp.concatenate([row[:, None] * freq, col[:, None] * freq], axis=-1)
    cos, sin = jnp.cos(ang), jnp.sin(ang)
    reps = LANES // head_dim
    cos_t = jnp.tile(jnp.concatenate([cos, cos], axis=-1), (1, reps))
    sin_t = jnp.tile(jnp.concatenate([-sin, sin], axis=-1), (1, reps))
    cos_t = jnp.concatenate([cos_t, jnp.ones((tm, LANES), F32)], axis=0)
    sin_t = jnp.concatenate([sin_t, jnp.zeros((tm, LANES), F32)], axis=0)
    return cos_t, sin_t


def _half_split_perm(n_heads, head_dim):
    per_head = np.concatenate([np.arange(0, head_dim, 2), np.arange(1, head_dim, 2)])
    return (np.arange(n_heads)[:, None] * head_dim + per_head[None, :]).reshape(-1)


def _attend(q_ref, o_ref, *, n_heads, dh, lookahead, keys_of, values_of, addends_of, floor_of=None):
    per_store = LANES // dh

    def scores(hd):
        qh = q_ref[:, hd * dh:(hd + 1) * dh]
        ss = []
        for kb, adds in zip(keys_of(hd), addends_of(hd)):
            s = lax.dot_general(kb, qh, _DN_T, preferred_element_type=F32)
            for a in adds:
                s = s + a
            ss.append(s)
        return ss

    def finish(hd, ss):
        m = ss[0].max(axis=0, keepdims=True)
        for s in ss[1:]:
            m = jnp.maximum(m, s.max(axis=0, keepdims=True))
        floor = None if floor_of is None else floor_of(hd)
        if floor is not None:
            m = jnp.maximum(m, floor)
        ot = None
        for s, vb in zip(ss, values_of(hd)):
            part = jnp.dot(vb, jnp.exp2(s - m).astype(BF16), preferred_element_type=F32)
            ot = part if ot is None else ot + part
        l = ot[dh:dh + 1, :]
        if floor is not None:
            l = l + jnp.exp2(floor - m)
        return ot[:dh, :] / l

    pending = [scores(hd) for hd in range(min(lookahead, n_heads))]
    outs = []
    for hd in range(n_heads):
        ss = pending.pop(0)
        if hd + lookahead < n_heads:
            pending.append(scores(hd + lookahead))
        outs.append(finish(hd, ss))
        if len(outs) == per_store:
            o2 = outs[0] if per_store == 1 else jnp.concatenate(outs, axis=0)
            outs = []
            o_ref[:, (hd + 1 - per_store) * dh:(hd + 1) * dh] = o2.T.astype(BF16)


def _na_kernel(q_ref, kl_ref, kc_ref, vlt_ref, vct_ref, val_ref, mask_ref, o_ref, *, n_groups, seq):
    g = pl.program_id(1)
    dh = NA_HEAD_DIM
    tq = NA_QROWS * GRID_W
    n_keys = NA_KROWS * GRID_W
    n_edge = NA_WIN_R * GRID_W
    half = (NA_WIN_R // 2) * GRID_W
    hs = lambda hd: slice(hd * dh, (hd + 1) * dh)
    edge = (g == 0) | (g == n_groups - 1)

    @pl.when((g < n_groups) & jnp.logical_not(edge))
    def _():
        off = pl.multiple_of(g * tq - half, tq)
        _attend(q_ref, o_ref, n_heads=NA_HEADS, dh=dh, lookahead=NA_LOOKAHEAD,
                keys_of=lambda hd: [kl_ref[pl.ds(off, n_keys), hs(hd)], kc_ref[:, hs(hd)]],
                values_of=lambda hd: [vlt_ref[hd, :, pl.ds(off, n_keys)], vct_ref[hd]],
                addends_of=lambda hd: [[val_ref[hd], mask_ref[...]], []])

    @pl.when((g < n_groups) & edge)
    def _():
        koff = pl.multiple_of(jnp.where(g == 0, 0, seq - n_edge), tq)
        toff = pl.multiple_of(jnp.where(g == 0, half, 0), tq)
        _attend(q_ref, o_ref, n_heads=NA_HEADS, dh=dh, lookahead=NA_LOOKAHEAD,
                keys_of=lambda hd: [kl_ref[pl.ds(koff, n_edge), hs(hd)], kc_ref[:, hs(hd)]],
                values_of=lambda hd: [vlt_ref[hd, :, pl.ds(koff, n_edge)], vct_ref[hd]],
                addends_of=lambda hd: [[val_ref[hd, pl.ds(toff, n_edge), :], mask_ref[pl.ds(toff, n_edge), :]], []])

    @pl.when(g >= n_groups)
    def _():
        _attend(q_ref, o_ref, n_heads=NA_HEADS, dh=dh, lookahead=NA_LOOKAHEAD,
                keys_of=lambda hd: [kc_ref[:, hs(hd)]],
                values_of=lambda hd: [vct_ref[hd]],
                addends_of=lambda hd: [[]])


def _na_bias_tables(rpb, rows):
    qr = np.arange(NA_QROWS)
    kri = np.arange(NA_KROWS)
    j = np.arange(GRID_W)
    kc = np.arange(GRID_W)
    half = NA_WIN_R // 2
    ridx = np.clip(kri[:, None] - half - qr[None, :] + NA_WIN_R - 1, 0, 2 * NA_WIN_R - 2)
    cidx = np.clip(kc[:, None] - j[None, :] + NA_WIN_C - 1, 0, 2 * NA_WIN_C - 2)
    n_k, n_q = NA_KROWS * GRID_W, NA_QROWS * GRID_W
    n_dc = 2 * NA_WIN_C - 1
    col_hot = cidx[None] == np.arange(n_dc)[:, None, None]
    row_hot = np.eye(NA_QROWS, dtype=bool)
    onehot = (row_hot[:, None, None, :, None] & col_hot[None, :, :, None, :]).astype(np.float32)
    onehot = onehot.reshape(NA_QROWS * n_dc, GRID_W * n_q)
    rows_sel = (rpb.astype(F32)[:, ridx] * LOG2E).reshape(rpb.shape[0] * NA_KROWS, NA_QROWS * n_dc)
    val = jnp.dot(rows_sel, jnp.asarray(onehot), precision=lax.Precision.HIGHEST).reshape(rpb.shape[0], n_k, n_q)
    ws = np.clip(j - NA_WIN_C // 2, 0, GRID_W - NA_WIN_C)
    col_ok = (kc[:, None] >= ws[None, :]) & (kc[:, None] < ws[None, :] + NA_WIN_C)
    masks = []
    for r0 in (0, NA_QROWS, rows - NA_QROWS):
        r = r0 + qr
        rs = np.clip(r - half, 0, rows - NA_WIN_R)
        k_abs = r0 - half + kri
        row_ok = (k_abs[:, None] >= rs[None, :]) & (k_abs[:, None] < rs[None, :] + NA_WIN_R)
        ok = row_ok[:, None, :, None] & col_ok[None, :, None, :]
        masks.append(np.where(ok, 0.0, NEG).reshape(n_k, n_q))
    return val, jnp.asarray(np.stack(masks), F32)


def _na_attention(qk, vt, rpb, *, n_batch, seq, ctx_len, with_ctx_out):
    rows = seq // GRID_W
    qd = NA_HEADS * NA_HEAD_DIM
    tq = NA_QROWS * GRID_W
    n_keys = NA_KROWS * GRID_W
    n_groups = seq // tq
    ctx_tiles = ctx_len // tq
    n_ctx_tiles = ctx_tiles if with_ctx_out else 0
    n_lat = n_batch * seq
    n_rows = (n_lat + n_batch * ctx_len) if with_ctx_out else n_lat
    ctx_blk0 = n_lat // ctx_len
    vrows = NA_HEAD_DIM + ONES_ROWS
    val, mask = _na_bias_tables(rpb, rows)

    def q_idx(b, g):
        return (jnp.where(g < n_groups, b * n_groups + g, n_batch * n_groups + b * ctx_tiles + (g - n_groups)), 0)

    def mask_idx(b, g):
        return (jnp.where(g == 0, 0, jnp.where(g >= n_groups - 1, 2, 1)), 0, 0)

    return pl.pallas_call(
        functools.partial(_na_kernel, n_groups=n_groups, seq=seq),
        out_shape=jax.ShapeDtypeStruct((n_rows, qd), BF16),
        grid=(n_batch, n_groups + n_ctx_tiles),
        in_specs=[
            pl.BlockSpec((tq, qd), q_idx),
            pl.BlockSpec((seq, qd), lambda b, g: (b, 1)),
            pl.BlockSpec((ctx_len, qd), lambda b, g: (ctx_blk0 + b, 1)),
            pl.BlockSpec((NA_HEADS, vrows, seq), lambda b, g: (0, 0, b)),
            pl.BlockSpec((NA_HEADS, vrows, ctx_len), lambda b, g: (0, 0, ctx_blk0 + b)),
            _resident((NA_HEADS, n_keys, tq), lambda b, g: (0, 0, 0)),
            pl.BlockSpec((None, n_keys, tq), mask_idx),
        ],
        out_specs=pl.BlockSpec((tq, qd), q_idx),
        compiler_params=_params(2),
        name="na_attention",
    )(qk, qk, qk, vt, vt, val, mask)


def _sw_kernel(sink_ref, q_ref, kl_ref, kc_ref, vlt_ref, vct_ref, mask_ref, o_ref, *, n_tiles, seq):
    n = pl.program_id(1)
    dh = SW_HEAD_DIM
    grp = SW_HEADS // SW_KV_HEADS
    win = SW_KWIN
    ks = lambda hd: slice((hd // grp) * dh, (hd // grp + 1) * dh)
    floor_of = lambda hd: sink_ref[hd] * LOG2E

    @pl.when(n < n_tiles)
    def _():
        ws = pl.multiple_of(jnp.clip(n * SW_QTILE - SW_WINDOW, 0, seq - win), LANES)
        _attend(q_ref, o_ref, n_heads=SW_HEADS, dh=dh, lookahead=SW_LOOKAHEAD,
                keys_of=lambda hd: [kl_ref[pl.ds(ws, win), ks(hd)], kc_ref[:, ks(hd)]],
                values_of=lambda hd: [vlt_ref[hd // grp, :, pl.ds(ws, win)], vct_ref[hd // grp]],
                addends_of=lambda hd: [[mask_ref[...]], []], floor_of=floor_of)

    @pl.when(n >= n_tiles)
    def _():
        _attend(q_ref, o_ref, n_heads=SW_HEADS, dh=dh, lookahead=SW_LOOKAHEAD,
                keys_of=lambda hd: [kc_ref[:, ks(hd)]],
                values_of=lambda hd: [vct_ref[hd // grp]],
                addends_of=lambda hd: [[]], floor_of=floor_of)


def _sw_attention(proj, vt, sink, *, n_batch, seq, ctx_len, with_ctx_out):
    qd = SW_HEADS * SW_HEAD_DIM
    tq = SW_QTILE
    win = SW_KWIN
    n_tiles = seq // tq
    ctx_tiles = ctx_len // tq
    n_ctx_tiles = ctx_tiles if with_ctx_out else 0
    lat_tiles = n_batch * n_tiles
    n_lat = n_batch * seq
    n_rows = (n_lat + n_batch * ctx_len) if with_ctx_out else n_lat
    k_col = qd // LANES
    ctx_blk0 = n_lat // ctx_len
    vrows = SW_HEAD_DIM + ONES_ROWS
    ki = np.arange(win)[:, None]
    qj = np.arange(tq)[None, :]
    mask = np.stack([np.where(np.abs(ki - qj - off) <= SW_WINDOW, 0.0, NEG) for off in (0, SW_WINDOW, 2 * SW_WINDOW)])
    mask = jnp.asarray(mask, F32)

    def q_idx(b, n, s):
        return (jnp.where(n < n_tiles, b * n_tiles + n, lat_tiles + b * ctx_tiles + (n - n_tiles)), 0)

    def mask_idx(b, n, s):
        nn = jnp.minimum(n, n_tiles - 1)
        ws_blk = jnp.clip(nn * (tq // SW_WINDOW) - 1, 0, (seq - win) // SW_WINDOW)
        return (nn * (tq // SW_WINDOW) - ws_blk, 0, 0)

    grid_spec = pltpu.PrefetchScalarGridSpec(
        num_scalar_prefetch=1,
        grid=(n_batch, n_tiles + n_ctx_tiles),
        in_specs=[
            pl.BlockSpec((tq, qd), q_idx),
            pl.BlockSpec((seq, LANES), lambda b, n, s: (b, k_col)),
            pl.BlockSpec((ctx_len, LANES), lambda b, n, s: (ctx_blk0 + b, k_col)),
            pl.BlockSpec((SW_KV_HEADS, vrows, seq), lambda b, n, s: (0, 0, b)),
            pl.BlockSpec((SW_KV_HEADS, vrows, ctx_len), lambda b, n, s: (0, 0, ctx_blk0 + b)),
            pl.BlockSpec((None, win, tq), mask_idx),
        ],
        out_specs=pl.BlockSpec((tq, qd), q_idx),
    )
    return pl.pallas_call(
        functools.partial(_sw_kernel, n_tiles=n_tiles, seq=seq),
        out_shape=jax.ShapeDtypeStruct((n_rows, qd), BF16),
        grid_spec=grid_spec,
        compiler_params=_params(2),
        name="sw_attention",
    )(sink.astype(F32), proj, proj, proj, vt, vt, mask)


def _ga_kernel(q_ref, kl_ref, kc_ref, vlt_ref, vct_ref, o_ref):
    dh = GA_HEAD_DIM
    grp = GA_HEADS // GA_KV_HEADS
    ks = lambda hd: slice((hd // grp) * dh, (hd // grp + 1) * dh)
    _attend(q_ref, o_ref, n_heads=GA_HEADS, dh=dh, lookahead=GA_LOOKAHEAD,
            keys_of=lambda hd: [kl_ref[:, ks(hd)], kc_ref[:, ks(hd)]],
            values_of=lambda hd: [vlt_ref[hd // grp], vct_ref[hd // grp]],
            addends_of=lambda hd: [[], []])


def _ga_attention(proj, vt, *, n_batch, seq, ctx_len):
    qd = GA_HEADS * GA_HEAD_DIM
    kvd = GA_KV_HEADS * GA_HEAD_DIM
    tq = GA_QTILE
    n_tiles = seq // tq
    n_lat = n_batch * seq
    k_col = qd // kvd
    ctx_blk0 = n_lat // ctx_len
    vrows = GA_HEAD_DIM + ONES_ROWS
    return pl.pallas_call(
        _ga_kernel,
        out_shape=jax.ShapeDtypeStruct((n_lat, qd), BF16),
        grid=(n_batch, n_tiles),
        in_specs=[
            pl.BlockSpec((tq, qd), lambda b, n: (b * n_tiles + n, 0)),
            pl.BlockSpec((seq, kvd), lambda b, n: (b, k_col)),
            pl.BlockSpec((ctx_len, kvd), lambda b, n: (ctx_blk0 + b, k_col)),
            pl.BlockSpec((GA_KV_HEADS, vrows, seq), lambda b, n: (0, 0, b)),
            pl.BlockSpec((GA_KV_HEADS, vrows, ctx_len), lambda b, n: (0, 0, ctx_blk0 + b)),
        ],
        out_specs=pl.BlockSpec((tq, qd), lambda b, n: (b * n_tiles + n, 0)),
        compiler_params=_params(2),
        name="ga_attention",
    )(proj, proj, proj, vt, vt)


def _conv_kernel(um_ref, up_ref, un_ref, w_ref, b_ref, lg_ref, lb_ref, o_ref, win_ref, sh_ref, acc_ref, *,
                 lat_tiles, ctx_tiles):
    j = pl.program_id(1)
    jc = j - lat_tiles
    first = jnp.where(j < lat_tiles, j == 0, jc == 0)
    last = jnp.where(j < lat_tiles, j == lat_tiles - 1, jc == ctx_tiles - 1)
    tl = um_ref.shape[0]
    d = um_ref.shape[1]
    sh_rows = sh_ref.shape[1]
    win_ref[0:CONV_HALO, :] = jnp.where(first, 0.0, up_ref[...])
    win_ref[CONV_HALO:CONV_HALO + tl, :] = um_ref[...]
    win_ref[CONV_HALO + tl:, :] = jnp.where(last, 0.0, un_ref[...])
    base = CONV_HALO - CONV_WIDTH // 2

    def lane_slab(s, carry):
        ls = pl.ds(pl.multiple_of(s * LANES, LANES), LANES)
        for r in range(1, SUBLANES):
            sh_ref[r] = win_ref[r:r + sh_rows, ls]
        for rb in range(tl // CONV_ROWS):
            acc = jnp.broadcast_to(b_ref[:, ls], (CONV_ROWS, LANES))
            for k in range(CONV_WIDTH):
                a8, r = divmod(base + k, SUBLANES)
                lo = a8 * SUBLANES + rb * CONV_ROWS
                src = win_ref[lo:lo + CONV_ROWS, ls] if r == 0 else sh_ref[r, lo:lo + CONV_ROWS, :]
                acc = acc + w_ref[k:k + 1, ls] * src
            acc_ref[rb * CONV_ROWS:(rb + 1) * CONV_ROWS, ls] = acc
        return carry

    lax.fori_loop(0, d // LANES, lane_slab, 0)
    u = acc_ref[...]
    mu = jnp.mean(u, axis=-1, keepdims=True)
    var = jnp.mean(jnp.square(u - mu), axis=-1, keepdims=True)
    y = (u - mu) * lax.rsqrt(var + EPS) * lg_ref[...] + lb_ref[...]
    o_ref[...] = _silu(y).astype(BF16)


def _conv_core(u, w_dw, b_dw, ln_g, ln_b, *, n_batch, seq, ctx_len, with_ctx_out):
    d = u.shape[1]
    tl = CONV_TILE
    lat_tiles = seq // tl
    ctx_tiles = ctx_len // tl
    n_ctx = ctx_tiles if with_ctx_out else 0
    n_rows = (n_batch * seq + n_batch * ctx_len) if with_ctx_out else n_batch * seq
    halo_per_tile = tl // CONV_HALO
    n_halo_blocks = u.shape[0] // CONV_HALO
    sh_rows = tl + 2 * CONV_HALO - SUBLANES

    def main(b, j):
        return jnp.where(j < lat_tiles, b * lat_tiles + j, n_batch * lat_tiles + b * ctx_tiles + (j - lat_tiles))

    return pl.pallas_call(
        functools.partial(_conv_kernel, lat_tiles=lat_tiles, ctx_tiles=ctx_tiles),
        out_shape=jax.ShapeDtypeStruct((n_rows, d), BF16),
        grid=(n_batch, lat_tiles + n_ctx),
        in_specs=[
            pl.BlockSpec((tl, d), lambda b, j: (main(b, j), 0)),
            pl.BlockSpec((CONV_HALO, d), lambda b, j: (jnp.maximum(main(b, j) * halo_per_tile - 1, 0), 0)),
            pl.BlockSpec((CONV_HALO, d),
                         lambda b, j: (jnp.minimum((main(b, j) + 1) * halo_per_tile, n_halo_blocks - 1), 0)),
            pl.BlockSpec((CONV_WIDTH, d), lambda b, j: (0, 0)),
            pl.BlockSpec((1, d), lambda b, j: (0, 0)),
            pl.BlockSpec((1, d), lambda b, j: (0, 0)),
            pl.BlockSpec((1, d), lambda b, j: (0, 0)),
        ],
        out_specs=pl.BlockSpec((tl, d), lambda b, j: (main(b, j), 0)),
        scratch_shapes=[pltpu.VMEM((tl + 2 * CONV_HALO, d), F32), pltpu.VMEM((SUBLANES, sh_rows, LANES), F32),
                        pltpu.VMEM((tl, d), F32)],
        compiler_params=_params(2),
        name="conv_core",
    )(u, u, u, w_dw, b_dw.reshape(1, d), ln_g.reshape(1, d), ln_b.reshape(1, d))


def kernel(x, c, ctx, c_ctx, mod_w, mod_b, norm_g, ffn_w_gate, ffn_w_up, ffn_w_down, na_w_qkv, na_w_o, na_rpb, sw_w_qkv, sw_w_o, sw_sink, cv_w_pw1, cv_b_pw1, cv_w_dw, cv_b_dw, cv_ln_g, cv_ln_b, cv_w_pw2, cv_b_pw2, ga_w_qkv, ga_w_o, ga_q_norm, ga_k_norm):
    n_batch, seq, d = x.shape
    ctx_len = ctx.shape[1]
    depth = mod_w.shape[0]
    n_lat = n_batch * seq
    n_ctx = n_batch * ctx_len
    n_all = n_lat + n_ctx
    assert d == D_MODEL and n_batch < MOD_ROWS and seq % CONV_TILE == 0 and ctx_len % CONV_TILE == 0
    assert seq % ctx_len == 0 and seq >= SW_KWIN and seq % SW_QTILE == 0 and ctx_len % SW_QTILE == 0
    assert (seq // GRID_W) % NA_QROWS == 0 and (seq // GRID_W) >= NA_KROWS and seq % GA_QTILE == 0
    tm = _row_tile(n_lat, n_ctx, seq)
    common = dict(tm=tm, seq=seq, n_batch=n_batch)

    c_all = jnp.concatenate([c, c_ctx[None, :], jnp.zeros((MOD_ROWS - n_batch - 1, d), F32)], axis=0)
    mod = _modulation(c_all, mod_w, mod_b).reshape(depth, MOD_ROWS, N_MOD, d)

    wg, wu, wd = ffn_w_gate, ffn_w_up, ffn_w_down
    zero_bias = jnp.zeros((1, d), F32)

    xs = x.reshape(n_lat, d)
    for i in range(depth):
        kind, j = i % N_MIXERS, i // N_MIXERS
        last = i == depth - 1
        ctx_in = not (last and kind == 2)
        ctx_out = not last
        rows_in = n_all if ctx_in else n_lat
        rows_out = n_all if ctx_out else n_lat
        if kind == 0:
            qd = NA_HEADS * NA_HEAD_DIM
            proj = (_na_epilogue, na_w_qkv[j].astype(BF16), [], [],
                    *_proj_outputs(rows_in, tm, 2 * qd, BF16, NA_HEADS, NA_HEAD_DIM))
        elif kind == 1:
            qd, kvd = SW_HEADS * SW_HEAD_DIM, SW_KV_HEADS * SW_HEAD_DIM
            perm = np.concatenate([_half_split_perm(SW_HEADS, SW_HEAD_DIM),
                                   qd + _half_split_perm(SW_KV_HEADS, SW_HEAD_DIM),
                                   np.arange(qd + kvd, qd + 2 * kvd)])
            cos_t, sin_t = _rope_tables(seq, SW_HEAD_DIM, tm)
            proj = (_sw_epilogue, sw_w_qkv[j][:, perm].astype(BF16), [cos_t, sin_t], _rope_specs(tm, seq, n_lat),
                    *_proj_outputs(rows_in, tm, qd + kvd, BF16, SW_KV_HEADS, SW_HEAD_DIM))
        elif kind == 2:
            proj = (_cv_epilogue, cv_w_pw1[j].astype(BF16), [cv_b_pw1[j].reshape(1, -1)],
                    [pl.BlockSpec((1, 2 * d), lambda t: (0, 0))], *_proj_outputs(rows_in, tm, d, F32))
        else:
            assert not ctx_out, "context outputs of the global mixer are not implemented"
            qd, kvd = GA_HEADS * GA_HEAD_DIM, GA_KV_HEADS * GA_HEAD_DIM
            perm = np.concatenate([_half_split_perm(GA_HEADS, GA_HEAD_DIM),
                                   qd + _half_split_perm(GA_KV_HEADS, GA_HEAD_DIM),
                                   np.arange(qd + kvd, qd + 2 * kvd)])
            hperm = _half_split_perm(1, GA_HEAD_DIM)
            cos_t, sin_t = _rope_tables(seq, GA_HEAD_DIM, tm)
            vec = pl.BlockSpec((1, GA_HEAD_DIM), lambda t: (0, 0))
            proj = (_ga_epilogue, ga_w_qkv[j][:, perm].astype(BF16),
                    [cos_t, sin_t, ga_q_norm[j][hperm].reshape(1, -1), ga_k_norm[j][hperm].reshape(1, -1)],
                    _rope_specs(tm, seq, n_lat) + [vec, vec],
                    *_proj_outputs(rows_in, tm, qd + kvd, BF16, GA_KV_HEADS, GA_HEAD_DIM))
        x2 = ctx.reshape(n_ctx, d) if (i == 0 and ctx_in) else None
        xs, *pj = _ffn(xs, mod, norm_g, wg, wu, wd, layer=i, half=0, n_rows=rows_in, **common, x2=x2, proj=proj)
        if kind == 0:
            a = _na_attention(*pj, na_rpb[j], n_batch=n_batch, seq=seq, ctx_len=ctx_len, with_ctx_out=ctx_out)
            w_o, b_o = na_w_o[j], zero_bias
        elif kind == 1:
            a = _sw_attention(*pj, sw_sink[j], n_batch=n_batch, seq=seq, ctx_len=ctx_len, with_ctx_out=ctx_out)
            w_o, b_o = sw_w_o[j], zero_bias
        elif kind == 2:
            a = _conv_core(*pj, cv_w_dw[j], cv_b_dw[j], cv_ln_g[j], cv_ln_b[j], n_batch=n_batch, seq=seq,
                           ctx_len=ctx_len, with_ctx_out=ctx_out)
            w_o, b_o = cv_w_pw2[j], cv_b_pw2[j].reshape(1, d)
        else:
            a = _ga_attention(*pj, n_batch=n_batch, seq=seq, ctx_len=ctx_len)
            w_o, b_o = ga_w_o[j], zero_bias
        xs = _ffn(xs, mod, norm_g, wg, wu, wd, layer=i, half=1, n_rows=rows_out, **common,
                  mixer_out=(a, w_o.astype(BF16), b_o))
    return xs[:n_lat].reshape(n_batch, seq, d)
```

```python
import functools

import numpy as np
import jax
import jax.numpy as jnp
from jax import lax
from jax.experimental import pallas as pl
from jax.experimental.pallas import tpu as pltpu

F32 = jnp.float32
BF16 = jnp.bfloat16

D_MODEL = 1024
DEPTH = 4
GRID_W = 64
N_MIXERS = 4
EPS = 1e-6
NEG = -1e30
LOG2E = 1.4426950408889634
ROPE_THETA = 10000.0
D_FF = 2816
FFN_RESIDUAL_WEIGHT = 0.5
N_MOD = 9
NA_HEADS = 16
NA_HEAD_DIM = 64
NA_WIN_R = 8
NA_WIN_C = 16
NA_QROWS = 4
NA_KROWS = NA_QROWS + NA_WIN_R
SW_HEADS = 16
SW_KV_HEADS = 2
SW_HEAD_DIM = 64
SW_WINDOW = 128
SW_QTILE = 256
SW_KWIN = SW_QTILE + 2 * SW_WINDOW
ONES_ROWS = 16
NA_LOOKAHEAD, SW_LOOKAHEAD, GA_LOOKAHEAD = 2, 3, 2
CONV_WIDTH = 31
GA_HEADS = 8
GA_KV_HEADS = 2
GA_HEAD_DIM = 128
GA_QTILE = 256

LANES = 128
SUBLANES = 8
CONV_HALO = 16
CONV_TILE = 256
CONV_ROWS = 128
W_CHUNKS = 16
W_SLOTS = 6
PROJ_SUB_BLOCKS = 2
MOD_ROWS = 16
VMEM_LIMIT = 60 * 1024 * 1024

_DN_T = (((1,), (1,)), ((), ()))


def _params(n_axes, vmem=VMEM_LIMIT):
    return pltpu.CompilerParams(dimension_semantics=("arbitrary",) * n_axes, vmem_limit_bytes=vmem)


def _resident(block_shape, index_map):
    return pl.BlockSpec(block_shape, index_map, pipeline_mode=pl.Buffered(1))


def _rms(x):
    return x * lax.rsqrt(jnp.mean(x * x, axis=-1, keepdims=True) + EPS)


def _silu(x):
    return x * jax.nn.sigmoid(x)


def _row_tile(n_lat_rows, n_ctx_rows, seq):
    for t in (512, 256, 128):
        if seq % t == 0 and n_ctx_rows % t == 0 and n_lat_rows % t == 0:
            return t
    raise ValueError("unsupported token counts")


def _mod_kernel(c_ref, w_ref, b_ref, o_ref):
    a = _silu(c_ref[...]).astype(BF16)
    o_ref[0] = jnp.dot(a, w_ref[0].astype(BF16), preferred_element_type=F32) + b_ref[0]


def _modulation(c_all, mod_w, mod_b):
    depth, d, n = mod_w.shape
    tn = 1152
    return pl.pallas_call(
        _mod_kernel,
        out_shape=jax.ShapeDtypeStruct((depth, MOD_ROWS, n), F32),
        grid=(depth, n // tn),
        in_specs=[
            pl.BlockSpec((MOD_ROWS, d), lambda i, j: (0, 0)),
            pl.BlockSpec((1, d, tn), lambda i, j: (i, 0, j)),
            pl.BlockSpec((1, 1, tn), lambda i, j: (i, 0, j)),
        ],
        out_specs=pl.BlockSpec((1, MOD_ROWS, tn), lambda i, j: (i, 0, j)),
        compiler_params=_params(2),
        name="modulation",
    )(c_all, mod_w, mod_b.reshape(depth, 1, n))


def _load_weights_bf16(layer, half, streams):
    jobs = []
    ring_pos = {}
    for w_hbm, dst_ref, stage_ref, sem_ref in streams:
        rows = stage_ref.shape[1]
        for c in range(dst_ref.shape[0] // rows):
            slot = ring_pos.get(id(stage_ref), 0) % stage_ref.shape[0]
            ring_pos[id(stage_ref)] = ring_pos.get(id(stage_ref), 0) + 1
            copy = pltpu.make_async_copy(w_hbm.at[layer, half, pl.ds(c * rows, rows), :], stage_ref.at[slot],
                                         sem_ref.at[slot])
            jobs.append((copy, dst_ref, stage_ref, slot, c * rows, rows))
    ahead = W_SLOTS - 1
    for copy, *_ in jobs[:ahead]:
        copy.start()
    for i, (copy, dst_ref, stage_ref, slot, r0, rows) in enumerate(jobs):
        if i + ahead < len(jobs):
            jobs[i + ahead][0].start()
        copy.wait()
        dst_ref[r0:r0 + rows, :] = stage_ref[slot].astype(BF16)


def _ffn_kernel(*refs, layer, half, split, mix, epilogue, n_in, n_out, n_sub):
    ins, outs = iter(refs[:n_in]), refs[n_in:n_in + n_out]
    wg_ref, wu_ref, wd_ref, stage_a, stage_b, sem_a, sem_b = refs[n_in + n_out:]
    slot = 2 * half
    x_ref = next(ins)
    x2_ref = next(ins) if split is not None else None
    if mix:
        a_ref, wo_ref, bo_ref = next(ins), next(ins), next(ins)
    m_ref, g_ref, wg_hbm, wu_hbm, wd_hbm = (next(ins) for _ in range(5))
    if epilogue is not None:
        wq_ref = next(ins)
        extras = list(ins)

    @pl.when(pl.program_id(0) == 0)
    def _():
        _load_weights_bf16(layer, half, [(wg_hbm, wg_ref, stage_a, sem_a), (wu_hbm, wu_ref, stage_a, sem_a),
                                         (wd_hbm, wd_ref, stage_b, sem_b)])

    shift = m_ref[3 * slot:3 * slot + 1, :]
    scale = m_ref[3 * slot + 1:3 * slot + 2, :]
    gate = m_ref[3 * slot + 2:3 * slot + 3, :]
    g_pre = g_ref[2 * slot:2 * slot + 1, :]
    g_post = g_ref[2 * slot + 1:2 * slot + 2, :]
    sub = x_ref.shape[0] // n_sub
    for r in range(n_sub):
        rows = slice(r * sub, (r + 1) * sub)
        x = x_ref[rows, :]
        if split is not None:
            x = jnp.where(pl.program_id(0) < split, x, x2_ref[rows, :])
        if mix:
            ym = jnp.dot(a_ref[rows, :], wo_ref[...], preferred_element_type=F32) + bo_ref[...]
            x = x + m_ref[5:6, :] * (_rms(ym) * g_ref[3:4, :])
        h = (_rms(x) * g_pre * (1.0 + scale) + shift).astype(BF16)
        gt = jnp.dot(h, wg_ref[...], preferred_element_type=F32)
        up = jnp.dot(h, wu_ref[...], preferred_element_type=F32)
        a = (_silu(gt) * up).astype(BF16)
        y = jnp.dot(a, wd_ref[...], preferred_element_type=F32)
        x = x + (FFN_RESIDUAL_WEIGHT * gate) * (_rms(y) * g_post)
        outs[0][rows, :] = x
        if epilogue is not None:
            hq = (_rms(x) * g_ref[2:3, :] * (1.0 + m_ref[4:5, :]) + m_ref[3:4, :]).astype(BF16)
            epilogue(jnp.dot(hq, wq_ref[...], preferred_element_type=F32), extras, outs[1:], rows)


def _group_map(tm, seq, n_batch):
    return lambda i: jnp.minimum((i * tm) // seq, n_batch)


def _ffn(x, mod, norm_g, wg, wu, wd, *, layer, half, n_rows, tm, seq, n_batch, x2=None, mixer_out=None, proj=None):
    d = x.shape[1]
    f = wg.shape[-1]
    slot = 2 * half
    grp = _group_map(tm, seq, n_batch)
    if x2 is None:
        split = None
        xs, x_specs = [x], [pl.BlockSpec((tm, d), lambda i: (i, 0))]
    else:
        split = x.shape[0] // tm
        xs = [x, x2]
        x_specs = [pl.BlockSpec((tm, d), lambda i: (jnp.minimum(i, split - 1), 0)),
                   pl.BlockSpec((tm, d), lambda i: (jnp.maximum(i - split, 0), 0))]
    if mixer_out is not None:
        a, w_o, b_o = mixer_out
        xs += [a, w_o, b_o]
        x_specs += [pl.BlockSpec((tm, a.shape[1]), lambda i: (i, 0)), _resident(w_o.shape, lambda i: (0, 0)),
                    pl.BlockSpec((1, d), lambda i: (0, 0))]
    xs += [mod, norm_g, wg, wu, wd]
    x_specs += [
        pl.BlockSpec((None, None, N_MOD, d), lambda i: (layer, grp(i), 0, 0)),
        pl.BlockSpec((None, 6, d), lambda i: (layer, 0, 0)),
        pl.BlockSpec(memory_space=pl.ANY),
        pl.BlockSpec(memory_space=pl.ANY),
        pl.BlockSpec(memory_space=pl.ANY),
    ]
    scratch = [pltpu.VMEM((d, f), BF16), pltpu.VMEM((d, f), BF16), pltpu.VMEM((f, d), BF16),
               pltpu.VMEM((W_SLOTS, d // W_CHUNKS, f), F32), pltpu.VMEM((W_SLOTS, f // W_CHUNKS, d), F32),
               pltpu.SemaphoreType.DMA((W_SLOTS,)), pltpu.SemaphoreType.DMA((W_SLOTS,))]
    out_shape = [jax.ShapeDtypeStruct((n_rows, d), F32)]
    out_specs = [pl.BlockSpec((tm, d), lambda i: (i, 0))]
    epilogue = None
    n_sub = 1
    if proj is not None:
        epilogue, w_qkv, extras, extra_specs, p_shapes, p_specs = proj
        n_sub = PROJ_SUB_BLOCKS if (tm >= 512 and epilogue is not _na_epilogue) else 1
        xs += [w_qkv] + extras
        x_specs += [_resident(w_qkv.shape, lambda i: (0, 0))] + extra_specs
        out_shape += p_shapes
        out_specs += p_specs
    res = pl.pallas_call(
        functools.partial(_ffn_kernel, layer=layer, half=half, split=split, mix=mixer_out is not None,
                          epilogue=epilogue, n_in=len(xs), n_out=len(out_shape), n_sub=n_sub),
        out_shape=out_shape,
        grid=(n_rows // tm,),
        in_specs=x_specs,
        out_specs=out_specs,
        scratch_shapes=scratch,
        compiler_params=_params(1),
        name=f"ffn_l{layer}h{half}",
    )(*xs)
    return res[0] if proj is None else res


def _rotate_half(x, half):
    if 2 * half == LANES:
        return pltpu.roll(x, half, axis=1)
    lane = lax.broadcasted_iota(jnp.int32, x.shape, 1)
    return jnp.where(lane % (2 * half) < half, pltpu.roll(x, LANES - half, axis=1), pltpu.roll(x, half, axis=1))


def _store_values_transposed(vt_ref, v, dh, rows):
    per = LANES // dh
    for j in range(v.shape[1] // LANES):
        t = v[:, j * LANES:(j + 1) * LANES].T
        for i in range(per):
            vt_ref[j * per + i, 0:dh, rows] = t[i * dh:(i + 1) * dh, :].astype(BF16)
    vt_ref[:, dh:, rows] = jnp.ones((vt_ref.shape[0], ONES_ROWS, v.shape[0]), BF16)


def _na_epilogue(p, extras, outs, rows):
    o_ref, vt_ref = outs
    qd = NA_HEADS * NA_HEAD_DIM
    o_ref[rows, :qd] = (p[:, :qd] * (NA_HEAD_DIM ** -0.5 * LOG2E)).astype(BF16)
    o_ref[rows, qd:] = p[:, qd:2 * qd].astype(BF16)
    _store_values_transposed(vt_ref, p[:, 2 * qd:], NA_HEAD_DIM, rows)


def _sw_epilogue(p, extras, outs, rows):
    cos_ref, sin_ref = extras
    o_ref, vt_ref = outs
    cos = cos_ref[rows, :]
    sin = sin_ref[rows, :]
    n_q = SW_HEADS * SW_HEAD_DIM // LANES
    n_k = SW_KV_HEADS * SW_HEAD_DIM // LANES
    for j in range(n_q + n_k):
        xs = p[:, j * LANES:(j + 1) * LANES]
        r = xs * cos + _rotate_half(xs, SW_HEAD_DIM // 2) * sin
        if j < n_q:
            r = r * (SW_HEAD_DIM ** -0.5 * LOG2E)
        o_ref[rows, j * LANES:(j + 1) * LANES] = r.astype(BF16)
    _store_values_transposed(vt_ref, p[:, (n_q + n_k) * LANES:], SW_HEAD_DIM, rows)


def _ga_epilogue(p, extras, outs, rows):
    cos_ref, sin_ref, qn_ref, kn_ref = extras
    o_ref, vt_ref = outs
    cos = cos_ref[rows, :]
    sin = sin_ref[rows, :]
    for j in range(GA_HEADS + GA_KV_HEADS):
        xs = _rms(p[:, j * LANES:(j + 1) * LANES])
        if j < GA_HEADS:
            xs = xs * (qn_ref[...] * (GA_HEAD_DIM ** -0.5 * LOG2E))
        else:
            xs = xs * kn_ref[...]
        r = xs * cos + _rotate_half(xs, GA_HEAD_DIM // 2) * sin
        o_ref[rows, j * LANES:(j + 1) * LANES] = r.astype(BF16)
    _store_values_transposed(vt_ref, p[:, (GA_HEADS + GA_KV_HEADS) * LANES:], GA_HEAD_DIM, rows)


def _cv_epilogue(p, extras, outs, rows):
    (b_ref,), (o_ref,) = extras, outs
    u = p + b_ref[...]
    d = o_ref.shape[1]
    o_ref[rows, :] = u[:, :d] * jax.nn.sigmoid(u[:, d:])


def _proj_outputs(n_rows, tm, n_out, out_dtype, vt_heads=0, vt_dim=0):
    shapes = [jax.ShapeDtypeStruct((n_rows, n_out), out_dtype)]
    specs = [pl.BlockSpec((tm, n_out), lambda i: (i, 0))]
    if vt_heads:
        vrows = vt_dim + ONES_ROWS
        shapes.append(jax.ShapeDtypeStruct((vt_heads, vrows, n_rows), BF16))
        specs.append(pl.BlockSpec((vt_heads, vrows, tm), lambda i: (0, 0, i)))
    return shapes, specs


def _rope_specs(tm, seq, n_lat_rows):
    n_seq_blocks = seq // tm
    idx = lambda i: (jnp.where(i * tm < n_lat_rows, i % n_seq_blocks, n_seq_blocks), 0)
    return [pl.BlockSpec((tm, LANES), idx), pl.BlockSpec((tm, LANES), idx)]


def _rope_tables(seq, head_dim, tm):
    t = np.arange(seq)
    row = (t // GRID_W).astype(np.float32)
    col = (t % GRID_W).astype(np.float32)
    n_pairs_axis = head_dim // 4
    freq = jnp.asarray(ROPE_THETA, F32) ** (-jnp.arange(n_pairs_axis, dtype=F32) / n_pairs_axis)
    ang = jnp.concatenate([row[:, None] * freq, col[:, None] * freq], axis=-1)
    cos, sin = jnp.cos(ang), jnp.sin(ang)
    reps = LANES // head_dim
    cos_t = jnp.tile(jnp.concatenate([cos, cos], axis=-1), (1, reps))
    sin_t = jnp.tile(jnp.concatenate([-sin, sin], axis=-1), (1, reps))
    cos_t = jnp.concatenate([cos_t, jnp.ones((tm, LANES), F32)], axis=0)
    sin_t = jnp.concatenate([sin_t, jnp.zeros((tm, LANES), F32)], axis=0)
    return cos_t, sin_t


def _half_split_perm(n_heads, head_dim):
    per_head = np.concatenate([np.arange(0, head_dim, 2), np.arange(1, head_dim, 2)])
    return (np.arange(n_heads)[:, None] * head_dim + per_head[None, :]).reshape(-1)


def _attend(q_ref, o_ref, *, n_heads, dh, lookahead, keys_of, values_of, addends_of, floor_of=None):
    per_store = LANES // dh

    def scores(hd):
        qh = q_ref[:, hd * dh:(hd + 1) * dh]
        ss = []
        for kb, adds in zip(keys_of(hd), addends_of(hd)):
            s = lax.dot_general(kb, qh, _DN_T, preferred_element_type=F32)
            for a in adds:
                s = s + a
            ss.append(s)
        return ss

    def finish(hd, ss):
        m = ss[0].max(axis=0, keepdims=True)
        for s in ss[1:]:
            m = jnp.maximum(m, s.max(axis=0, keepdims=True))
        floor = None if floor_of is None else floor_of(hd)
        if floor is not None:
            m = jnp.maximum(m, floor)
        ot = None
        for s, vb in zip(ss, values_of(hd)):
            part = jnp.dot(vb, jnp.exp2(s - m).astype(BF16), preferred_element_type=F32)
            ot = part if ot is None else ot + part
        l = ot[dh:dh + 1, :]
        if floor is not None:
            l = l + jnp.exp2(floor - m)
        return ot[:dh, :] / l

    pending = [scores(hd) for hd in range(min(lookahead, n_heads))]
    outs = []
    for hd in range(n_heads):
        ss = pending.pop(0)
        if hd + lookahead < n_heads:
            pending.append(scores(hd + lookahead))
        outs.append(finish(hd, ss))
        if len(outs) == per_store:
            o2 = outs[0] if per_store == 1 else jnp.concatenate(outs, axis=0)
            outs = []
            o_ref[:, (hd + 1 - per_store) * dh:(hd + 1) * dh] = o2.T.astype(BF16)


def _na_kernel(q_ref, kl_ref, kc_ref, vlt_ref, vct_ref, val_ref, mask_ref, o_ref, *, n_groups, seq):
    g = pl.program_id(1)
    dh = NA_HEAD_DIM
    tq = NA_QROWS * GRID_W
    n_keys = NA_KROWS * GRID_W
    n_edge = NA_WIN_R * GRID_W
    half = (NA_WIN_R // 2) * GRID_W
    hs = lambda hd: slice(hd * dh, (hd + 1) * dh)
    edge = (g == 0) | (g == n_groups - 1)

    @pl.when((g < n_groups) & jnp.logical_not(edge))
    def _():
        off = pl.multiple_of(g * tq - half, tq)
        _attend(q_ref, o_ref, n_heads=NA_HEADS, dh=dh, lookahead=NA_LOOKAHEAD,
                keys_of=lambda hd: [kl_ref[pl.ds(off, n_keys), hs(hd)], kc_ref[:, hs(hd)]],
                values_of=lambda hd: [vlt_ref[hd, :, pl.ds(off, n_keys)], vct_ref[hd]],
                addends_of=lambda hd: [[val_ref[hd], mask_ref[...]], []])

    @pl.when((g < n_groups) & edge)
    def _():
        koff = pl.multiple_of(jnp.where(g == 0, 0, seq - n_edge), tq)
        toff = pl.multiple_of(jnp.where(g == 0, half, 0), tq)
        _attend(q_ref, o_ref, n_heads=NA_HEADS, dh=dh, lookahead=NA_LOOKAHEAD,
                keys_of=lambda hd: [kl_ref[pl.ds(koff, n_edge), hs(hd)], kc_ref[:, hs(hd)]],
                values_of=lambda hd: [vlt_ref[hd, :, pl.ds(koff, n_edge)], vct_ref[hd]],
                addends_of=lambda hd: [[val_ref[hd, pl.ds(toff, n_edge), :], mask_ref[pl.ds(toff, n_edge), :]], []])

    @pl.when(g >= n_groups)
    def _():
        _attend(q_ref, o_ref, n_heads=NA_HEADS, dh=dh, lookahead=NA_LOOKAHEAD,
                keys_of=lambda hd: [kc_ref[:, hs(hd)]],
                values_of=lambda hd: [vct_ref[hd]],
                addends_of=lambda hd: [[]])


def _na_bias_tables(rpb, rows):
    qr = np.arange(NA_QROWS)
    kri = np.arange(NA_KROWS)
    j = np.arange(GRID_W)
    kc = np.arange(GRID_W)
    half = NA_WIN_R // 2
    ridx = np.clip(kri[:, None] - half - qr[None, :] + NA_WIN_R - 1, 0, 2 * NA_WIN_R - 2)
    cidx = np.clip(kc[:, None] - j[None, :] + NA_WIN_C - 1, 0, 2 * NA_WIN_C - 2)
    n_k, n_q = NA_KROWS * GRID_W, NA_QROWS * GRID_W
    n_dc = 2 * NA_WIN_C - 1
    col_hot = cidx[None] == np.arange(n_dc)[:, None, None]
    row_hot = np.eye(NA_QROWS, dtype=bool)
    onehot = (row_hot[:, None, None, :, None] & col_hot[None, :, :, None, :]).astype(np.float32)
    onehot = onehot.reshape(NA_QROWS * n_dc, GRID_W * n_q)
    rows_sel = (rpb.astype(F32)[:, ridx] * LOG2E).reshape(rpb.shape[0] * NA_KROWS, NA_QROWS * n_dc)
    val = jnp.dot(rows_sel, jnp.asarray(onehot), precision=lax.Precision.HIGHEST).reshape(rpb.shape[0], n_k, n_q)
    ws = np.clip(j - NA_WIN_C // 2, 0, GRID_W - NA_WIN_C)
    col_ok = (kc[:, None] >= ws[None, :]) & (kc[:, None] < ws[None, :] + NA_WIN_C)
    masks = []
    for r0 in (0, NA_QROWS, rows - NA_QROWS):
        r = r0 + qr
        rs = np.clip(r - half, 0, rows - NA_WIN_R)
        k_abs = r0 - half + kri
        row_ok = (k_abs[:, None] >= rs[None, :]) & (k_abs[:, None] < rs[None, :] + NA_WIN_R)
        ok = row_ok[:, None, :, None] & col_ok[None, :, None, :]
        masks.append(np.where(ok, 0.0, NEG).reshape(n_k, n_q))
    return val, jnp.asarray(np.stack(masks), F32)


def _na_attention(qk, vt, rpb, *, n_batch, seq, ctx_len, with_ctx_out):
    rows = seq // GRID_W
    qd = NA_HEADS * NA_HEAD_DIM
    tq = NA_QROWS * GRID_W
    n_keys = NA_KROWS * GRID_W
    n_groups = seq // tq
    ctx_tiles = ctx_len // tq
    n_ctx_tiles = ctx_tiles if with_ctx_out else 0
    n_lat = n_batch * seq
    n_rows = (n_lat + n_batch * ctx_len) if with_ctx_out else n_lat
    ctx_blk0 = n_lat // ctx_len
    vrows = NA_HEAD_DIM + ONES_ROWS
    val, mask = _na_bias_tables(rpb, rows)

    def q_idx(b, g):
        return (jnp.where(g < n_groups, b * n_groups + g, n_batch * n_groups + b * ctx_tiles + (g - n_groups)), 0)

    def mask_idx(b, g):
        return (jnp.where(g == 0, 0, jnp.where(g >= n_groups - 1, 2, 1)), 0, 0)

    return pl.pallas_call(
        functools.partial(_na_kernel, n_groups=n_groups, seq=seq),
        out_shape=jax.ShapeDtypeStruct((n_rows, qd), BF16),
        grid=(n_batch, n_groups + n_ctx_tiles),
        in_specs=[
            pl.BlockSpec((tq, qd), q_idx),
            pl.BlockSpec((seq, qd), lambda b, g: (b, 1)),
            pl.BlockSpec((ctx_len, qd), lambda b, g: (ctx_blk0 + b, 1)),
            pl.BlockSpec((NA_HEADS, vrows, seq), lambda b, g: (0, 0, b)),
            pl.BlockSpec((NA_HEADS, vrows, ctx_len), lambda b, g: (0, 0, ctx_blk0 + b)),
            _resident((NA_HEADS, n_keys, tq), lambda b, g: (0, 0, 0)),
            pl.BlockSpec((None, n_keys, tq), mask_idx),
        ],
        out_specs=pl.BlockSpec((tq, qd), q_idx),
        compiler_params=_params(2),
        name="na_attention",
    )(qk, qk, qk, vt, vt, val, mask)


def _sw_kernel(sink_ref, q_ref, kl_ref, kc_ref, vlt_ref, vct_ref, mask_ref, o_ref, *, n_tiles, seq):
    n = pl.program_id(1)
    dh = SW_HEAD_DIM
    grp = SW_HEADS // SW_KV_HEADS
    win = SW_KWIN
    ks = lambda hd: slice((hd // grp) * dh, (hd // grp + 1) * dh)
    floor_of = lambda hd: sink_ref[hd] * LOG2E

    @pl.when(n < n_tiles)
    def _():
        ws = pl.multiple_of(jnp.clip(n * SW_QTILE - SW_WINDOW, 0, seq - win), LANES)
        _attend(q_ref, o_ref, n_heads=SW_HEADS, dh=dh, lookahead=SW_LOOKAHEAD,
                keys_of=lambda hd: [kl_ref[pl.ds(ws, win), ks(hd)], kc_ref[:, ks(hd)]],
                values_of=lambda hd: [vlt_ref[hd // grp, :, pl.ds(ws, win)], vct_ref[hd // grp]],
                addends_of=lambda hd: [[mask_ref[...]], []], floor_of=floor_of)

    @pl.when(n >= n_tiles)
    def _():
        _attend(q_ref, o_ref, n_heads=SW_HEADS, dh=dh, lookahead=SW_LOOKAHEAD,
                keys_of=lambda hd: [kc_ref[:, ks(hd)]],
                values_of=lambda hd: [vct_ref[hd // grp]],
                addends_of=lambda hd: [[]], floor_of=floor_of)


def _sw_attention(proj, vt, sink, *, n_batch, seq, ctx_len, with_ctx_out):
    qd = SW_HEADS * SW_HEAD_DIM
    tq = SW_QTILE
    win = SW_KWIN
    n_tiles = seq // tq
    ctx_tiles = ctx_len // tq
    n_ctx_tiles = ctx_tiles if with_ctx_out else 0
    lat_tiles = n_batch * n_tiles
    n_lat = n_batch * seq
    n_rows = (n_lat + n_batch * ctx_len) if with_ctx_out else n_lat
    k_col = qd // LANES
    ctx_blk0 = n_lat // ctx_len
    vrows = SW_HEAD_DIM + ONES_ROWS
    ki = np.arange(win)[:, None]
    qj = np.arange(tq)[None, :]
    mask = np.stack([np.where(np.abs(ki - qj - off) <= SW_WINDOW, 0.0, NEG) for off in (0, SW_WINDOW, 2 * SW_WINDOW)])
    mask = jnp.asarray(mask, F32)

    def q_idx(b, n, s):
        return (jnp.where(n < n_tiles, b * n_tiles + n, lat_tiles + b * ctx_tiles + (n - n_tiles)), 0)

    def mask_idx(b, n, s):
        nn = jnp.minimum(n, n_tiles - 1)
        ws_blk = jnp.clip(nn * (tq // SW_WINDOW) - 1, 0, (seq - win) // SW_WINDOW)
        return (nn * (tq // SW_WINDOW) - ws_blk, 0, 0)

    grid_spec = pltpu.PrefetchScalarGridSpec(
        num_scalar_prefetch=1,
        grid=(n_batch, n_tiles + n_ctx_tiles),
        in_specs=[
            pl.BlockSpec((tq, qd), q_idx),
            pl.BlockSpec((seq, LANES), lambda b, n, s: (b, k_col)),
            pl.BlockSpec((ctx_len, LANES), lambda b, n, s: (ctx_blk0 + b, k_col)),
            pl.BlockSpec((SW_KV_HEADS, vrows, seq), lambda b, n, s: (0, 0, b)),
            pl.BlockSpec((SW_KV_HEADS, vrows, ctx_len), lambda b, n, s: (0, 0, ctx_blk0 + b)),
            pl.BlockSpec((None, win, tq), mask_idx),
        ],
        out_specs=pl.BlockSpec((tq, qd), q_idx),
    )
    return pl.pallas_call(
        functools.partial(_sw_kernel, n_tiles=n_tiles, seq=seq),
        out_shape=jax.ShapeDtypeStruct((n_rows, qd), BF16),
        grid_spec=grid_spec,
        compiler_params=_params(2),
        name="sw_attention",
    )(sink.astype(F32), proj, proj, proj, vt, vt, mask)


def _ga_kernel(q_ref, kl_ref, kc_ref, vlt_ref, vct_ref, o_ref):
    dh = GA_HEAD_DIM
    grp = GA_HEADS // GA_KV_HEADS
    ks = lambda hd: slice((hd // grp) * dh, (hd // grp + 1) * dh)
    _attend(q_ref, o_ref, n_heads=GA_HEADS, dh=dh, lookahead=GA_LOOKAHEAD,
            keys_of=lambda hd: [kl_ref[:, ks(hd)], kc_ref[:, ks(hd)]],
            values_of=lambda hd: [vlt_ref[hd // grp], vct_ref[hd // grp]],
            addends_of=lambda hd: [[], []])


def _ga_attention(proj, vt, *, n_batch, seq, ctx_len):
    qd = GA_HEADS * GA_HEAD_DIM
    kvd = GA_KV_HEADS * GA_HEAD_DIM
    tq = GA_QTILE
    n_tiles = seq // tq
    n_lat = n_batch * seq
    k_col = qd // kvd
    ctx_blk0 = n_lat // ctx_len
    vrows = GA_HEAD_DIM + ONES_ROWS
    return pl.pallas_call(
        _ga_kernel,
        out_shape=jax.ShapeDtypeStruct((n_lat, qd), BF16),
        grid=(n_batch, n_tiles),
        in_specs=[
            pl.BlockSpec((tq, qd), lambda b, n: (b * n_tiles + n, 0)),
            pl.BlockSpec((seq, kvd), lambda b, n: (b, k_col)),
            pl.BlockSpec((ctx_len, kvd), lambda b, n: (ctx_blk0 + b, k_col)),
            pl.BlockSpec((GA_KV_HEADS, vrows, seq), lambda b, n: (0, 0, b)),
            pl.BlockSpec((GA_KV_HEADS, vrows, ctx_len), lambda b, n: (0, 0, ctx_blk0 + b)),
        ],
        out_specs=pl.BlockSpec((tq, qd), lambda b, n: (b * n_tiles + n, 0)),
        compiler_params=_params(2),
        name="ga_attention",
    )(proj, proj, proj, vt, vt)


def _conv_kernel(um_ref, up_ref, un_ref, w_ref, b_ref, lg_ref, lb_ref, o_ref, win_ref, sh_ref, acc_ref, *,
                 lat_tiles, ctx_tiles):
    j = pl.program_id(1)
    jc = j - lat_tiles
    first = jnp.where(j < lat_tiles, j == 0, jc == 0)
    last = jnp.where(j < lat_tiles, j == lat_tiles - 1, jc == ctx_tiles - 1)
    tl = um_ref.shape[0]
    d = um_ref.shape[1]
    sh_rows = sh_ref.shape[1]
    win_ref[0:CONV_HALO, :] = jnp.where(first, 0.0, up_ref[...])
    win_ref[CONV_HALO:CONV_HALO + tl, :] = um_ref[...]
    win_ref[CONV_HALO + tl:, :] = jnp.where(last, 0.0, un_ref[...])
    base = CONV_HALO - CONV_WIDTH // 2

    def lane_slab(s, carry):
        ls = pl.ds(pl.multiple_of(s * LANES, LANES), LANES)
        for r in range(1, SUBLANES):
            sh_ref[r] = win_ref[r:r + sh_rows, ls]
        for rb in range(tl // CONV_ROWS):
            acc = jnp.broadcast_to(b_ref[:, ls], (CONV_ROWS, LANES))
            for k in range(CONV_WIDTH):
                a8, r = divmod(base + k, SUBLANES)
                lo = a8 * SUBLANES + rb * CONV_ROWS
                src = win_ref[lo:lo + CONV_ROWS, ls] if r == 0 else sh_ref[r, lo:lo + CONV_ROWS, :]
                acc = acc + w_ref[k:k + 1, ls] * src
            acc_ref[rb * CONV_ROWS:(rb + 1) * CONV_ROWS, ls] = acc
        return carry

    lax.fori_loop(0, d // LANES, lane_slab, 0)
    u = acc_ref[...]
    mu = jnp.mean(u, axis=-1, keepdims=True)
    var = jnp.mean(jnp.square(u - mu), axis=-1, keepdims=True)
    y = (u - mu) * lax.rsqrt(var + EPS) * lg_ref[...] + lb_ref[...]
    o_ref[...] = _silu(y).astype(BF16)


def _conv_core(u, w_dw, b_dw, ln_g, ln_b, *, n_batch, seq, ctx_len, with_ctx_out):
    d = u.shape[1]
    tl = CONV_TILE
    lat_tiles = seq // tl
    ctx_tiles = ctx_len // tl
    n_ctx = ctx_tiles if with_ctx_out else 0
    n_rows = (n_batch * seq + n_batch * ctx_len) if with_ctx_out else n_batch * seq
    halo_per_tile = tl // CONV_HALO
    n_halo_blocks = u.shape[0] // CONV_HALO
    sh_rows = tl + 2 * CONV_HALO - SUBLANES

    def main(b, j):
        return jnp.where(j < lat_tiles, b * lat_tiles + j, n_batch * lat_tiles + b * ctx_tiles + (j - lat_tiles))

    return pl.pallas_call(
        functools.partial(_conv_kernel, lat_tiles=lat_tiles, ctx_tiles=ctx_tiles),
        out_shape=jax.ShapeDtypeStruct((n_rows, d), BF16),
        grid=(n_batch, lat_tiles + n_ctx),
        in_specs=[
            pl.BlockSpec((tl, d), lambda b, j: (main(b, j), 0)),
            pl.BlockSpec((CONV_HALO, d), lambda b, j: (jnp.maximum(main(b, j) * halo_per_tile - 1, 0), 0)),
            pl.BlockSpec((CONV_HALO, d),
                         lambda b, j: (jnp.minimum((main(b, j) + 1) * halo_per_tile, n_halo_blocks - 1), 0)),
            pl.BlockSpec((CONV_WIDTH, d), lambda b, j: (0, 0)),
            pl.BlockSpec((1, d), lambda b, j: (0, 0)),
            pl.BlockSpec((1, d), lambda b, j: (0, 0)),
            pl.BlockSpec((1, d), lambda b, j: (0, 0)),
        ],
        out_specs=pl.BlockSpec((tl, d), lambda b, j: (main(b, j), 0)),
        scratch_shapes=[pltpu.VMEM((tl + 2 * CONV_HALO, d), F32), pltpu.VMEM((SUBLANES, sh_rows, LANES), F32),
                        pltpu.VMEM((tl, d), F32)],
        compiler_params=_params(2),
        name="conv_core",
    )(u, u, u, w_dw, b_dw.reshape(1, d), ln_g.reshape(1, d), ln_b.reshape(1, d))


def kernel(x, c, ctx, c_ctx, mod_w, mod_b, norm_g, ffn_w_gate, ffn_w_up, ffn_w_down, na_w_qkv, na_w_o, na_rpb, sw_w_qkv, sw_w_o, sw_sink, cv_w_pw1, cv_b_pw1, cv_w_dw, cv_b_dw, cv_ln_g, cv_ln_b, cv_w_pw2, cv_b_pw2, ga_w_qkv, ga_w_o, ga_q_norm, ga_k_norm):
    n_batch, seq, d = x.shape
    ctx_len = ctx.shape[1]
    depth = mod_w.shape[0]
    n_lat = n_batch * seq
    n_ctx = n_batch * ctx_len
    n_all = n_lat + n_ctx
    assert d == D_MODEL and n_batch < MOD_ROWS and seq % CONV_TILE == 0 and ctx_len % CONV_TILE == 0
    assert seq % ctx_len == 0 and seq >= SW_KWIN and seq % SW_QTILE == 0 and ctx_len % SW_QTILE == 0
    assert (seq // GRID_W) % NA_QROWS == 0 and (seq // GRID_W) >= NA_KROWS and seq % GA_QTILE == 0
    tm = _row_tile(n_lat, n_ctx, seq)
    common = dict(tm=tm, seq=seq, n_batch=n_batch)

    c_all = jnp.concatenate([c, c_ctx[None, :], jnp.zeros((MOD_ROWS - n_batch - 1, d), F32)], axis=0)
    mod = _modulation(c_all, mod_w, mod_b).reshape(depth, MOD_ROWS, N_MOD, d)

    wg, wu, wd = ffn_w_gate, ffn_w_up, ffn_w_down
    zero_bias = jnp.zeros((1, d), F32)

    xs = x.reshape(n_lat, d)
    for i in range(depth):
        kind, j = i % N_MIXERS, i // N_MIXERS
        last = i == depth - 1
        ctx_in = not (last and kind == 2)
        ctx_out = not last
        rows_in = n_all if ctx_in else n_lat
        rows_out = n_all if ctx_out else n_lat
        if kind == 0:
            qd = NA_HEADS * NA_HEAD_DIM
            proj = (_na_epilogue, na_w_qkv[j].astype(BF16), [], [],
                    *_proj_outputs(rows_in, tm, 2 * qd, BF16, NA_HEADS, NA_HEAD_DIM))
        elif kind == 1:
            qd, kvd = SW_HEADS * SW_HEAD_DIM, SW_KV_HEADS * SW_HEAD_DIM
            perm = np.concatenate([_half_split_perm(SW_HEADS, SW_HEAD_DIM),
                                   qd + _half_split_perm(SW_KV_HEADS, SW_HEAD_DIM),
                                   np.arange(qd + kvd, qd + 2 * kvd)])
            cos_t, sin_t = _rope_tables(seq, SW_HEAD_DIM, tm)
            proj = (_sw_epilogue, sw_w_qkv[j][:, perm].astype(BF16), [cos_t, sin_t], _rope_specs(tm, seq, n_lat),
                    *_proj_outputs(rows_in, tm, qd + kvd, BF16, SW_KV_HEADS, SW_HEAD_DIM))
        elif kind == 2:
            proj = (_cv_epilogue, cv_w_pw1[j].astype(BF16), [cv_b_pw1[j].reshape(1, -1)],
                    [pl.BlockSpec((1, 2 * d), lambda t: (0, 0))], *_proj_outputs(rows_in, tm, d, F32))
        else:
            assert not ctx_out, "context outputs of the global mixer are not implemented"
            qd, kvd = GA_HEADS * GA_HEAD_DIM, GA_KV_HEADS * GA_HEAD_DIM
            perm = np.concatenate([_half_split_perm(GA_HEADS, GA_HEAD_DIM),
                                   qd + _half_split_perm(GA_KV_HEADS, GA_HEAD_DIM),
                                   np.arange(qd + kvd, qd + 2 * kvd)])
            hperm = _half_split_perm(1, GA_HEAD_DIM)
            cos_t, sin_t = _rope_tables(seq, GA_HEAD_DIM, tm)
            vec = pl.BlockSpec((1, GA_HEAD_DIM), lambda t: (0, 0))
            proj = (_ga_epilogue, ga_w_qkv[j][:, perm].astype(BF16),
                    [cos_t, sin_t, ga_q_norm[j][hperm].reshape(1, -1), ga_k_norm[j][hperm].reshape(1, -1)],
                    _rope_specs(tm, seq, n_lat) + [vec, vec],
                    *_proj_outputs(rows_in, tm, qd + kvd, BF16, GA_KV_HEADS, GA_HEAD_DIM))
        x2 = ctx.reshape(n_ctx, d) if (i == 0 and ctx_in) else None
        xs, *pj = _ffn(xs, mod, norm_g, wg, wu, wd, layer=i, half=0, n_rows=rows_in, **common, x2=x2, proj=proj)
        if kind == 0:
            a = _na_attention(*pj, na_rpb[j], n_batch=n_batch, seq=seq, ctx_len=ctx_len, with_ctx_out=ctx_out)
            w_o, b_o = na_w_o[j], zero_bias
        elif kind == 1:
            a = _sw_attention(*pj, sw_sink[j], n_batch=n_batch, seq=seq, ctx_len=ctx_len, with_ctx_out=ctx_out)
            w_o, b_o = sw_w_o[j], zero_bias
        elif kind == 2:
            a = _conv_core(*pj, cv_w_dw[j], cv_b_dw[j], cv_ln_g[j], cv_ln_b[j], n_batch=n_batch, seq=seq,
                           ctx_len=ctx_len, with_ctx_out=ctx_out)
            w_o, b_o = cv_w_pw2[j], cv_b_pw2[j].reshape(1, d)
        else:
            a = _ga_attention(*pj, n_batch=n_batch, seq=seq, ctx_len=ctx_len)
            w_o, b_o = ga_w_o[j], zero_bias
        xs = _ffn(xs, mod, norm_g, wg, wu, wd, layer=i, half=1, n_rows=rows_out, **common,
                  mixer_out=(a, w_o.astype(BF16), b_o))
    return xs[:n_lat].reshape(n_batch, seq, d)
```

```python
import functools

import numpy as np
import jax
import jax.numpy as jnp
from jax import lax
from jax.experimental import pallas as pl
from jax.experimental.pallas import tpu as pltpu

F32 = jnp.float32
BF16 = jnp.bfloat16

D_MODEL = 1024
DEPTH = 4
GRID_W = 64
N_MIXERS = 4
EPS = 1e-6
NEG = -1e30
LOG2E = 1.4426950408889634
ROPE_THETA = 10000.0
D_FF = 2816
FFN_RESIDUAL_WEIGHT = 0.5
N_MOD = 9
NA_HEADS = 16
NA_HEAD_DIM = 64
NA_WIN_R = 8
NA_WIN_C = 16
NA_QROWS = 4
NA_KROWS = NA_QROWS + NA_WIN_R
SW_HEADS = 16
SW_KV_HEADS = 2
SW_HEAD_DIM = 64
SW_WINDOW = 128
SW_QTILE = 256
SW_KWIN = SW_QTILE + 2 * SW_WINDOW
ONES_ROWS = 16
NA_LOOKAHEAD, SW_LOOKAHEAD, GA_LOOKAHEAD = 2, 4, 2
CONV_WIDTH = 31
GA_HEADS = 8
GA_KV_HEADS = 2
GA_HEAD_DIM = 128
GA_QTILE = 512

LANES = 128
SUBLANES = 8
CONV_HALO = 16
CONV_TILE = 256
CONV_ROWS = 128
W_CHUNKS = 16
W_SLOTS = 6
PROJ_SUB_BLOCKS = 2
MOD_ROWS = 16
VMEM_LIMIT = 60 * 1024 * 1024

_DN_T = (((1,), (1,)), ((), ()))


def _params(n_axes, vmem=VMEM_LIMIT):
    return pltpu.CompilerParams(dimension_semantics=("arbitrary",) * n_axes, vmem_limit_bytes=vmem)


def _resident(block_shape, index_map):
    return pl.BlockSpec(block_shape, index_map, pipeline_mode=pl.Buffered(1))


def _rms(x):
    return x * lax.rsqrt(jnp.mean(x * x, axis=-1, keepdims=True) + EPS)


def _silu(x):
    return x * jax.nn.sigmoid(x)


def _row_tile(n_lat_rows, n_ctx_rows, seq):
    for t in (512, 256, 128):
        if seq % t == 0 and n_ctx_rows % t == 0 and n_lat_rows % t == 0:
            return t
    raise ValueError("unsupported token counts")


def _mod_kernel(c_ref, w_ref, b_ref, o_ref):
    a = _silu(c_ref[...]).astype(BF16)
    o_ref[0] = jnp.dot(a, w_ref[0].astype(BF16), preferred_element_type=F32) + b_ref[0]


def _modulation(c_all, mod_w, mod_b):
    depth, d, n = mod_w.shape
    tn = 1152
    return pl.pallas_call(
        _mod_kernel,
        out_shape=jax.ShapeDtypeStruct((depth, MOD_ROWS, n), F32),
        grid=(depth, n // tn),
        in_specs=[
            pl.BlockSpec((MOD_ROWS, d), lambda i, j: (0, 0)),
            pl.BlockSpec((1, d, tn), lambda i, j: (i, 0, j)),
            pl.BlockSpec((1, 1, tn), lambda i, j: (i, 0, j)),
        ],
        out_specs=pl.BlockSpec((1, MOD_ROWS, tn), lambda i, j: (i, 0, j)),
        compiler_params=_params(2),
        name="modulation",
    )(c_all, mod_w, mod_b.reshape(depth, 1, n))


def _load_weights_bf16(layer, half, streams):
    jobs = []
    ring_pos = {}
    for w_hbm, dst_ref, stage_ref, sem_ref in streams:
        rows = stage_ref.shape[1]
        for c in range(dst_ref.shape[0] // rows):
            slot = ring_pos.get(id(stage_ref), 0) % stage_ref.shape[0]
            ring_pos[id(stage_ref)] = ring_pos.get(id(stage_ref), 0) + 1
            copy = pltpu.make_async_copy(w_hbm.at[layer, half, pl.ds(c * rows, rows), :], stage_ref.at[slot],
                                         sem_ref.at[slot])
            jobs.append((copy, dst_ref, stage_ref, slot, c * rows, rows))
    ahead = W_SLOTS - 1
    for copy, *_ in jobs[:ahead]:
        copy.start()
    for i, (copy, dst_ref, stage_ref, slot, r0, rows) in enumerate(jobs):
        if i + ahead < len(jobs):
            jobs[i + ahead][0].start()
        copy.wait()
        dst_ref[r0:r0 + rows, :] = stage_ref[slot].astype(BF16)


def _ffn_kernel(*refs, layer, half, split, mix, epilogue, n_in, n_out, n_sub):
    ins, outs = iter(refs[:n_in]), refs[n_in:n_in + n_out]
    wg_ref, wu_ref, wd_ref, stage_a, stage_b, sem_a, sem_b = refs[n_in + n_out:]
    slot = 2 * half
    x_ref = next(ins)
    x2_ref = next(ins) if split is not None else None
    if mix:
        a_ref, wo_ref, bo_ref = next(ins), next(ins), next(ins)
    m_ref, g_ref, wg_hbm, wu_hbm, wd_hbm = (next(ins) for _ in range(5))
    if epilogue is not None:
        wq_ref = next(ins)
        extras = list(ins)

    @pl.when(pl.program_id(0) == 0)
    def _():
        _load_weights_bf16(layer, half, [(wg_hbm, wg_ref, stage_a, sem_a), (wu_hbm, wu_ref, stage_a, sem_a),
                                         (wd_hbm, wd_ref, stage_b, sem_b)])

    shift = m_ref[3 * slot:3 * slot + 1, :]
    scale = m_ref[3 * slot + 1:3 * slot + 2, :]
    gate = m_ref[3 * slot + 2:3 * slot + 3, :]
    g_pre = g_ref[2 * slot:2 * slot + 1, :]
    g_post = g_ref[2 * slot + 1:2 * slot + 2, :]
    sub = x_ref.shape[0] // n_sub
    for r in range(n_sub):
        rows = slice(r * sub, (r + 1) * sub)
        x = x_ref[rows, :]
        if split is not None:
            x = jnp.where(pl.program_id(0) < split, x, x2_ref[rows, :])
        if mix:
            ym = jnp.dot(a_ref[rows, :], wo_ref[...], preferred_element_type=F32) + bo_ref[...]
            x = x + m_ref[5:6, :] * (_rms(ym) * g_ref[3:4, :])
        h = (_rms(x) * g_pre * (1.0 + scale) + shift).astype(BF16)
        gt = jnp.dot(h, wg_ref[...], preferred_element_type=F32)
        up = jnp.dot(h, wu_ref[...], preferred_element_type=F32)
        a = (_silu(gt) * up).astype(BF16)
        y = jnp.dot(a, wd_ref[...], preferred_element_type=F32)
        x = x + (FFN_RESIDUAL_WEIGHT * gate) * (_rms(y) * g_post)
        outs[0][rows, :] = x
        if epilogue is not None:
            hq = (_rms(x) * g_ref[2:3, :] * (1.0 + m_ref[4:5, :]) + m_ref[3:4, :]).astype(BF16)
            epilogue(jnp.dot(hq, wq_ref[...], preferred_element_type=F32), extras, outs[1:], rows)


def _group_map(tm, seq, n_batch):
    return lambda i: jnp.minimum((i * tm) // seq, n_batch)


def _ffn(x, mod, norm_g, wg, wu, wd, *, layer, half, n_rows, tm, seq, n_batch, x2=None, mixer_out=None, proj=None):
    d = x.shape[1]
    f = wg.shape[-1]
    slot = 2 * half
    grp = _group_map(tm, seq, n_batch)
    if x2 is None:
        split = None
        xs, x_specs = [x], [pl.BlockSpec((tm, d), lambda i: (i, 0))]
    else:
        split = x.shape[0] // tm
        xs = [x, x2]
        x_specs = [pl.BlockSpec((tm, d), lambda i: (jnp.minimum(i, split - 1), 0)),
                   pl.BlockSpec((tm, d), lambda i: (jnp.maximum(i - split, 0), 0))]
    if mixer_out is not None:
        a, w_o, b_o = mixer_out
        xs += [a, w_o, b_o]
        x_specs += [pl.BlockSpec((tm, a.shape[1]), lambda i: (i, 0)), _resident(w_o.shape, lambda i: (0, 0)),
                    pl.BlockSpec((1, d), lambda i: (0, 0))]
    xs += [mod, norm_g, wg, wu, wd]
    x_specs += [
        pl.BlockSpec((None, None, N_MOD, d), lambda i: (layer, grp(i), 0, 0)),
        pl.BlockSpec((None, 6, d), lambda i: (layer, 0, 0)),
        pl.BlockSpec(memory_space=pl.ANY),
        pl.BlockSpec(memory_space=pl.ANY),
        pl.BlockSpec(memory_space=pl.ANY),
    ]
    scratch = [pltpu.VMEM((d, f), BF16), pltpu.VMEM((d, f), BF16), pltpu.VMEM((f, d), BF16),
               pltpu.VMEM((W_SLOTS, d // W_CHUNKS, f), F32), pltpu.VMEM((W_SLOTS, f // W_CHUNKS, d), F32),
               pltpu.SemaphoreType.DMA((W_SLOTS,)), pltpu.SemaphoreType.DMA((W_SLOTS,))]
    out_shape = [jax.ShapeDtypeStruct((n_rows, d), F32)]
    out_specs = [pl.BlockSpec((tm, d), lambda i: (i, 0))]
    epilogue = None
    n_sub = 1
    if proj is not None:
        epilogue, w_qkv, extras, extra_specs, p_shapes, p_specs = proj
        n_sub = PROJ_SUB_BLOCKS if (tm >= 512 and epilogue is not _na_epilogue) else 1
        xs += [w_qkv] + extras
        x_specs += [_resident(w_qkv.shape, lambda i: (0, 0))] + extra_specs
        out_shape += p_shapes
        out_specs += p_specs
    res = pl.pallas_call(
        functools.partial(_ffn_kernel, layer=layer, half=half, split=split, mix=mixer_out is not None,
                          epilogue=epilogue, n_in=len(xs), n_out=len(out_shape), n_sub=n_sub),
        out_shape=out_shape,
        grid=(n_rows // tm,),
        in_specs=x_specs,
        out_specs=out_specs,
        scratch_shapes=scratch,
        compiler_params=_params(1),
        name=f"ffn_l{layer}h{half}",
    )(*xs)
    return res[0] if proj is None else res


def _rotate_half(x, half):
    if 2 * half == LANES:
        return pltpu.roll(x, half, axis=1)
    lane = lax.broadcasted_iota(jnp.int32, x.shape, 1)
    return jnp.where(lane % (2 * half) < half, pltpu.roll(x, LANES - half, axis=1), pltpu.roll(x, half, axis=1))


def _store_values_transposed(vt_ref, v, dh, rows):
    per = LANES // dh
    for j in range(v.shape[1] // LANES):
        t = v[:, j * LANES:(j + 1) * LANES].T
        for i in range(per):
            vt_ref[j * per + i, 0:dh, rows] = t[i * dh:(i + 1) * dh, :].astype(BF16)
    vt_ref[:, dh:, rows] = jnp.ones((vt_ref.shape[0], ONES_ROWS, v.shape[0]), BF16)


def _na_epilogue(p, extras, outs, rows):
    o_ref, vt_ref = outs
    qd = NA_HEADS * NA_HEAD_DIM
    o_ref[rows, :qd] = (p[:, :qd] * (NA_HEAD_DIM ** -0.5 * LOG2E)).astype(BF16)
    o_ref[rows, qd:] = p[:, qd:2 * qd].astype(BF16)
    _store_values_transposed(vt_ref, p[:, 2 * qd:], NA_HEAD_DIM, rows)


def _sw_epilogue(p, extras, outs, rows):
    cos_ref, sin_ref = extras
    o_ref, vt_ref = outs
    cos = cos_ref[rows, :]
    sin = sin_ref[rows, :]
    n_q = SW_HEADS * SW_HEAD_DIM // LANES
    n_k = SW_KV_HEADS * SW_HEAD_DIM // LANES
    for j in range(n_q + n_k):
        xs = p[:, j * LANES:(j + 1) * LANES]
        r = xs * cos + _rotate_half(xs, SW_HEAD_DIM // 2) * sin
        if j < n_q:
            r = r * (SW_HEAD_DIM ** -0.5 * LOG2E)
        o_ref[rows, j * LANES:(j + 1) * LANES] = r.astype(BF16)
    _store_values_transposed(vt_ref, p[:, (n_q + n_k) * LANES:], SW_HEAD_DIM, rows)


def _ga_epilogue(p, extras, outs, rows):
    cos_ref, sin_ref, qn_ref, kn_ref = extras
    o_ref, vt_ref = outs
    cos = cos_ref[rows, :]
    sin = sin_ref[rows, :]
    for j in range(GA_HEADS + GA_KV_HEADS):
        xs = _rms(p[:, j * LANES:(j + 1) * LANES])
        if j < GA_HEADS:
            xs = xs * (qn_ref[...] * (GA_HEAD_DIM ** -0.5 * LOG2E))
        else:
            xs = xs * kn_ref[...]
        r = xs * cos + _rotate_half(xs, GA_HEAD_DIM // 2) * sin
        o_ref[rows, j * LANES:(j + 1) * LANES] = r.astype(BF16)
    _store_values_transposed(vt_ref, p[:, (GA_HEADS + GA_KV_HEADS) * LANES:], GA_HEAD_DIM, rows)


def _cv_epilogue(p, extras, outs, rows):
    (b_ref,), (o_ref,) = extras, outs
    u = p + b_ref[...]
    d = o_ref.shape[1]
    o_ref[rows, :] = u[:, :d] * jax.nn.sigmoid(u[:, d:])


def _proj_outputs(n_rows, tm, n_out, out_dtype, vt_heads=0, vt_dim=0):
    shapes = [jax.ShapeDtypeStruct((n_rows, n_out), out_dtype)]
    specs = [pl.BlockSpec((tm, n_out), lambda i: (i, 0))]
    if vt_heads:
        vrows = vt_dim + ONES_ROWS
        shapes.append(jax.ShapeDtypeStruct((vt_heads, vrows, n_rows), BF16))
        specs.append(pl.BlockSpec((vt_heads, vrows, tm), lambda i: (0, 0, i)))
    return shapes, specs


def _rope_specs(tm, seq, n_lat_rows):
    n_seq_blocks = seq // tm
    idx = lambda i: (jnp.where(i * tm < n_lat_rows, i % n_seq_blocks, n_seq_blocks), 0)
    return [pl.BlockSpec((tm, LANES), idx), pl.BlockSpec((tm, LANES), idx)]


def _rope_tables(seq, head_dim, tm):
    t = np.arange(seq)
    row = (t // GRID_W).astype(np.float32)
    col = (t % GRID_W).astype(np.float32)
    n_pairs_axis = head_dim // 4
    freq = jnp.asarray(ROPE_THETA, F32) ** (-jnp.arange(n_pairs_axis, dtype=F32) / n_pairs_axis)
    ang = jnp.concatenate([row[:, None] * freq, col[:, None] * freq], axis=-1)
    cos, sin = jnp.cos(ang), jnp.sin(ang)
    reps = LANES // head_dim
    cos_t = jnp.tile(jnp.concatenate([cos, cos], axis=-1), (1, reps))
    sin_t = jnp.tile(jnp.concatenate([-sin, sin], axis=-1), (1, reps))
    cos_t = jnp.concatenate([cos_t, jnp.ones((tm, LANES), F32)], axis=0)
    sin_t = jnp.concatenate([sin_t, jnp.zeros((tm, LANES), F32)], axis=0)
    return cos_t, sin_t


def _half_split_perm(n_heads, head_dim):
    per_head = np.concatenate([np.arange(0, head_dim, 2), np.arange(1, head_dim, 2)])
    return (np.arange(n_heads)[:, None] * head_dim + per_head[None, :]).reshape(-1)


def _attend(q_ref, o_ref, *, n_heads, dh, lookahead, keys_of, values_of, addends_of, floor_of=None):
    per_store = LANES // dh

    def scores(hd):
        qh = q_ref[:, hd * dh:(hd + 1) * dh]
        ss = []
        for kb, adds in zip(keys_of(hd), addends_of(hd)):
            s = lax.dot_general(kb, qh, _DN_T, preferred_element_type=F32)
            for a in adds:
                s = s + a
            ss.append(s)
        return ss

    def finish(hd, ss):
        m = ss[0].max(axis=0, keepdims=True)
        for s in ss[1:]:
            m = jnp.maximum(m, s.max(axis=0, keepdims=True))
        floor = None if floor_of is None else floor_of(hd)
        if floor is not None:
            m = jnp.maximum(m, floor)
        ot = None
        for s, vb in zip(ss, values_of(hd)):
            part = jnp.dot(vb, jnp.exp2(s - m).astype(BF16), preferred_element_type=F32)
            ot = part if ot is None else ot + part
        l = ot[dh:dh + 1, :]
        if floor is not None:
            l = l + jnp.exp2(floor - m)
        return ot[:dh, :] / l

    pending = [scores(hd) for hd in range(min(lookahead, n_heads))]
    outs = []
    for hd in range(n_heads):
        ss = pending.pop(0)
        if hd + lookahead < n_heads:
            pending.append(scores(hd + lookahead))
        outs.append(finish(hd, ss))
        if len(outs) == per_store:
            o2 = outs[0] if per_store == 1 else jnp.concatenate(outs, axis=0)
            outs = []
            o_ref[:, (hd + 1 - per_store) * dh:(hd + 1) * dh] = o2.T.astype(BF16)


def _na_kernel(q_ref, kl_ref, kc_ref, vlt_ref, vct_ref, val_ref, mask_ref, o_ref, *, n_groups, seq):
    g = pl.program_id(1)
    dh = NA_HEAD_DIM
    tq = NA_QROWS * GRID_W
    n_keys = NA_KROWS * GRID_W
    n_edge = NA_WIN_R * GRID_W
    half = (NA_WIN_R // 2) * GRID_W
    hs = lambda hd: slice(hd * dh, (hd + 1) * dh)
    edge = (g == 0) | (g == n_groups - 1)

    @pl.when((g < n_groups) & jnp.logical_not(edge))
    def _():
        off = pl.multiple_of(g * tq - half, tq)
        _attend(q_ref, o_ref, n_heads=NA_HEADS, dh=dh, lookahead=NA_LOOKAHEAD,
                keys_of=lambda hd: [kl_ref[pl.ds(off, n_keys), hs(hd)], kc_ref[:, hs(hd)]],
                values_of=lambda hd: [vlt_ref[hd, :, pl.ds(off, n_keys)], vct_ref[hd]],
                addends_of=lambda hd: [[val_ref[hd], mask_ref[...]], []])

    @pl.when((g < n_groups) & edge)
    def _():
        koff = pl.multiple_of(jnp.where(g == 0, 0, seq - n_edge), tq)
        toff = pl.multiple_of(jnp.where(g == 0, half, 0), tq)
        _attend(q_ref, o_ref, n_heads=NA_HEADS, dh=dh, lookahead=NA_LOOKAHEAD,
                keys_of=lambda hd: [kl_ref[pl.ds(koff, n_edge), hs(hd)], kc_ref[:, hs(hd)]],
                values_of=lambda hd: [vlt_ref[hd, :, pl.ds(koff, n_edge)], vct_ref[hd]],
                addends_of=lambda hd: [[val_ref[hd, pl.ds(toff, n_edge), :], mask_ref[pl.ds(toff, n_edge), :]], []])

    @pl.when(g >= n_groups)
    def _():
        _attend(q_ref, o_ref, n_heads=NA_HEADS, dh=dh, lookahead=NA_LOOKAHEAD,
                keys_of=lambda hd: [kc_ref[:, hs(hd)]],
                values_of=lambda hd: [vct_ref[hd]],
                addends_of=lambda hd: [[]])


def _na_bias_tables(rpb, rows):
    qr = np.arange(NA_QROWS)
    kri = np.arange(NA_KROWS)
    j = np.arange(GRID_W)
    kc = np.arange(GRID_W)
    half = NA_WIN_R // 2
    ridx = np.clip(kri[:, None] - half - qr[None, :] + NA_WIN_R - 1, 0, 2 * NA_WIN_R - 2)
    cidx = np.clip(kc[:, None] - j[None, :] + NA_WIN_C - 1, 0, 2 * NA_WIN_C - 2)
    n_k, n_q = NA_KROWS * GRID_W, NA_QROWS * GRID_W
    n_dc = 2 * NA_WIN_C - 1
    col_hot = cidx[None] == np.arange(n_dc)[:, None, None]
    row_hot = np.eye(NA_QROWS, dtype=bool)
    onehot = (row_hot[:, None, None, :, None] & col_hot[None, :, :, None, :]).astype(np.float32)
    onehot = onehot.reshape(NA_QROWS * n_dc, GRID_W * n_q)
    rows_sel = (rpb.astype(F32)[:, ridx] * LOG2E).reshape(rpb.shape[0] * NA_KROWS, NA_QROWS * n_dc)
    val = jnp.dot(rows_sel, jnp.asarray(onehot), precision=lax.Precision.HIGHEST).reshape(rpb.shape[0], n_k, n_q)
    ws = np.clip(j - NA_WIN_C // 2, 0, GRID_W - NA_WIN_C)
    col_ok = (kc[:, None] >= ws[None, :]) & (kc[:, None] < ws[None, :] + NA_WIN_C)
    masks = []
    for r0 in (0, NA_QROWS, rows - NA_QROWS):
        r = r0 + qr
        rs = np.clip(r - half, 0, rows - NA_WIN_R)
        k_abs = r0 - half + kri
        row_ok = (k_abs[:, None] >= rs[None, :]) & (k_abs[:, None] < rs[None, :] + NA_WIN_R)
        ok = row_ok[:, None, :, None] & col_ok[None, :, None, :]
        masks.append(np.where(ok, 0.0, NEG).reshape(n_k, n_q))
    return val, jnp.asarray(np.stack(masks), F32)


def _na_attention(qk, vt, rpb, *, n_batch, seq, ctx_len, with_ctx_out):
    rows = seq // GRID_W
    qd = NA_HEADS * NA_HEAD_DIM
    tq = NA_QROWS * GRID_W
    n_keys = NA_KROWS * GRID_W
    n_groups = seq // tq
    ctx_tiles = ctx_len // tq
    n_ctx_tiles = ctx_tiles if with_ctx_out else 0
    n_lat = n_batch * seq
    n_rows = (n_lat + n_batch * ctx_len) if with_ctx_out else n_lat
    ctx_blk0 = n_lat // ctx_len
    vrows = NA_HEAD_DIM + ONES_ROWS
    val, mask = _na_bias_tables(rpb, rows)

    def q_idx(b, g):
        return (jnp.where(g < n_groups, b * n_groups + g, n_batch * n_groups + b * ctx_tiles + (g - n_groups)), 0)

    def mask_idx(b, g):
        return (jnp.where(g == 0, 0, jnp.where(g >= n_groups - 1, 2, 1)), 0, 0)

    return pl.pallas_call(
        functools.partial(_na_kernel, n_groups=n_groups, seq=seq),
        out_shape=jax.ShapeDtypeStruct((n_rows, qd), BF16),
        grid=(n_batch, n_groups + n_ctx_tiles),
        in_specs=[
            pl.BlockSpec((tq, qd), q_idx),
            pl.BlockSpec((seq, qd), lambda b, g: (b, 1)),
            pl.BlockSpec((ctx_len, qd), lambda b, g: (ctx_blk0 + b, 1)),
            pl.BlockSpec((NA_HEADS, vrows, seq), lambda b, g: (0, 0, b)),
            pl.BlockSpec((NA_HEADS, vrows, ctx_len), lambda b, g: (0, 0, ctx_blk0 + b)),
            _resident((NA_HEADS, n_keys, tq), lambda b, g: (0, 0, 0)),
            pl.BlockSpec((None, n_keys, tq), mask_idx),
        ],
        out_specs=pl.BlockSpec((tq, qd), q_idx),
        compiler_params=_params(2),
        name="na_attention",
    )(qk, qk, qk, vt, vt, val, mask)


def _sw_kernel(sink_ref, q_ref, kl_ref, kc_ref, vlt_ref, vct_ref, mask_ref, o_ref, *, n_tiles, seq):
    n = pl.program_id(1)
    dh = SW_HEAD_DIM
    grp = SW_HEADS // SW_KV_HEADS
    win = SW_KWIN
    ks = lambda hd: slice((hd // grp) * dh, (hd // grp + 1) * dh)
    floor_of = lambda hd: sink_ref[hd] * LOG2E

    @pl.when(n < n_tiles)
    def _():
        ws = pl.multiple_of(jnp.clip(n * SW_QTILE - SW_WINDOW, 0, seq - win), LANES)
        _attend(q_ref, o_ref, n_heads=SW_HEADS, dh=dh, lookahead=SW_LOOKAHEAD,
                keys_of=lambda hd: [kl_ref[pl.ds(ws, win), ks(hd)], kc_ref[:, ks(hd)]],
                values_of=lambda hd: [vlt_ref[hd // grp, :, pl.ds(ws, win)], vct_ref[hd // grp]],
                addends_of=lambda hd: [[mask_ref[...]], []], floor_of=floor_of)

    @pl.when(n >= n_tiles)
    def _():
        _attend(q_ref, o_ref, n_heads=SW_HEADS, dh=dh, lookahead=SW_LOOKAHEAD,
                keys_of=lambda hd: [kc_ref[:, ks(hd)]],
                values_of=lambda hd: [vct_ref[hd // grp]],
                addends_of=lambda hd: [[]], floor_of=floor_of)


def _sw_attention(proj, vt, sink, *, n_batch, seq, ctx_len, with_ctx_out):
    qd = SW_HEADS * SW_HEAD_DIM
    tq = SW_QTILE
    win = SW_KWIN
    n_tiles = seq // tq
    ctx_tiles = ctx_len // tq
    n_ctx_tiles = ctx_tiles if with_ctx_out else 0
    lat_tiles = n_batch * n_tiles
    n_lat = n_batch * seq
    n_rows = (n_lat + n_batch * ctx_len) if with_ctx_out else n_lat
    k_col = qd // LANES
    ctx_blk0 = n_lat // ctx_len
    vrows = SW_HEAD_DIM + ONES_ROWS
    ki = np.arange(win)[:, None]
    qj = np.arange(tq)[None, :]
    mask = np.stack([np.where(np.abs(ki - qj - off) <= SW_WINDOW, 0.0, NEG) for off in (0, SW_WINDOW, 2 * SW_WINDOW)])
    mask = jnp.asarray(mask, F32)

    def q_idx(b, n, s):
        return (jnp.where(n < n_tiles, b * n_tiles + n, lat_tiles + b * ctx_tiles + (n - n_tiles)), 0)

    def mask_idx(b, n, s):
        nn = jnp.minimum(n, n_tiles - 1)
        ws_blk = jnp.clip(nn * (tq // SW_WINDOW) - 1, 0, (seq - win) // SW_WINDOW)
        return (nn * (tq // SW_WINDOW) - ws_blk, 0, 0)

    grid_spec = pltpu.PrefetchScalarGridSpec(
        num_scalar_prefetch=1,
        grid=(n_batch, n_tiles + n_ctx_tiles),
        in_specs=[
            pl.BlockSpec((tq, qd), q_idx),
            pl.BlockSpec((seq, LANES), lambda b, n, s: (b, k_col)),
            pl.BlockSpec((ctx_len, LANES), lambda b, n, s: (ctx_blk0 + b, k_col)),
            pl.BlockSpec((SW_KV_HEADS, vrows, seq), lambda b, n, s: (0, 0, b)),
            pl.BlockSpec((SW_KV_HEADS, vrows, ctx_len), lambda b, n, s: (0, 0, ctx_blk0 + b)),
            pl.BlockSpec((None, win, tq), mask_idx),
        ],
        out_specs=pl.BlockSpec((tq, qd), q_idx),
    )
    return pl.pallas_call(
        functools.partial(_sw_kernel, n_tiles=n_tiles, seq=seq),
        out_shape=jax.ShapeDtypeStruct((n_rows, qd), BF16),
        grid_spec=grid_spec,
        compiler_params=_params(2),
        name="sw_attention",
    )(sink.astype(F32), proj, proj, proj, vt, vt, mask)


def _ga_kernel(q_ref, kl_ref, kc_ref, vlt_ref, vct_ref, o_ref):
    dh = GA_HEAD_DIM
    grp = GA_HEADS // GA_KV_HEADS
    ks = lambda hd: slice((hd // grp) * dh, (hd // grp + 1) * dh)
    _attend(q_ref, o_ref, n_heads=GA_HEADS, dh=dh, lookahead=GA_LOOKAHEAD,
            keys_of=lambda hd: [kl_ref[:, ks(hd)], kc_ref[:, ks(hd)]],
            values_of=lambda hd: [vlt_ref[hd // grp], vct_ref[hd // grp]],
            addends_of=lambda hd: [[], []])


def _ga_attention(proj, vt, *, n_batch, seq, ctx_len):
    qd = GA_HEADS * GA_HEAD_DIM
    kvd = GA_KV_HEADS * GA_HEAD_DIM
    tq = GA_QTILE
    n_tiles = seq // tq
    n_lat = n_batch * seq
    k_col = qd // kvd
    ctx_blk0 = n_lat // ctx_len
    vrows = GA_HEAD_DIM + ONES_ROWS
    return pl.pallas_call(
        _ga_kernel,
        out_shape=jax.ShapeDtypeStruct((n_lat, qd), BF16),
        grid=(n_batch, n_tiles),
        in_specs=[
            pl.BlockSpec((tq, qd), lambda b, n: (b * n_tiles + n, 0)),
            pl.BlockSpec((seq, kvd), lambda b, n: (b, k_col)),
            pl.BlockSpec((ctx_len, kvd), lambda b, n: (ctx_blk0 + b, k_col)),
            pl.BlockSpec((GA_KV_HEADS, vrows, seq), lambda b, n: (0, 0, b)),
            pl.BlockSpec((GA_KV_HEADS, vrows, ctx_len), lambda b, n: (0, 0, ctx_blk0 + b)),
        ],
        out_specs=pl.BlockSpec((tq, qd), lambda b, n: (b * n_tiles + n, 0)),
        compiler_params=_params(2),
        name="ga_attention",
    )(proj, proj, proj, vt, vt)


def _conv_kernel(um_ref, up_ref, un_ref, w_ref, b_ref, lg_ref, lb_ref, o_ref, win_ref, sh_ref, acc_ref, *,
                 lat_tiles, ctx_tiles):
    j = pl.program_id(1)
    jc = j - lat_tiles
    first = jnp.where(j < lat_tiles, j == 0, jc == 0)
    last = jnp.where(j < lat_tiles, j == lat_tiles - 1, jc == ctx_tiles - 1)
    tl = um_ref.shape[0]
    d = um_ref.shape[1]
    sh_rows = sh_ref.shape[1]
    win_ref[0:CONV_HALO, :] = jnp.where(first, 0.0, up_ref[...])
    win_ref[CONV_HALO:CONV_HALO + tl, :] = um_ref[...]
    win_ref[CONV_HALO + tl:, :] = jnp.where(last, 0.0, un_ref[...])
    base = CONV_HALO - CONV_WIDTH // 2

    def lane_slab(s, carry):
        ls = pl.ds(pl.multiple_of(s * LANES, LANES), LANES)
        for r in range(1, SUBLANES):
            sh_ref[r] = win_ref[r:r + sh_rows, ls]
        for rb in range(tl // CONV_ROWS):
            acc = jnp.broadcast_to(b_ref[:, ls], (CONV_ROWS, LANES))
            for k in range(CONV_WIDTH):
                a8, r = divmod(base + k, SUBLANES)
                lo = a8 * SUBLANES + rb * CONV_ROWS
                src = win_ref[lo:lo + CONV_ROWS, ls] if r == 0 else sh_ref[r, lo:lo + CONV_ROWS, :]
                acc = acc + w_ref[k:k + 1, ls] * src
            acc_ref[rb * CONV_ROWS:(rb + 1) * CONV_ROWS, ls] = acc
        return carry

    lax.fori_loop(0, d // LANES, lane_slab, 0)
    u = acc_ref[...]
    mu = jnp.mean(u, axis=-1, keepdims=True)
    var = jnp.mean(jnp.square(u - mu), axis=-1, keepdims=True)
    y = (u - mu) * lax.rsqrt(var + EPS) * lg_ref[...] + lb_ref[...]
    o_ref[...] = _silu(y).astype(BF16)


def _conv_core(u, w_dw, b_dw, ln_g, ln_b, *, n_batch, seq, ctx_len, with_ctx_out):
    d = u.shape[1]
    tl = CONV_TILE
    lat_tiles = seq // tl
    ctx_tiles = ctx_len // tl
    n_ctx = ctx_tiles if with_ctx_out else 0
    n_rows = (n_batch * seq + n_batch * ctx_len) if with_ctx_out else n_batch * seq
    halo_per_tile = tl // CONV_HALO
    n_halo_blocks = u.shape[0] // CONV_HALO
    sh_rows = tl + 2 * CONV_HALO - SUBLANES

    def main(b, j):
        return jnp.where(j < lat_tiles, b * lat_tiles + j, n_batch * lat_tiles + b * ctx_tiles + (j - lat_tiles))

    return pl.pallas_call(
        functools.partial(_conv_kernel, lat_tiles=lat_tiles, ctx_tiles=ctx_tiles),
        out_shape=jax.ShapeDtypeStruct((n_rows, d), BF16),
        grid=(n_batch, lat_tiles + n_ctx),
        in_specs=[
            pl.BlockSpec((tl, d), lambda b, j: (main(b, j), 0)),
            pl.BlockSpec((CONV_HALO, d), lambda b, j: (jnp.maximum(main(b, j) * halo_per_tile - 1, 0), 0)),
            pl.BlockSpec((CONV_HALO, d),
                         lambda b, j: (jnp.minimum((main(b, j) + 1) * halo_per_tile, n_halo_blocks - 1), 0)),
            pl.BlockSpec((CONV_WIDTH, d), lambda b, j: (0, 0)),
            pl.BlockSpec((1, d), lambda b, j: (0, 0)),
            pl.BlockSpec((1, d), lambda b, j: (0, 0)),
            pl.BlockSpec((1, d), lambda b, j: (0, 0)),
        ],
        out_specs=pl.BlockSpec((tl, d), lambda b, j: (main(b, j), 0)),
        scratch_shapes=[pltpu.VMEM((tl + 2 * CONV_HALO, d), F32), pltpu.VMEM((SUBLANES, sh_rows, LANES), F32),
                        pltpu.VMEM((tl, d), F32)],
        compiler_params=_params(2),
        name="conv_core",
    )(u, u, u, w_dw, b_dw.reshape(1, d), ln_g.reshape(1, d), ln_b.reshape(1, d))


def kernel(x, c, ctx, c_ctx, mod_w, mod_b, norm_g, ffn_w_gate, ffn_w_up, ffn_w_down, na_w_qkv, na_w_o, na_rpb, sw_w_qkv, sw_w_o, sw_sink, cv_w_pw1, cv_b_pw1, cv_w_dw, cv_b_dw, cv_ln_g, cv_ln_b, cv_w_pw2, cv_b_pw2, ga_w_qkv, ga_w_o, ga_q_norm, ga_k_norm):
    n_batch, seq, d = x.shape
    ctx_len = ctx.shape[1]
    depth = mod_w.shape[0]
    n_lat = n_batch * seq
    n_ctx = n_batch * ctx_len
    n_all = n_lat + n_ctx
    assert d == D_MODEL and n_batch < MOD_ROWS and seq % CONV_TILE == 0 and ctx_len % CONV_TILE == 0
    assert seq % ctx_len == 0 and seq >= SW_KWIN and seq % SW_QTILE == 0 and ctx_len % SW_QTILE == 0
    assert (seq // GRID_W) % NA_QROWS == 0 and (seq // GRID_W) >= NA_KROWS and seq % GA_QTILE == 0
    tm = _row_tile(n_lat, n_ctx, seq)
    common = dict(tm=tm, seq=seq, n_batch=n_batch)

    c_all = jnp.concatenate([c, c_ctx[None, :], jnp.zeros((MOD_ROWS - n_batch - 1, d), F32)], axis=0)
    mod = _modulation(c_all, mod_w, mod_b).reshape(depth, MOD_ROWS, N_MOD, d)

    wg, wu, wd = ffn_w_gate, ffn_w_up, ffn_w_down
    zero_bias = jnp.zeros((1, d), F32)

    xs = x.reshape(n_lat, d)
    for i in range(depth):
        kind, j = i % N_MIXERS, i // N_MIXERS
        last = i == depth - 1
        ctx_in = not (last and kind == 2)
        ctx_out = not last
        rows_in = n_all if ctx_in else n_lat
        rows_out = n_all if ctx_out else n_lat
        if kind == 0:
            qd = NA_HEADS * NA_HEAD_DIM
            proj = (_na_epilogue, na_w_qkv[j].astype(BF16), [], [],
                    *_proj_outputs(rows_in, tm, 2 * qd, BF16, NA_HEADS, NA_HEAD_DIM))
        elif kind == 1:
            qd, kvd = SW_HEADS * SW_HEAD_DIM, SW_KV_HEADS * SW_HEAD_DIM
            perm = np.concatenate([_half_split_perm(SW_HEADS, SW_HEAD_DIM),
                                   qd + _half_split_perm(SW_KV_HEADS, SW_HEAD_DIM),
                                   np.arange(qd + kvd, qd + 2 * kvd)])
            cos_t, sin_t = _rope_tables(seq, SW_HEAD_DIM, tm)
            proj = (_sw_epilogue, sw_w_qkv[j][:, perm].astype(BF16), [cos_t, sin_t], _rope_specs(tm, seq, n_lat),
                    *_proj_outputs(rows_in, tm, qd + kvd, BF16, SW_KV_HEADS, SW_HEAD_DIM))
        elif kind == 2:
            proj = (_cv_epilogue, cv_w_pw1[j].astype(BF16), [cv_b_pw1[j].reshape(1, -1)],
                    [pl.BlockSpec((1, 2 * d), lambda t: (0, 0))], *_proj_outputs(rows_in, tm, d, F32))
        else:
            assert not ctx_out, "context outputs of the global mixer are not implemented"
            qd, kvd = GA_HEADS * GA_HEAD_DIM, GA_KV_HEADS * GA_HEAD_DIM
            perm = np.concatenate([_half_split_perm(GA_HEADS, GA_HEAD_DIM),
                                   qd + _half_split_perm(GA_KV_HEADS, GA_HEAD_DIM),
                                   np.arange(qd + kvd, qd + 2 * kvd)])
            hperm = _half_split_perm(1, GA_HEAD_DIM)
            cos_t, sin_t = _rope_tables(seq, GA_HEAD_DIM, tm)
            vec = pl.BlockSpec((1, GA_HEAD_DIM), lambda t: (0, 0))
            proj = (_ga_epilogue, ga_w_qkv[j][:, perm].astype(BF16),
                    [cos_t, sin_t, ga_q_norm[j][hperm].reshape(1, -1), ga_k_norm[j][hperm].reshape(1, -1)],
                    _rope_specs(tm, seq, n_lat) + [vec, vec],
                    *_proj_outputs(rows_in, tm, qd + kvd, BF16, GA_KV_HEADS, GA_HEAD_DIM))
        x2 = ctx.reshape(n_ctx, d) if (i == 0 and ctx_in) else None
        xs, *pj = _ffn(xs, mod, norm_g, wg, wu, wd, layer=i, half=0, n_rows=rows_in, **common, x2=x2, proj=proj)
        if kind == 0:
            a = _na_attention(*pj, na_rpb[j], n_batch=n_batch, seq=seq, ctx_len=ctx_len, with_ctx_out=ctx_out)
            w_o, b_o = na_w_o[j], zero_bias
        elif kind == 1:
            a = _sw_attention(*pj, sw_sink[j], n_batch=n_batch, seq=seq, ctx_len=ctx_len, with_ctx_out=ctx_out)
            w_o, b_o = sw_w_o[j], zero_bias
        elif kind == 2:
            a = _conv_core(*pj, cv_w_dw[j], cv_b_dw[j], cv_ln_g[j], cv_ln_b[j], n_batch=n_batch, seq=seq,
                           ctx_len=ctx_len, with_ctx_out=ctx_out)
            w_o, b_o = cv_w_pw2[j], cv_b_pw2[j].reshape(1, d)
        else:
            a = _ga_attention(*pj, n_batch=n_batch, seq=seq, ctx_len=ctx_len)
            w_o, b_o = ga_w_o[j], zero_bias
        xs = _ffn(xs, mod, norm_g, wg, wu, wd, layer=i, half=1, n_rows=rows_out, **common,
                  mixer_out=(a, w_o.astype(BF16), b_o))
    return xs[:n_lat].reshape(n_batch, seq, d)
```

```python
import functools

import numpy as np
import jax
import jax.numpy as jnp
from jax import lax
from jax.experimental import pallas as pl
from jax.experimental.pallas import tpu as pltpu

F32 = jnp.float32
BF16 = jnp.bfloat16

D_MODEL = 1024
DEPTH = 4
GRID_W = 64
N_MIXERS = 4
EPS = 1e-6
NEG = -1e30
LOG2E = 1.4426950408889634
ROPE_THETA = 10000.0
D_FF = 2816
FFN_RESIDUAL_WEIGHT = 0.5
N_MOD = 9
NA_HEADS = 16
NA_HEAD_DIM = 64
NA_WIN_R = 8
NA_WIN_C = 16
NA_QROWS = 4
NA_KROWS = NA_QROWS + NA_WIN_R
SW_HEADS = 16
SW_KV_HEADS = 2
SW_HEAD_DIM = 64
SW_WINDOW = 128
SW_QTILE = 256
SW_KWIN = SW_QTILE + 2 * SW_WINDOW
ONES_ROWS = 16
NA_LOOKAHEAD, SW_LOOKAHEAD, GA_LOOKAHEAD = 2, 4, 2
CONV_WIDTH = 31
GA_HEADS = 8
GA_KV_HEADS = 2
GA_HEAD_DIM = 128
GA_QTILE = 512

LANES = 128
SUBLANES = 8
CONV_HALO = 16
CONV_TILE = 256
CONV_ROWS = 128
W_CHUNKS = 16
W_SLOTS = 6
PROJ_SUB_BLOCKS = 2
MOD_ROWS = 16
VMEM_LIMIT = 60 * 1024 * 1024

_DN_T = (((1,), (1,)), ((), ()))


def _params(n_axes, vmem=VMEM_LIMIT):
    return pltpu.CompilerParams(dimension_semantics=("arbitrary",) * n_axes, vmem_limit_bytes=vmem)


def _resident(block_shape, index_map):
    return pl.BlockSpec(block_shape, index_map, pipeline_mode=pl.Buffered(1))


def _rms(x):
    return x * lax.rsqrt(jnp.mean(x * x, axis=-1, keepdims=True) + EPS)


def _silu(x):
    return x * jax.nn.sigmoid(x)


def _row_tile(n_lat_rows, n_ctx_rows, seq):
    for t in (512, 256, 128):
        if seq % t == 0 and n_ctx_rows % t == 0 and n_lat_rows % t == 0:
            return t
    raise ValueError("unsupported token counts")


def _mod_kernel(c_ref, w_ref, b_ref, o_ref):
    a = _silu(c_ref[...]).astype(BF16)
    o_ref[0] = jnp.dot(a, w_ref[0].astype(BF16), preferred_element_type=F32) + b_ref[0]


def _modulation(c_all, mod_w, mod_b):
    depth, d, n = mod_w.shape
    tn = 1152
    return pl.pallas_call(
        _mod_kernel,
        out_shape=jax.ShapeDtypeStruct((depth, MOD_ROWS, n), F32),
        grid=(depth, n // tn),
        in_specs=[
            pl.BlockSpec((MOD_ROWS, d), lambda i, j: (0, 0)),
            pl.BlockSpec((1, d, tn), lambda i, j: (i, 0, j)),
            pl.BlockSpec((1, 1, tn), lambda i, j: (i, 0, j)),
        ],
        out_specs=pl.BlockSpec((1, MOD_ROWS, tn), lambda i, j: (i, 0, j)),
        compiler_params=_params(2),
        name="modulation",
    )(c_all, mod_w, mod_b.reshape(depth, 1, n))


def _load_weights_bf16(layer, half, streams):
    jobs = []
    ring_pos = {}
    for w_hbm, dst_ref, stage_ref, sem_ref in streams:
        rows = stage_ref.shape[1]
        for c in range(dst_ref.shape[0] // rows):
            slot = ring_pos.get(id(stage_ref), 0) % stage_ref.shape[0]
            ring_pos[id(stage_ref)] = ring_pos.get(id(stage_ref), 0) + 1
            copy = pltpu.make_async_copy(w_hbm.at[layer, half, pl.ds(c * rows, rows), :], stage_ref.at[slot],
                                         sem_ref.at[slot])
            jobs.append((copy, dst_ref, stage_ref, slot, c * rows, rows))
    ahead = W_SLOTS - 1
    for copy, *_ in jobs[:ahead]:
        copy.start()
    for i, (copy, dst_ref, stage_ref, slot, r0, rows) in enumerate(jobs):
        if i + ahead < len(jobs):
            jobs[i + ahead][0].start()
        copy.wait()
        dst_ref[r0:r0 + rows, :] = stage_ref[slot].astype(BF16)


def _ffn_kernel(*refs, layer, half, split, mix, epilogue, n_in, n_out, n_sub):
    ins, outs = iter(refs[:n_in]), refs[n_in:n_in + n_out]
    wg_ref, wu_ref, wd_ref, stage_a, stage_b, sem_a, sem_b = refs[n_in + n_out:]
    slot = 2 * half
    x_ref = next(ins)
    x2_ref = next(ins) if split is not None else None
    if mix:
        a_ref, wo_ref, bo_ref = next(ins), next(ins), next(ins)
    m_ref, g_ref, wg_hbm, wu_hbm, wd_hbm = (next(ins) for _ in range(5))
    if epilogue is not None:
        wq_ref = next(ins)
        extras = list(ins)

    @pl.when(pl.program_id(0) == 0)
    def _():
        _load_weights_bf16(layer, half, [(wg_hbm, wg_ref, stage_a, sem_a), (wu_hbm, wu_ref, stage_a, sem_a),
                                         (wd_hbm, wd_ref, stage_b, sem_b)])

    shift = m_ref[3 * slot:3 * slot + 1, :]
    scale = m_ref[3 * slot + 1:3 * slot + 2, :]
    gate = m_ref[3 * slot + 2:3 * slot + 3, :]
    g_pre = g_ref[2 * slot:2 * slot + 1, :]
    g_post = g_ref[2 * slot + 1:2 * slot + 2, :]
    sub = x_ref.shape[0] // n_sub
    for r in range(n_sub):
        rows = slice(r * sub, (r + 1) * sub)
        x = x_ref[rows, :]
        if split is not None:
            x = jnp.where(pl.program_id(0) < split, x, x2_ref[rows, :])
        if mix:
            ym = jnp.dot(a_ref[rows, :], wo_ref[...], preferred_element_type=F32) + bo_ref[...]
            x = x + m_ref[5:6, :] * (_rms(ym) * g_ref[3:4, :])
        h = (_rms(x) * g_pre * (1.0 + scale) + shift).astype(BF16)
        gt = jnp.dot(h, wg_ref[...], preferred_element_type=F32)
        up = jnp.dot(h, wu_ref[...], preferred_element_type=F32)
        a = (_silu(gt) * up).astype(BF16)
        y = jnp.dot(a, wd_ref[...], preferred_element_type=F32)
        x = x + (FFN_RESIDUAL_WEIGHT * gate) * (_rms(y) * g_post)
        outs[0][rows, :] = x
        if epilogue is not None:
            hq = (_rms(x) * g_ref[2:3, :] * (1.0 + m_ref[4:5, :]) + m_ref[3:4, :]).astype(BF16)
            epilogue(jnp.dot(hq, wq_ref[...], preferred_element_type=F32), extras, outs[1:], rows)


def _group_map(tm, seq, n_batch):
    return lambda i: jnp.minimum((i * tm) // seq, n_batch)


def _ffn(x, mod, norm_g, wg, wu, wd, *, layer, half, n_rows, tm, seq, n_batch, x2=None, mixer_out=None, proj=None):
    d = x.shape[1]
    f = wg.shape[-1]
    slot = 2 * half
    grp = _group_map(tm, seq, n_batch)
    if x2 is None:
        split = None
        xs, x_specs = [x], [pl.BlockSpec((tm, d), lambda i: (i, 0))]
    else:
        split = x.shape[0] // tm
        xs = [x, x2]
        x_specs = [pl.BlockSpec((tm, d), lambda i: (jnp.minimum(i, split - 1), 0)),
                   pl.BlockSpec((tm, d), lambda i: (jnp.maximum(i - split, 0), 0))]
    if mixer_out is not None:
        a, w_o, b_o = mixer_out
        xs += [a, w_o, b_o]
        x_specs += [pl.BlockSpec((tm, a.shape[1]), lambda i: (i, 0)), _resident(w_o.shape, lambda i: (0, 0)),
                    pl.BlockSpec((1, d), lambda i: (0, 0))]
    xs += [mod, norm_g, wg, wu, wd]
    x_specs += [
        pl.BlockSpec((None, None, N_MOD, d), lambda i: (layer, grp(i), 0, 0)),
        pl.BlockSpec((None, 6, d), lambda i: (layer, 0, 0)),
        pl.BlockSpec(memory_space=pl.ANY),
        pl.BlockSpec(memory_space=pl.ANY),
        pl.BlockSpec(memory_space=pl.ANY),
    ]
    scratch = [pltpu.VMEM((d, f), BF16), pltpu.VMEM((d, f), BF16), pltpu.VMEM((f, d), BF16),
               pltpu.VMEM((W_SLOTS, d // W_CHUNKS, f), F32), pltpu.VMEM((W_SLOTS, f // W_CHUNKS, d), F32),
               pltpu.SemaphoreType.DMA((W_SLOTS,)), pltpu.SemaphoreType.DMA((W_SLOTS,))]
    out_shape = [jax.ShapeDtypeStruct((n_rows, d), F32)]
    out_specs = [pl.BlockSpec((tm, d), lambda i: (i, 0))]
    epilogue = None
    n_sub = 1
    if proj is not None:
        epilogue, w_qkv, extras, extra_specs, p_shapes, p_specs = proj
        n_sub = PROJ_SUB_BLOCKS if (tm >= 512 and epilogue is not _na_epilogue) else 1
        xs += [w_qkv] + extras
        x_specs += [_resident(w_qkv.shape, lambda i: (0, 0))] + extra_specs
        out_shape += p_shapes
        out_specs += p_specs
    res = pl.pallas_call(
        functools.partial(_ffn_kernel, layer=layer, half=half, split=split, mix=mixer_out is not None,
                          epilogue=epilogue, n_in=len(xs), n_out=len(out_shape), n_sub=n_sub),
        out_shape=out_shape,
        grid=(n_rows // tm,),
        in_specs=x_specs,
        out_specs=out_specs,
        scratch_shapes=scratch,
        compiler_params=_params(1),
        name=f"ffn_l{layer}h{half}",
    )(*xs)
    return res[0] if proj is None else res


def _rotate_half(x, half):
    if 2 * half == LANES:
        return pltpu.roll(x, half, axis=1)
    lane = lax.broadcasted_iota(jnp.int32, x.shape, 1)
    return jnp.where(lane % (2 * half) < half, pltpu.roll(x, LANES - half, axis=1), pltpu.roll(x, half, axis=1))


def _store_values_transposed(vt_ref, v, dh, rows):
    per = LANES // dh
    for j in range(v.shape[1] // LANES):
        t = v[:, j * LANES:(j + 1) * LANES].T
        for i in range(per):
            vt_ref[j * per + i, 0:dh, rows] = t[i * dh:(i + 1) * dh, :].astype(BF16)
    vt_ref[:, dh:, rows] = jnp.ones((vt_ref.shape[0], ONES_ROWS, v.shape[0]), BF16)


def _na_epilogue(p, extras, outs, rows):
    o_ref, vt_ref = outs
    qd = NA_HEADS * NA_HEAD_DIM
    o_ref[rows, :qd] = (p[:, :qd] * (NA_HEAD_DIM ** -0.5 * LOG2E)).astype(BF16)
    o_ref[rows, qd:] = p[:, qd:2 * qd].astype(BF16)
    _store_values_transposed(vt_ref, p[:, 2 * qd:], NA_HEAD_DIM, rows)


def _sw_epilogue(p, extras, outs, rows):
    cos_ref, sin_ref = extras
    o_ref, vt_ref = outs
    cos = cos_ref[rows, :]
    sin = sin_ref[rows, :]
    n_q = SW_HEADS * SW_HEAD_DIM // LANES
    n_k = SW_KV_HEADS * SW_HEAD_DIM // LANES
    for j in range(n_q + n_k):
        xs = p[:, j * LANES:(j + 1) * LANES]
        r = xs * cos + _rotate_half(xs, SW_HEAD_DIM // 2) * sin
        if j < n_q:
            r = r * (SW_HEAD_DIM ** -0.5 * LOG2E)
        o_ref[rows, j * LANES:(j + 1) * LANES] = r.astype(BF16)
    _store_values_transposed(vt_ref, p[:, (n_q + n_k) * LANES:], SW_HEAD_DIM, rows)


def _ga_epilogue(p, extras, outs, rows):
    cos_ref, sin_ref, qn_ref, kn_ref = extras
    o_ref, vt_ref = outs
    cos = cos_ref[rows, :]
    sin = sin_ref[rows, :]
    for j in range(GA_HEADS + GA_KV_HEADS):
        xs = _rms(p[:, j * LANES:(j + 1) * LANES])
        if j < GA_HEADS:
            xs = xs * (qn_ref[...] * (GA_HEAD_DIM ** -0.5 * LOG2E))
        else:
            xs = xs * kn_ref[...]
        r = xs * cos + _rotate_half(xs, GA_HEAD_DIM // 2) * sin
        o_ref[rows, j * LANES:(j + 1) * LANES] = r.astype(BF16)
    _store_values_transposed(vt_ref, p[:, (GA_HEADS + GA_KV_HEADS) * LANES:], GA_HEAD_DIM, rows)


def _cv_epilogue(p, extras, outs, rows):
    (b_ref,), (o_ref,) = extras, outs
    u = p + b_ref[...]
    d = o_ref.shape[1]
    o_ref[rows, :] = u[:, :d] * jax.nn.sigmoid(u[:, d:])


def _proj_outputs(n_rows, tm, n_out, out_dtype, vt_heads=0, vt_dim=0):
    shapes = [jax.ShapeDtypeStruct((n_rows, n_out), out_dtype)]
    specs = [pl.BlockSpec((tm, n_out), lambda i: (i, 0))]
    if vt_heads:
        vrows = vt_dim + ONES_ROWS
        shapes.append(jax.ShapeDtypeStruct((vt_heads, vrows, n_rows), BF16))
        specs.append(pl.BlockSpec((vt_heads, vrows, tm), lambda i: (0, 0, i)))
    return shapes, specs


def _rope_specs(tm, seq, n_lat_rows):
    n_seq_blocks = seq // tm
    idx = lambda i: (jnp.where(i * tm < n_lat_rows, i % n_seq_blocks, n_seq_blocks), 0)
    return [pl.BlockSpec((tm, LANES), idx), pl.BlockSpec((tm, LANES), idx)]


def _rope_tables(seq, head_dim, tm):
    t = np.arange(seq)
    row = (t // GRID_W).astype(np.float32)
    col = (t % GRID_W).astype(np.float32)
    n_pairs_axis = head_dim // 4
    freq = jnp.asarray(ROPE_THETA, F32) ** (-jnp.arange(n_pairs_axis, dtype=F32) / n_pairs_axis)
    ang = jnp.concatenate([row[:, None] * freq, col[:, None] * freq], axis=-1)
    cos, sin = jnp.cos(ang), jnp.sin(ang)
    reps = LANES // head_dim
    cos_t = jnp.tile(jnp.concatenate([cos, cos], axis=-1), (1, reps))
    sin_t = jnp.tile(jnp.concatenate([-sin, sin], axis=-1), (1, reps))
    cos_t = jnp.concatenate([cos_t, jnp.ones((tm, LANES), F32)], axis=0)
    sin_t = jnp.concatenate([sin_t, jnp.zeros((tm, LANES), F32)], axis=0)
    return cos_t, sin_t


def _half_split_perm(n_heads, head_dim):
    per_head = np.concatenate([np.arange(0, head_dim, 2), np.arange(1, head_dim, 2)])
    return (np.arange(n_heads)[:, None] * head_dim + per_head[None, :]).reshape(-1)


def _attend(q_ref, o_ref, *, n_heads, dh, lookahead, keys_of, values_of, addends_of, floor_of=None):
    per_store = LANES // dh

    def scores(hd):
        qh = q_ref[:, hd * dh:(hd + 1) * dh]
        ss = []
        for kb, adds in zip(keys_of(hd), addends_of(hd)):
            s = lax.dot_general(kb, qh, _DN_T, preferred_element_type=F32)
            for a in adds:
                s = s + a
            ss.append(s)
        return ss

    def finish(hd, ss):
        m = ss[0].max(axis=0, keepdims=True)
        for s in ss[1:]:
            m = jnp.maximum(m, s.max(axis=0, keepdims=True))
        floor = None if floor_of is None else floor_of(hd)
        if floor is not None:
            m = jnp.maximum(m, floor)
        ot = None
        for s, vb in zip(ss, values_of(hd)):
            part = jnp.dot(vb, jnp.exp2(s - m).astype(BF16), preferred_element_type=F32)
            ot = part if ot is None else ot + part
        l = ot[dh:dh + 1, :]
        if floor is not None:
            l = l + jnp.exp2(floor - m)
        return ot[:dh, :] / l

    pending = [scores(hd) for hd in range(min(lookahead, n_heads))]
    outs = []
    for hd in range(n_heads):
        ss = pending.pop(0)
        if hd + lookahead < n_heads:
            pending.append(scores(hd + lookahead))
        outs.append(finish(hd, ss))
        if len(outs) == per_store:
            o2 = outs[0] if per_store == 1 else jnp.concatenate(outs, axis=0)
            outs = []
            o_ref[:, (hd + 1 - per_store) * dh:(hd + 1) * dh] = o2.T.astype(BF16)


def _na_kernel(q_ref, kl_ref, kc_ref, vlt_ref, vct_ref, val_ref, mask_ref, o_ref, *, n_groups, seq):
    g = pl.program_id(1)
    dh = NA_HEAD_DIM
    tq = NA_QROWS * GRID_W
    n_keys = NA_KROWS * GRID_W
    n_edge = NA_WIN_R * GRID_W
    half = (NA_WIN_R // 2) * GRID_W
    hs = lambda hd: slice(hd * dh, (hd + 1) * dh)
    edge = (g == 0) | (g == n_groups - 1)

    @pl.when((g < n_groups) & jnp.logical_not(edge))
    def _():
        off = pl.multiple_of(g * tq - half, tq)
        _attend(q_ref, o_ref, n_heads=NA_HEADS, dh=dh, lookahead=NA_LOOKAHEAD,
                keys_of=lambda hd: [kl_ref[pl.ds(off, n_keys), hs(hd)], kc_ref[:, hs(hd)]],
                values_of=lambda hd: [vlt_ref[hd, :, pl.ds(off, n_keys)], vct_ref[hd]],
                addends_of=lambda hd: [[val_ref[hd], mask_ref[...]], []])

    @pl.when((g < n_groups) & edge)
    def _():
        koff = pl.multiple_of(jnp.where(g == 0, 0, seq - n_edge), tq)
        toff = pl.multiple_of(jnp.where(g == 0, half, 0), tq)
        _attend(q_ref, o_ref, n_heads=NA_HEADS, dh=dh, lookahead=NA_LOOKAHEAD,
                keys_of=lambda hd: [kl_ref[pl.ds(koff, n_edge), hs(hd)], kc_ref[:, hs(hd)]],
                values_of=lambda hd: [vlt_ref[hd, :, pl.ds(koff, n_edge)], vct_ref[hd]],
                addends_of=lambda hd: [[val_ref[hd, pl.ds(toff, n_edge), :], mask_ref[pl.ds(toff, n_edge), :]], []])

    @pl.when(g >= n_groups)
    def _():
        _attend(q_ref, o_ref, n_heads=NA_HEADS, dh=dh, lookahead=NA_LOOKAHEAD,
                keys_of=lambda hd: [kc_ref[:, hs(hd)]],
                values_of=lambda hd: [vct_ref[hd]],
                addends_of=lambda hd: [[]])


def _na_bias_tables(rpb, rows):
    qr = np.arange(NA_QROWS)
    kri = np.arange(NA_KROWS)
    j = np.arange(GRID_W)
    kc = np.arange(GRID_W)
    half = NA_WIN_R // 2
    ridx = np.clip(kri[:, None] - half - qr[None, :] + NA_WIN_R - 1, 0, 2 * NA_WIN_R - 2)
    cidx = np.clip(kc[:, None] - j[None, :] + NA_WIN_C - 1, 0, 2 * NA_WIN_C - 2)
    n_k, n_q = NA_KROWS * GRID_W, NA_QROWS * GRID_W
    n_dc = 2 * NA_WIN_C - 1
    col_hot = cidx[None] == np.arange(n_dc)[:, None, None]
    row_hot = np.eye(NA_QROWS, dtype=bool)
    onehot = (row_hot[:, None, None, :, None] & col_hot[None, :, :, None, :]).astype(np.float32)
    onehot = onehot.reshape(NA_QROWS * n_dc, GRID_W * n_q)
    rows_sel = (rpb.astype(F32)[:, ridx] * LOG2E).reshape(rpb.shape[0] * NA_KROWS, NA_QROWS * n_dc)
    val = jnp.dot(rows_sel, jnp.asarray(onehot), precision=lax.Precision.HIGHEST).reshape(rpb.shape[0], n_k, n_q)
    ws = np.clip(j - NA_WIN_C // 2, 0, GRID_W - NA_WIN_C)
    col_ok = (kc[:, None] >= ws[None, :]) & (kc[:, None] < ws[None, :] + NA_WIN_C)
    masks = []
    for r0 in (0, NA_QROWS, rows - NA_QROWS):
        r = r0 + qr
        rs = np.clip(r - half, 0, rows - NA_WIN_R)
        k_abs = r0 - half + kri
        row_ok = (k_abs[:, None] >= rs[None, :]) & (k_abs[:, None] < rs[None, :] + NA_WIN_R)
        ok = row_ok[:, None, :, None] & col_ok[None, :, None, :]
        masks.append(np.where(ok, 0.0, NEG).reshape(n_k, n_q))
    return val, jnp.asarray(np.stack(masks), F32)


def _na_attention(qk, vt, rpb, *, n_batch, seq, ctx_len, with_ctx_out):
    rows = seq // GRID_W
    qd = NA_HEADS * NA_HEAD_DIM
    tq = NA_QROWS * GRID_W
    n_keys = NA_KROWS * GRID_W
    n_groups = seq // tq
    ctx_tiles = ctx_len // tq
    n_ctx_tiles = ctx_tiles if with_ctx_out else 0
    n_lat = n_batch * seq
    n_rows = (n_lat + n_batch * ctx_len) if with_ctx_out else n_lat
    ctx_blk0 = n_lat // ctx_len
    vrows = NA_HEAD_DIM + ONES_ROWS
    val, mask = _na_bias_tables(rpb, rows)

    def q_idx(b, g):
        return (jnp.where(g < n_groups, b * n_groups + g, n_batch * n_groups + b * ctx_tiles + (g - n_groups)), 0)

    def mask_idx(b, g):
        return (jnp.where(g == 0, 0, jnp.where(g >= n_groups - 1, 2, 1)), 0, 0)

    return pl.pallas_call(
        functools.partial(_na_kernel, n_groups=n_groups, seq=seq),
        out_shape=jax.ShapeDtypeStruct((n_rows, qd), BF16),
        grid=(n_batch, n_groups + n_ctx_tiles),
        in_specs=[
            pl.BlockSpec((tq, qd), q_idx),
            pl.BlockSpec((seq, qd), lambda b, g: (b, 1)),
            pl.BlockSpec((ctx_len, qd), lambda b, g: (ctx_blk0 + b, 1)),
            pl.BlockSpec((NA_HEADS, vrows, seq), lambda b, g: (0, 0, b)),
            pl.BlockSpec((NA_HEADS, vrows, ctx_len), lambda b, g: (0, 0, ctx_blk0 + b)),
            _resident((NA_HEADS, n_keys, tq), lambda b, g: (0, 0, 0)),
            pl.BlockSpec((None, n_keys, tq), mask_idx),
        ],
        out_specs=pl.BlockSpec((tq, qd), q_idx),
        compiler_params=_params(2),
        name="na_attention",
    )(qk, qk, qk, vt, vt, val, mask)


def _sw_kernel(sink_ref, q_ref, kl_ref, kc_ref, vlt_ref, vct_ref, mask_ref, o_ref, *, n_tiles, seq):
    n = pl.program_id(1)
    dh = SW_HEAD_DIM
    grp = SW_HEADS // SW_KV_HEADS
    win = SW_KWIN
    ks = lambda hd: slice((hd // grp) * dh, (hd // grp + 1) * dh)
    floor_of = lambda hd: sink_ref[hd] * LOG2E

    @pl.when(n < n_tiles)
    def _():
        ws = pl.multiple_of(jnp.clip(n * SW_QTILE - SW_WINDOW, 0, seq - win), LANES)
        keys = [jnp.concatenate([kl_ref[pl.ds(ws, win), sl], kc_ref[:, sl]], axis=0)
                for sl in (ks(0), ks(grp))]
        vals = [jnp.concatenate([vlt_ref[kvh, :, pl.ds(ws, win)], vct_ref[kvh]], axis=1)
                for kvh in range(SW_KV_HEADS)]
        mask = jnp.concatenate([mask_ref[...], jnp.zeros((kc_ref.shape[0], SW_QTILE), F32)], axis=0)
        _attend(q_ref, o_ref, n_heads=SW_HEADS, dh=dh, lookahead=SW_LOOKAHEAD,
                keys_of=lambda hd: [keys[hd // grp]],
                values_of=lambda hd: [vals[hd // grp]],
                addends_of=lambda hd: [[mask]], floor_of=floor_of)

    @pl.when(n >= n_tiles)
    def _():
        _attend(q_ref, o_ref, n_heads=SW_HEADS, dh=dh, lookahead=SW_LOOKAHEAD,
                keys_of=lambda hd: [kc_ref[:, ks(hd)]],
                values_of=lambda hd: [vct_ref[hd // grp]],
                addends_of=lambda hd: [[]], floor_of=floor_of)


def _sw_attention(proj, vt, sink, *, n_batch, seq, ctx_len, with_ctx_out):
    qd = SW_HEADS * SW_HEAD_DIM
    tq = SW_QTILE
    win = SW_KWIN
    n_tiles = seq // tq
    ctx_tiles = ctx_len // tq
    n_ctx_tiles = ctx_tiles if with_ctx_out else 0
    lat_tiles = n_batch * n_tiles
    n_lat = n_batch * seq
    n_rows = (n_lat + n_batch * ctx_len) if with_ctx_out else n_lat
    k_col = qd // LANES
    ctx_blk0 = n_lat // ctx_len
    vrows = SW_HEAD_DIM + ONES_ROWS
    ki = np.arange(win)[:, None]
    qj = np.arange(tq)[None, :]
    mask = np.stack([np.where(np.abs(ki - qj - off) <= SW_WINDOW, 0.0, NEG) for off in (0, SW_WINDOW, 2 * SW_WINDOW)])
    mask = jnp.asarray(mask, F32)

    def q_idx(b, n, s):
        return (jnp.where(n < n_tiles, b * n_tiles + n, lat_tiles + b * ctx_tiles + (n - n_tiles)), 0)

    def mask_idx(b, n, s):
        nn = jnp.minimum(n, n_tiles - 1)
        ws_blk = jnp.clip(nn * (tq // SW_WINDOW) - 1, 0, (seq - win) // SW_WINDOW)
        return (nn * (tq // SW_WINDOW) - ws_blk, 0, 0)

    grid_spec = pltpu.PrefetchScalarGridSpec(
        num_scalar_prefetch=1,
        grid=(n_batch, n_tiles + n_ctx_tiles),
        in_specs=[
            pl.BlockSpec((tq, qd), q_idx),
            pl.BlockSpec((seq, LANES), lambda b, n, s: (b, k_col)),
            pl.BlockSpec((ctx_len, LANES), lambda b, n, s: (ctx_blk0 + b, k_col)),
            pl.BlockSpec((SW_KV_HEADS, vrows, seq), lambda b, n, s: (0, 0, b)),
            pl.BlockSpec((SW_KV_HEADS, vrows, ctx_len), lambda b, n, s: (0, 0, ctx_blk0 + b)),
            pl.BlockSpec((None, win, tq), mask_idx),
        ],
        out_specs=pl.BlockSpec((tq, qd), q_idx),
    )
    return pl.pallas_call(
        functools.partial(_sw_kernel, n_tiles=n_tiles, seq=seq),
        out_shape=jax.ShapeDtypeStruct((n_rows, qd), BF16),
        grid_spec=grid_spec,
        compiler_params=_params(2),
        name="sw_attention",
    )(sink.astype(F32), proj, proj, proj, vt, vt, mask)


def _ga_kernel(q_ref, kl_ref, kc_ref, vlt_ref, vct_ref, o_ref):
    dh = GA_HEAD_DIM
    grp = GA_HEADS // GA_KV_HEADS
    ks = lambda hd: slice((hd // grp) * dh, (hd // grp + 1) * dh)
    _attend(q_ref, o_ref, n_heads=GA_HEADS, dh=dh, lookahead=GA_LOOKAHEAD,
            keys_of=lambda hd: [kl_ref[:, ks(hd)], kc_ref[:, ks(hd)]],
            values_of=lambda hd: [vlt_ref[hd // grp], vct_ref[hd // grp]],
            addends_of=lambda hd: [[], []])


def _ga_attention(proj, vt, *, n_batch, seq, ctx_len):
    qd = GA_HEADS * GA_HEAD_DIM
    kvd = GA_KV_HEADS * GA_HEAD_DIM
    tq = GA_QTILE
    n_tiles = seq // tq
    n_lat = n_batch * seq
    k_col = qd // kvd
    ctx_blk0 = n_lat // ctx_len
    vrows = GA_HEAD_DIM + ONES_ROWS
    return pl.pallas_call(
        _ga_kernel,
        out_shape=jax.ShapeDtypeStruct((n_lat, qd), BF16),
        grid=(n_batch, n_tiles),
        in_specs=[
            pl.BlockSpec((tq, qd), lambda b, n: (b * n_tiles + n, 0)),
            pl.BlockSpec((seq, kvd), lambda b, n: (b, k_col)),
            pl.BlockSpec((ctx_len, kvd), lambda b, n: (ctx_blk0 + b, k_col)),
            pl.BlockSpec((GA_KV_HEADS, vrows, seq), lambda b, n: (0, 0, b)),
            pl.BlockSpec((GA_KV_HEADS, vrows, ctx_len), lambda b, n: (0, 0, ctx_blk0 + b)),
        ],
        out_specs=pl.BlockSpec((tq, qd), lambda b, n: (b * n_tiles + n, 0)),
        compiler_params=_params(2),
        name="ga_attention",
    )(proj, proj, proj, vt, vt)


def _conv_kernel(um_ref, up_ref, un_ref, w_ref, b_ref, lg_ref, lb_ref, o_ref, win_ref, sh_ref, acc_ref, *,
                 lat_tiles, ctx_tiles):
    j = pl.program_id(1)
    jc = j - lat_tiles
    first = jnp.where(j < lat_tiles, j == 0, jc == 0)
    last = jnp.where(j < lat_tiles, j == lat_tiles - 1, jc == ctx_tiles - 1)
    tl = um_ref.shape[0]
    d = um_ref.shape[1]
    sh_rows = sh_ref.shape[1]
    win_ref[0:CONV_HALO, :] = jnp.where(first, 0.0, up_ref[...])
    win_ref[CONV_HALO:CONV_HALO + tl, :] = um_ref[...]
    win_ref[CONV_HALO + tl:, :] = jnp.where(last, 0.0, un_ref[...])
    base = CONV_HALO - CONV_WIDTH // 2

    def lane_slab(s, carry):
        ls = pl.ds(pl.multiple_of(s * LANES, LANES), LANES)
        for r in range(1, SUBLANES):
            sh_ref[r] = win_ref[r:r + sh_rows, ls]
        for rb in range(tl // CONV_ROWS):
            acc = jnp.broadcast_to(b_ref[:, ls], (CONV_ROWS, LANES))
            for k in range(CONV_WIDTH):
                a8, r = divmod(base + k, SUBLANES)
                lo = a8 * SUBLANES + rb * CONV_ROWS
                src = win_ref[lo:lo + CONV_ROWS, ls] if r == 0 else sh_ref[r, lo:lo + CONV_ROWS, :]
                acc = acc + w_ref[k:k + 1, ls] * src
            acc_ref[rb * CONV_ROWS:(rb + 1) * CONV_ROWS, ls] = acc
        return carry

    lax.fori_loop(0, d // LANES, lane_slab, 0)
    u = acc_ref[...]
    mu = jnp.mean(u, axis=-1, keepdims=True)
    var = jnp.mean(jnp.square(u - mu), axis=-1, keepdims=True)
    y = (u - mu) * lax.rsqrt(var + EPS) * lg_ref[...] + lb_ref[...]
    o_ref[...] = _silu(y).astype(BF16)


def _conv_core(u, w_dw, b_dw, ln_g, ln_b, *, n_batch, seq, ctx_len, with_ctx_out):
    d = u.shape[1]
    tl = CONV_TILE
    lat_tiles = seq // tl
    ctx_tiles = ctx_len // tl
    n_ctx = ctx_tiles if with_ctx_out else 0
    n_rows = (n_batch * seq + n_batch * ctx_len) if with_ctx_out else n_batch * seq
    halo_per_tile = tl // CONV_HALO
    n_halo_blocks = u.shape[0] // CONV_HALO
    sh_rows = tl + 2 * CONV_HALO - SUBLANES

    def main(b, j):
        return jnp.where(j < lat_tiles, b * lat_tiles + j, n_batch * lat_tiles + b * ctx_tiles + (j - lat_tiles))

    return pl.pallas_call(
        functools.partial(_conv_kernel, lat_tiles=lat_tiles, ctx_tiles=ctx_tiles),
        out_shape=jax.ShapeDtypeStruct((n_rows, d), BF16),
        grid=(n_batch, lat_tiles + n_ctx),
        in_specs=[
            pl.BlockSpec((tl, d), lambda b, j: (main(b, j), 0)),
            pl.BlockSpec((CONV_HALO, d), lambda b, j: (jnp.maximum(main(b, j) * halo_per_tile - 1, 0), 0)),
            pl.BlockSpec((CONV_HALO, d),
                         lambda b, j: (jnp.minimum((main(b, j) + 1) * halo_per_tile, n_halo_blocks - 1), 0)),
            pl.BlockSpec((CONV_WIDTH, d), lambda b, j: (0, 0)),
            pl.BlockSpec((1, d), lambda b, j: (0, 0)),
            pl.BlockSpec((1, d), lambda b, j: (0, 0)),
            pl.BlockSpec((1, d), lambda b, j: (0, 0)),
        ],
        out_specs=pl.BlockSpec((tl, d), lambda b, j: (main(b, j), 0)),
        scratch_shapes=[pltpu.VMEM((tl + 2 * CONV_HALO, d), F32), pltpu.VMEM((SUBLANES, sh_rows, LANES), F32),
                        pltpu.VMEM((tl, d), F32)],
        compiler_params=_params(2),
        name="conv_core",
    )(u, u, u, w_dw, b_dw.reshape(1, d), ln_g.reshape(1, d), ln_b.reshape(1, d))


def kernel(x, c, ctx, c_ctx, mod_w, mod_b, norm_g, ffn_w_gate, ffn_w_up, ffn_w_down, na_w_qkv, na_w_o, na_rpb, sw_w_qkv, sw_w_o, sw_sink, cv_w_pw1, cv_b_pw1, cv_w_dw, cv_b_dw, cv_ln_g, cv_ln_b, cv_w_pw2, cv_b_pw2, ga_w_qkv, ga_w_o, ga_q_norm, ga_k_norm):
    n_batch, seq, d = x.shape
    ctx_len = ctx.shape[1]
    depth = mod_w.shape[0]
    n_lat = n_batch * seq
    n_ctx = n_batch * ctx_len
    n_all = n_lat + n_ctx
    assert d == D_MODEL and n_batch < MOD_ROWS and seq % CONV_TILE == 0 and ctx_len % CONV_TILE == 0
    assert seq % ctx_len == 0 and seq >= SW_KWIN and seq % SW_QTILE == 0 and ctx_len % SW_QTILE == 0
    assert (seq // GRID_W) % NA_QROWS == 0 and (seq // GRID_W) >= NA_KROWS and seq % GA_QTILE == 0
    tm = _row_tile(n_lat, n_ctx, seq)
    common = dict(tm=tm, seq=seq, n_batch=n_batch)

    c_all = jnp.concatenate([c, c_ctx[None, :], jnp.zeros((MOD_ROWS - n_batch - 1, d), F32)], axis=0)
    mod = _modulation(c_all, mod_w, mod_b).reshape(depth, MOD_ROWS, N_MOD, d)

    wg, wu, wd = ffn_w_gate, ffn_w_up, ffn_w_down
    zero_bias = jnp.zeros((1, d), F32)

    xs = x.reshape(n_lat, d)
    for i in range(depth):
        kind, j = i % N_MIXERS, i // N_MIXERS
        last = i == depth - 1
        ctx_in = not (last and kind == 2)
        ctx_out = not last
        rows_in = n_all if ctx_in else n_lat
        rows_out = n_all if ctx_out else n_lat
        if kind == 0:
            qd = NA_HEADS * NA_HEAD_DIM
            proj = (_na_epilogue, na_w_qkv[j].astype(BF16), [], [],
                    *_proj_outputs(rows_in, tm, 2 * qd, BF16, NA_HEADS, NA_HEAD_DIM))
        elif kind == 1:
            qd, kvd = SW_HEADS * SW_HEAD_DIM, SW_KV_HEADS * SW_HEAD_DIM
            perm = np.concatenate([_half_split_perm(SW_HEADS, SW_HEAD_DIM),
                                   qd + _half_split_perm(SW_KV_HEADS, SW_HEAD_DIM),
                                   np.arange(qd + kvd, qd + 2 * kvd)])
            cos_t, sin_t = _rope_tables(seq, SW_HEAD_DIM, tm)
            proj = (_sw_epilogue, sw_w_qkv[j][:, perm].astype(BF16), [cos_t, sin_t], _rope_specs(tm, seq, n_lat),
                    *_proj_outputs(rows_in, tm, qd + kvd, BF16, SW_KV_HEADS, SW_HEAD_DIM))
        elif kind == 2:
            proj = (_cv_epilogue, cv_w_pw1[j].astype(BF16), [cv_b_pw1[j].reshape(1, -1)],
                    [pl.BlockSpec((1, 2 * d), lambda t: (0, 0))], *_proj_outputs(rows_in, tm, d, F32))
        else:
            assert not ctx_out, "context outputs of the global mixer are not implemented"
            qd, kvd = GA_HEADS * GA_HEAD_DIM, GA_KV_HEADS * GA_HEAD_DIM
            perm = np.concatenate([_half_split_perm(GA_HEADS, GA_HEAD_DIM),
                                   qd + _half_split_perm(GA_KV_HEADS, GA_HEAD_DIM),
                                   np.arange(qd + kvd, qd + 2 * kvd)])
            hperm = _half_split_perm(1, GA_HEAD_DIM)
            cos_t, sin_t = _rope_tables(seq, GA_HEAD_DIM, tm)
            vec = pl.BlockSpec((1, GA_HEAD_DIM), lambda t: (0, 0))
            proj = (_ga_epilogue, ga_w_qkv[j][:, perm].astype(BF16),
                    [cos_t, sin_t, ga_q_norm[j][hperm].reshape(1, -1), ga_k_norm[j][hperm].reshape(1, -1)],
                    _rope_specs(tm, seq, n_lat) + [vec, vec],
                    *_proj_outputs(rows_in, tm, qd + kvd, BF16, GA_KV_HEADS, GA_HEAD_DIM))
        x2 = ctx.reshape(n_ctx, d) if (i == 0 and ctx_in) else None
        xs, *pj = _ffn(xs, mod, norm_g, wg, wu, wd, layer=i, half=0, n_rows=rows_in, **common, x2=x2, proj=proj)
        if kind == 0:
            a = _na_attention(*pj, na_rpb[j], n_batch=n_batch, seq=seq, ctx_len=ctx_len, with_ctx_out=ctx_out)
            w_o, b_o = na_w_o[j], zero_bias
        elif kind == 1:
            a = _sw_attention(*pj, sw_sink[j], n_batch=n_batch, seq=seq, ctx_len=ctx_len, with_ctx_out=ctx_out)
            w_o, b_o = sw_w_o[j], zero_bias
        elif kind == 2:
            a = _conv_core(*pj, cv_w_dw[j], cv_b_dw[j], cv_ln_g[j], cv_ln_b[j], n_batch=n_batch, seq=seq,
                           ctx_len=ctx_len, with_ctx_out=ctx_out)
            w_o, b_o = cv_w_pw2[j], cv_b_pw2[j].reshape(1, d)
        else:
            a = _ga_attention(*pj, n_batch=n_batch, seq=seq, ctx_len=ctx_len)
            w_o, b_o = ga_w_o[j], zero_bias
        xs = _ffn(xs, mod, norm_g, wg, wu, wd, layer=i, half=1, n_rows=rows_out, **common,
                  mixer_out=(a, w_o.astype(BF16), b_o))
    return xs[:n_lat].reshape(n_batch, seq, d)
```
